```python
import math
import jax, jax.numpy as jnp
from jax import lax
import numpy as np

D_MODEL = 1024
BATCH = 2
SEQ = 8192
DEPTH = 1
DEC_BATCH = 4
DEC_SEQ = 8192
PAST_LEN = 128

MLA_HEADS = 8
MLA_Q_LORA = 384
MLA_KV_LORA = 256
MLA_NOPE = 64
MLA_ROPE = 32
MLA_QK = MLA_NOPE + MLA_ROPE
MLA_V = 64
ROPE_THETA = 10000.0
Q_BLOCK = 128
DN_HEADS = 4
DN_DK = 128
DN_DV = 128
DN_CONV = 5
DN_CHUNK = 64
DN_QKV = DN_HEADS * (2 * DN_DK + DN_DV)
DN_Z = DN_HEADS * DN_DV
DN_AB = 4 * DN_HEADS
IN_WIDTH = MLA_Q_LORA + MLA_KV_LORA + MLA_ROPE + DN_QKV + DN_Z + DN_AB
MIX_WIDTH = MLA_HEADS * MLA_V + DN_HEADS * DN_DV
N_EXPERTS = 32
TOP_K = 4
D_FF = 1024
SWIGLU_LIMIT = 7.0
SWIGLU_ALPHA = 1.702
MOE_BLOCK = 256
EPS = 1e-6

kernel_name = "hymba_mla_gdn_moe_adaln_encoder"


def _rms(x, g):
    xf = x.astype(jnp.float32)
    y = xf * lax.rsqrt(jnp.mean(xf * xf, axis=-1, keepdims=True) + EPS)
    return y.astype(x.dtype) * g


def _l2norm(x):
    xf = x.astype(jnp.float32)
    return xf * lax.rsqrt(jnp.sum(xf * xf, axis=-1, keepdims=True) + EPS)


def _rope(x, pos):
    half = x.shape[-1] // 2
    freq = ROPE_THETA ** (-jnp.arange(half, dtype=jnp.float32) / half)
    ang = pos.astype(jnp.float32)[:, None] * freq[None, :]
    cos = jnp.cos(ang)[None, :, None, :]
    sin = jnp.sin(ang)[None, :, None, :]
    x1, x2 = x[..., :half], x[..., half:]
    return jnp.concatenate([x1 * cos - x2 * sin, x2 * cos + x1 * sin], axis=-1).astype(x.dtype)


def _block_attention(q, k, v):
    B, S, H, Dq = q.shape
    Dv = v.shape[-1]
    nb = S // Q_BLOCK
    scale = Dq ** -0.5
    qb = q.reshape(B, nb, Q_BLOCK, H, Dq).swapaxes(0, 1)

    def one(qi):
        s = jnp.einsum('bqhd,bkhd->bhqk', qi, k).astype(jnp.float32) * scale
        p = jax.nn.softmax(s, axis=-1)
        return jnp.einsum('bhqk,bkhv->bqhv', p.astype(v.dtype), v)

    o = lax.map(one, qb)
    return o.swapaxes(0, 1).reshape(B, S, H, Dv)


def _centred_dwconv(x, w):
    C = x.shape[-1]
    pad = (DN_CONV - 1) // 2
    return lax.conv_general_dilated(
        x, w[:, None, :], window_strides=(1,), padding=[(pad, pad)],
        dimension_numbers=('NWC', 'WIO', 'NWC'), feature_group_count=C)


def _gated_delta_chunked(q, k, v, g, beta):
    out_dtype = v.dtype
    B, S, H, Dk = q.shape
    Dv = v.shape[-1]
    C = DN_CHUNK
    N = S // C
    f32 = jnp.float32

    def chunks(t):
        return t.astype(f32).reshape((B, N, C, H) + t.shape[3:]).swapaxes(2, 3)

    qc, kc, vc = chunks(q), chunks(k), chunks(v)
    gc = jnp.cumsum(chunks(g), axis=-1)
    bc = chunks(beta)
    k_beta = kc * bc[..., None]
    v_beta = vc * bc[..., None]
    tril = jnp.tril(jnp.ones((C, C), bool))
    tril_strict = jnp.tril(jnp.ones((C, C), bool), -1)
    decay = jnp.exp(jnp.where(tril, gc[..., :, None] - gc[..., None, :], -jnp.inf))
    L = jnp.where(tril_strict, jnp.einsum('bnhcd,bnhkd->bnhck', k_beta, kc) * decay, 0.0)
    rhs = jnp.concatenate([v_beta, k_beta * jnp.exp(gc)[..., None]], axis=-1)
    sol = lax.linalg.triangular_solve(L, rhs, left_side=True, lower=True, unit_diagonal=True)
    u, w = sol[..., :Dv], sol[..., Dv:]
    a_intra = jnp.einsum('bnhcd,bnhkd->bnhck', qc, kc) * decay

    def step(state, inp):
        q_i, k_i, u_i, w_i, g_i, a_i = inp
        v_new = u_i - jnp.einsum('bhck,bhkv->bhcv', w_i, state)
        o_i = (jnp.einsum('bhck,bhkv->bhcv', q_i * jnp.exp(g_i)[..., None], state)
               + jnp.einsum('bhcj,bhjv->bhcv', a_i, v_new))
        g_last = g_i[..., -1]
        k_dec = k_i * jnp.exp(g_last[..., None] - g_i)[..., None]
        state = state * jnp.exp(g_last)[..., None, None] + jnp.einsum('bhck,bhcv->bhkv', k_dec, v_new)
        return state, o_i

    xs = tuple(jnp.moveaxis(t, 1, 0) for t in (qc, kc, u, w, gc, a_intra))
    _, o = lax.scan(step, jnp.zeros((B, H, Dk, Dv), f32), xs)
    return o.transpose(1, 0, 3, 2, 4).reshape(B, S, H, Dv).astype(out_dtype)


def _moe(h, p):
    T, D = h.shape
    logits = (h @ p['w_router']).astype(jnp.float32) + p['b_router'].astype(jnp.float32)
    top_val, top_idx = lax.top_k(logits, TOP_K)
    gates = jax.nn.softmax(top_val, axis=-1)
    TK = T * TOP_K
    flat_e = top_idx.reshape(-1).astype(jnp.int32)
    order = jnp.argsort(flat_e)
    sorted_e = flat_e[order]
    counts = jnp.bincount(flat_e, length=N_EXPERTS).astype(jnp.int32)
    padded = (counts + MOE_BLOCK - 1) // MOE_BLOCK * MOE_BLOCK
    start = jnp.cumsum(counts) - counts
    cum_padded = jnp.cumsum(padded)
    pstart = cum_padded - padded
    dest = pstart[sorted_e] + jnp.arange(TK, dtype=jnp.int32) - start[sorted_e]
    n_blocks = -(-TK // MOE_BLOCK) + N_EXPERTS
    P = n_blocks * MOE_BLOCK
    token_sorted = (order // TOP_K).astype(jnp.int32)
    slot_token = jnp.full((P,), T, jnp.int32).at[dest].set(token_sorted)
    block_expert = jnp.minimum(
        jnp.searchsorted(cum_padded, jnp.arange(n_blocks, dtype=jnp.int32) * MOE_BLOCK, side='right'),
        N_EXPERTS - 1)
    h_pad = jnp.concatenate([h, jnp.zeros((1, D), h.dtype)], axis=0)
    xb = h_pad[slot_token].reshape(n_blocks, MOE_BLOCK, D)

    def expert_block(args):
        xe, e = args
        gt = jnp.minimum(xe @ p['w_gate'][e] + p['b_gate'][e], SWIGLU_LIMIT)
        up = jnp.clip(xe @ p['w_up'][e] + p['b_up'][e], -SWIGLU_LIMIT, SWIGLU_LIMIT)
        act = (up + 1.0) * gt * jax.nn.sigmoid(SWIGLU_ALPHA * gt)
        return act @ p['w_down'][e] + p['b_down'][e]

    yb = lax.map(expert_block, (xb, block_expert)).reshape(P, D)
    y_sorted = yb[dest]
    gate_sorted = gates.reshape(-1)[order].astype(y_sorted.dtype)
    return jnp.zeros((T, D), y_sorted.dtype).at[token_sorted].add(gate_sorted[:, None] * y_sorted)


def _layer(x, c, p):
    B, S, D = x.shape
    f32 = jnp.float32
    mod = (jax.nn.silu(c) @ p['w_ada'] + p['b_ada'])[:, None, :]
    sh1, sc1, gt1, sh2, sc2, gt2 = jnp.split(mod, 6, axis=-1)
    h = _rms(x, p['norm1']) * (1 + sc1) + sh1
    proj = h @ p['w_in']
    i0 = MLA_Q_LORA
    i1 = i0 + MLA_KV_LORA
    i2 = i1 + MLA_ROPE
    i3 = i2 + DN_QKV
    i4 = i3 + DN_Z
    q_a, kv_a, k_rope, dn_qkv, dn_z, dn_ab = jnp.split(proj, [i0, i1, i2, i3, i4], axis=-1)
    pos = jnp.arange(S)

    q = (_rms(q_a, p['q_a_norm']) @ p['w_q_b']).reshape(B, S, MLA_HEADS, MLA_QK)
    kv = (_rms(kv_a, p['kv_a_norm']) @ p['w_kv_b']).reshape(B, S, MLA_HEADS, MLA_NOPE + MLA_V)
    k = jnp.concatenate(
        [kv[..., :MLA_NOPE], jnp.broadcast_to(k_rope[:, :, None, :], (B, S, MLA_HEADS, MLA_ROPE))], axis=-1)
    v = kv[..., MLA_NOPE:]
    q = _rms(q, p['q_norm'])
    k = _rms(k, p['k_norm'])
    q = jnp.concatenate([q[..., :MLA_NOPE], _rope(q[..., MLA_NOPE:], pos)], axis=-1)
    k = jnp.concatenate([k[..., :MLA_NOPE], _rope(k[..., MLA_NOPE:], pos)], axis=-1)
    o_mla = _block_attention(q, k, v).reshape(B, S, MLA_HEADS * MLA_V)

    qkv = jax.nn.silu(_centred_dwconv(dn_qkv, p['dn_conv']))
    dq, dk, dv = jnp.split(qkv, [DN_HEADS * DN_DK, 2 * DN_HEADS * DN_DK], axis=-1)
    dq = _l2norm(dq.reshape(B, S, DN_HEADS, DN_DK)) * (DN_DK ** -0.5)
    dk = _l2norm(dk.reshape(B, S, DN_HEADS, DN_DK))
    dv = dv.reshape(B, S, DN_HEADS, DN_DV)
    a_f, a_b, b_f, b_b = jnp.split(dn_ab.astype(f32), 4, axis=-1)
    a_log = p['dn_a_log'].astype(f32)
    dt_bias = p['dn_dt_bias'].astype(f32)
    g_f = -jnp.exp(a_log[0]) * jax.nn.softplus(a_f + dt_bias[0])
    g_b = -jnp.exp(a_log[1]) * jax.nn.softplus(a_b + dt_bias[1])
    o_f = _gated_delta_chunked(dq, dk, dv, g_f, jax.nn.sigmoid(b_f))
    flip = lambda t: jnp.flip(t, axis=1)
    o_b = flip(_gated_delta_chunked(flip(dq), flip(dk), flip(dv), flip(g_b), flip(jax.nn.sigmoid(b_b))))
    o_dn = _rms(o_f + o_b, p['dn_out_norm']) * jax.nn.silu(dn_z.reshape(B, S, DN_HEADS, DN_DV)).astype(o_f.dtype)

    mixed = jnp.concatenate([o_mla, o_dn.reshape(B, S, DN_HEADS * DN_DV).astype(o_mla.dtype)], axis=-1) @ p['w_o']
    x = x + gt1 * mixed

    h2 = _rms(x, p['norm2']) * (1 + sc2) + sh2
    x = x + gt2 * _moe(h2.reshape(B * S, D), p).reshape(B, S, D)
    return x


def setup_inputs(seed: int = 0) -> dict:
    key = jax.random.key(seed)
    ks = jax.random.split(key, 32)
    f32 = jnp.float32
    L, D, E, F = DEPTH, D_MODEL, N_EXPERTS, D_FF

    def nrm(k, shape, scale):
        return jax.random.normal(k, shape, f32) * scale

    def gain(k, shape):
        return 1.0 + 0.01 * jax.random.normal(k, shape, f32)

    dt = jnp.exp(jax.random.uniform(ks[20], (L, 2, DN_HEADS), f32, math.log(1e-3), math.log(1e-1)))
    dn_dt_bias = dt + jnp.log(-jnp.expm1(-dt))
    dn_a_log = jnp.log(jax.random.uniform(ks[21], (L, 2, DN_HEADS), f32, 1.0, 16.0))
    return {
        'x_prompt': jax.random.normal(ks[0], (BATCH, SEQ, D), f32),
        'x_sample': jax.random.normal(ks[1], (DEC_BATCH, DEC_SEQ, D), f32),
        'c_prompt': jax.random.normal(ks[2], (BATCH, D), f32),
        'c_sample': jax.random.normal(ks[3], (DEC_BATCH, D), f32),
        'w_ada': nrm(ks[4], (L, D, 6 * D), 0.5 * D ** -0.5),
        'b_ada': nrm(ks[5], (L, 6 * D), 0.01),
        'norm1': gain(ks[6], (L, D)),
        'w_in': nrm(ks[7], (L, D, IN_WIDTH), D ** -0.5),
        'q_a_norm': gain(ks[8], (L, MLA_Q_LORA)),
        'w_q_b': nrm(ks[9], (L, MLA_Q_LORA, MLA_HEADS * MLA_QK), MLA_Q_LORA ** -0.5),
        'kv_a_norm': gain(ks[10], (L, MLA_KV_LORA)),
        'w_kv_b': nrm(ks[11], (L, MLA_KV_LORA, MLA_HEADS * (MLA_NOPE + MLA_V)), MLA_KV_LORA ** -0.5),
        'q_norm': gain(ks[12], (L, MLA_QK)),
        'k_norm': gain(ks[13], (L, MLA_QK)),
        'dn_conv': nrm(ks[14], (L, DN_CONV, DN_QKV), DN_CONV ** -0.5),
        'dn_a_log': dn_a_log,
        'dn_dt_bias': dn_dt_bias,
        'dn_out_norm': gain(ks[15], (L, DN_DV)),
        'w_o': nrm(ks[16], (L, MIX_WIDTH, D), MIX_WIDTH ** -0.5),
        'norm2': gain(ks[17], (L, D)),
        'w_router': nrm(ks[18], (L, D, E), D ** -0.5),
        'b_router': nrm(ks[19], (L, E), 0.01),
        'w_gate': nrm(ks[22], (L, E, D, F), D ** -0.5),
        'b_gate': nrm(ks[23], (L, E, F), 0.01),
        'w_up': nrm(ks[24], (L, E, D, F), D ** -0.5),
        'b_up': nrm(ks[25], (L, E, F), 0.01),
        'w_down': nrm(ks[26], (L, E, F, D), F ** -0.5),
        'b_down': nrm(ks[27], (L, E, D), 0.01),
    }


def reference(x_prompt, x_sample, c_prompt, c_sample, w_ada, b_ada, norm1, w_in, q_a_norm, w_q_b,
              kv_a_norm, w_kv_b, q_norm, k_norm, dn_conv, dn_a_log, dn_dt_bias, dn_out_norm, w_o,
              norm2, w_router, b_router, w_gate, b_gate, w_up, b_up, w_down, b_down):
    y_prompt = x_prompt
    y_sample = x_sample
    for l in range(DEPTH):
        p = {
            'w_ada': w_ada[l], 'b_ada': b_ada[l], 'norm1': norm1[l], 'w_in': w_in[l],
            'q_a_norm': q_a_norm[l], 'w_q_b': w_q_b[l], 'kv_a_norm': kv_a_norm[l], 'w_kv_b': w_kv_b[l],
            'q_norm': q_norm[l], 'k_norm': k_norm[l], 'dn_conv': dn_conv[l], 'dn_a_log': dn_a_log[l],
            'dn_dt_bias': dn_dt_bias[l], 'dn_out_norm': dn_out_norm[l], 'w_o': w_o[l], 'norm2': norm2[l],
            'w_router': w_router[l], 'b_router': b_router[l], 'w_gate': w_gate[l], 'b_gate': b_gate[l],
            'w_up': w_up[l], 'b_up': b_up[l], 'w_down': w_down[l], 'b_down': b_down[l],
        }
        y_prompt = _layer(y_prompt, c_prompt, p)
        y_sample = _layer(y_sample, c_sample, p)
    return (y_prompt, y_sample)
```

```python
import functools
import math

import jax
import jax.numpy as jnp
from jax import lax
from jax.experimental import pallas as pl
from jax.experimental.pallas import tpu as pltpu

f32 = jnp.float32
bf16 = jnp.bfloat16
u32 = jnp.uint32
i32 = jnp.int32

LANES = 128
VMEM_LIMIT = 56 * 1024 * 1024

MLA_HEADS = 8
MLA_Q_LORA = 384
MLA_KV_LORA = 256
MLA_NOPE = 64
MLA_ROPE = 32
MLA_QK = MLA_NOPE + MLA_ROPE
MLA_V = 64
ROPE_THETA = 10000.0
DN_HEADS = 4
DN_DK = 128
DN_DV = 128
DN_CONV = 5
DN_CHUNK = 64
DN_SUB = 16
N_EXPERTS = 32
TOP_K = 4
SWIGLU_LIMIT = 7.0
SWIGLU_ALPHA = 1.702
MOE_BLOCK = 256
EPS = 1e-6

TM_IN = 256
TM_OUT = 256
TQ = 256
TK = 512
GDN_ROWS = 128
TT_DISPATCH = 512
TT_COMBINE = 256


def _cparams(sem):
    return pltpu.CompilerParams(dimension_semantics=sem, vmem_limit_bytes=VMEM_LIMIT)


def _split2(x):
    hi = x.astype(bf16)
    lo = (x - hi.astype(f32)).astype(bf16)
    return hi, lo


def _split3(x):
    hi = x.astype(bf16)
    r = x - hi.astype(f32)
    mid = r.astype(bf16)
    lo = (r - mid.astype(f32)).astype(bf16)
    return hi, mid, lo


def _dot(a, b):
    return jnp.dot(a, b, preferred_element_type=f32)


def _dot_nt(a, b):
    return lax.dot_general(a, b, (((1,), (1,)), ((), ())), preferred_element_type=f32)


def _dot3(a, b):
    ah, al = _split2(a)
    bh, bl = _split2(b)
    return _dot(ah, bh) + (_dot(al, bh) + _dot(ah, bl))


def _sigmoid(x):
    return 1.0 / (1.0 + jnp.exp(-x))


def _silu(x):
    return x * _sigmoid(x)


def _softplus(x):
    return jnp.maximum(x, 0.0) + jnp.log(1.0 + jnp.exp(-jnp.abs(x)))


def _ada_kernel(c_ref, w_ref, b_ref, o_ref):
    c = c_ref[...]
    o_ref[...] = _dot3(_silu(c), w_ref[...]) + b_ref[...]


def _ada(c, w_ada, b_ada):
    B, D = c.shape
    N = w_ada.shape[1]
    cp = jnp.zeros((8, D), f32).at[:B].set(c)
    tn = 1024
    out = pl.pallas_call(
        _ada_kernel,
        grid=(N // tn,),
        in_specs=[
            pl.BlockSpec((8, D), lambda j: (0, 0)),
            pl.BlockSpec((D, tn), lambda j: (0, j)),
            pl.BlockSpec((1, tn), lambda j: (0, j)),
        ],
        out_specs=pl.BlockSpec((8, tn), lambda j: (0, j)),
        out_shape=jax.ShapeDtypeStruct((8, N), f32),
        compiler_params=_cparams(("parallel",)),
        name="ada",
    )(cp, w_ada, b_ada.reshape(1, N))
    return out[:B]


def _chunk_tri(n, lower):
    r = lax.broadcasted_iota(i32, (n, n), 0)
    c = lax.broadcasted_iota(i32, (n, n), 1)
    same = (r // DN_CHUNK) == (c // DN_CHUNK)
    tri = (c <= r) if lower else (c >= r)
    return jnp.where(same & tri, 1.0, 0.0).astype(bf16)


def _inproj_kernel(
    x_ref, xp_ref, xn_ref, sh_ref, sc_ref, n1_ref,
    wqa_ref, wkva_ref, wkr_ref, wqkv_ref, wz_ref, wab_ref, wabt_ref,
    qan_ref, kvan_ref, wqb_ref, wkb_ref, wvb_ref, qn_ref, kn_ref,
    cos_ref, sina_ref, sinb_ref, conv_ref, gpar_ref, gpart_ref,
    q_ref, k_ref, v_ref, dq_ref, dk_ref, dv_ref, dkt_ref, z_ref, gcol_ref, grow_ref,
    ext_ref,
):
    i = pl.program_id(1)
    ni = pl.num_programs(1)
    tm = x_ref.shape[1]
    scale1 = 1.0 + sc_ref[0]
    shift1 = sh_ref[0]
    n1 = n1_ref[...]

    def modulate(xv):
        y = xv * lax.rsqrt(jnp.mean(xv * xv, axis=-1, keepdims=True) + EPS)
        return y * n1 * scale1 + shift1

    h = modulate(x_ref[0])
    hb = h.astype(bf16)
    hh = modulate(jnp.concatenate([xp_ref[0], xn_ref[0]], axis=0)).astype(bf16)

    qa = _dot(hb, wqa_ref[...])
    qa = qa * lax.rsqrt(jnp.mean(qa * qa, axis=-1, keepdims=True) + EPS) * qan_ref[...]
    kva = _dot(hb, wkva_ref[...])
    kva = kva * lax.rsqrt(jnp.mean(kva * kva, axis=-1, keepdims=True) + EPS) * kvan_ref[...]
    kr = _dot(hb, wkr_ref[...])
    qh = _dot(qa.astype(bf16), wqb_ref[...])
    kvb = kva.astype(bf16)
    kh = _dot(kvb, wkb_ref[...])
    vh = _dot(kvb, wvb_ref[...])
    cos = cos_ref[...]
    sina = sina_ref[...]
    sinb = sinb_ref[...]
    qg = qn_ref[...]
    kg = kn_ref[...]
    lane = lax.broadcasted_iota(i32, (tm, LANES), 1)
    q_scale = MLA_QK ** -0.5

    def norm_rope(blk, gain):
        ss = jnp.sum(blk * blk, axis=-1, keepdims=True) * (1.0 / MLA_QK)
        y = blk * lax.rsqrt(ss + EPS) * gain
        return y * cos + pltpu.roll(y, LANES - MLA_ROPE // 2, 1) * sina + pltpu.roll(y, MLA_ROPE // 2, 1) * sinb

    for hd in range(MLA_HEADS):
        sl = slice(hd * LANES, (hd + 1) * LANES)
        q_ref[0, hd] = (norm_rope(qh[:, sl], qg) * q_scale).astype(bf16)
        k_ref[0, hd] = norm_rope(kh[:, sl] + kr, kg).astype(bf16)
        vblk = vh[:, hd * MLA_V:(hd + 1) * MLA_V]
        vpad = jnp.concatenate([vblk, jnp.zeros((tm, LANES - MLA_V), f32)], axis=1)
        v_ref[0, hd] = jnp.where(lane == MLA_V, 1.0, vpad).astype(bf16)

    ext_ref[pl.ds(8, tm), :] = _dot(hb, wqkv_ref[...])
    halo = _dot(hh, wqkv_ref[...])
    ext_ref[pl.ds(0, 8), :] = jnp.where(i == 0, 0.0, halo[:8])
    ext_ref[pl.ds(8 + tm, 8), :] = jnp.where(i == ni - 1, 0.0, halo[8:])
    pad = (DN_CONV - 1) // 2
    cw = conv_ref[...]
    acc = ext_ref[pl.ds(8 - pad, tm), :] * cw[0:1]
    for j in range(1, DN_CONV):
        acc = acc + ext_ref[pl.ds(8 - pad + j, tm), :] * cw[j:j + 1]
    act = _silu(acc)
    nqk = DN_HEADS * DN_DK
    for hd in range(DN_HEADS):
        sl = slice(hd * DN_DK, (hd + 1) * DN_DK)
        qb = act[:, sl]
        qb = qb * lax.rsqrt(jnp.sum(qb * qb, axis=-1, keepdims=True) + EPS) * (DN_DK ** -0.5)
        dq_ref[0, :, sl] = qb
        kb = act[:, nqk + hd * DN_DK: nqk + (hd + 1) * DN_DK]
        kb = kb * lax.rsqrt(jnp.sum(kb * kb, axis=-1, keepdims=True) + EPS)
        dk_ref[0, :, sl] = kb
        dkt_ref[0, sl, :] = kb.T
    dv_ref[0] = act[:, 2 * nqk:]
    z_ref[0] = _dot(hb, wz_ref[...])

    hlo = (h - hb.astype(f32)).astype(bf16)
    wab = wab_ref[...]
    p1 = _dot(hb, wab)
    p2 = _dot(hlo, wab)
    ab = p1 + pltpu.roll(p1, LANES - 16, 1) + p2
    gpar = gpar_ref[...]
    lane16 = lax.broadcasted_iota(i32, (tm, LANES), 1)
    gval = jnp.where(lane16 < 8, -gpar[0:1] * _softplus(ab + gpar[1:2]), _sigmoid(ab))
    g3 = _split3(gval)
    lo_tri = _chunk_tri(tm, True)
    up_tri = _chunk_tri(tm, False)
    pre = _dot(lo_tri, g3[0]) + (_dot(lo_tri, g3[1]) + _dot(lo_tri, g3[2]))
    suf = _dot(up_tri, g3[0]) + (_dot(up_tri, g3[1]) + _dot(up_tri, g3[2]))
    gcol_ref[0] = jnp.where(lane16 < 4, pre, jnp.where(lane16 < 8, suf, gval))

    wabt = wabt_ref[...]
    r1 = _dot_nt(wabt, hb)
    r2 = _dot_nt(wabt[:16], hlo)
    abt = r1[:16] + r1[16:] + r2
    gpt = gpart_ref[...]
    row16 = lax.broadcasted_iota(i32, (16, tm), 0)
    gvt = jnp.where(row16 < 8, -gpt[:, 0:1] * _softplus(abt + gpt[:, 1:2]), _sigmoid(abt))
    t3 = _split3(gvt)
    pre_t = _dot(t3[0], up_tri) + (_dot(t3[1], up_tri) + _dot(t3[2], up_tri))
    suf_t = _dot(t3[0], lo_tri) + (_dot(t3[1], lo_tri) + _dot(t3[2], lo_tri))
    grow_ref[0] = jnp.where(row16 < 4, pre_t, jnp.where(row16 < 8, suf_t, gvt))


def _prep_weights(w_in, w_q_b, w_kv_b, q_norm, k_norm, dn_conv, dn_a_log, dn_dt_bias):
    D = w_in.shape[0]
    i0 = MLA_Q_LORA
    i1 = i0 + MLA_KV_LORA
    i2 = i1 + MLA_ROPE
    nqkv = DN_HEADS * (2 * DN_DK + DN_DV)
    i3 = i2 + nqkv
    i4 = i3 + DN_HEADS * DN_DV
    wqa = w_in[:, :i0].astype(bf16)
    wkva = w_in[:, i0:i1].astype(bf16)
    wkr = jnp.zeros((D, LANES), f32).at[:, MLA_NOPE:MLA_QK].set(w_in[:, i1:i2]).astype(bf16)
    wqkv = w_in[:, i2:i3].astype(bf16)
    wz = w_in[:, i3:i4].astype(bf16)
    wab_f = w_in[:, i4:]
    wab_hi = wab_f.astype(bf16)
    wab_lo = (wab_f - wab_hi.astype(f32)).astype(bf16)
    wab = jnp.zeros((D, LANES), bf16).at[:, :16].set(wab_hi).at[:, 16:32].set(wab_lo)
    wabt = jnp.concatenate([wab_hi.T, wab_lo.T], axis=0)
    wqb = w_q_b.reshape(MLA_Q_LORA, MLA_HEADS, MLA_QK)
    wqb = jnp.pad(wqb, ((0, 0), (0, 0), (0, LANES - MLA_QK))).reshape(MLA_Q_LORA, MLA_HEADS * LANES).astype(bf16)
    wkv = w_kv_b.reshape(MLA_KV_LORA, MLA_HEADS, MLA_NOPE + MLA_V)
    wkb = jnp.pad(wkv[:, :, :MLA_NOPE], ((0, 0), (0, 0), (0, LANES - MLA_NOPE)))
    wkb = wkb.reshape(MLA_KV_LORA, MLA_HEADS * LANES).astype(bf16)
    wvb = wkv[:, :, MLA_NOPE:].reshape(MLA_KV_LORA, MLA_HEADS * MLA_V).astype(bf16)
    qn = jnp.pad(q_norm, (0, LANES - MLA_QK)).reshape(1, LANES)
    kn = jnp.pad(k_norm, (0, LANES - MLA_QK)).reshape(1, LANES)
    conv = jnp.pad(dn_conv, ((0, 8 - DN_CONV), (0, 0)))
    ea = jnp.exp(dn_a_log.astype(f32)).reshape(-1)
    dtb = dn_dt_bias.astype(f32).reshape(-1)
    gpar = jnp.zeros((8, LANES), f32).at[0, :8].set(ea).at[1, :8].set(dtb)
    gpart = jnp.zeros((16, LANES), f32).at[:8, 0].set(ea).at[:8, 1].set(dtb)
    return dict(wqa=wqa, wkva=wkva, wkr=wkr, wqkv=wqkv, wz=wz, wab=wab, wabt=wabt, wqb=wqb, wkb=wkb,
                wvb=wvb, qn=qn, kn=kn, conv=conv, gpar=gpar, gpart=gpart)


def _rope_tables(S):
    half = MLA_ROPE // 2
    freq = ROPE_THETA ** (-jnp.arange(half, dtype=f32) / half)
    ang = jnp.arange(S, dtype=f32)[:, None] * freq[None, :]
    cos, sin = jnp.cos(ang), jnp.sin(ang)
    z = jnp.zeros((S, LANES), f32)
    cos_t = z.at[:, :MLA_NOPE].set(1.0).at[:, MLA_NOPE:MLA_NOPE + half].set(cos).at[:, MLA_NOPE + half:MLA_QK].set(cos)
    sina = z.at[:, MLA_NOPE:MLA_NOPE + half].set(-sin)
    sinb = z.at[:, MLA_NOPE + half:MLA_QK].set(sin)
    return cos_t, sina, sinb


def _inproj(x, sh1, sc1, norm1, q_a_norm, kv_a_norm, pw, rope):
    B, S, D = x.shape
    tm = min(TM_IN, S)
    nt = S // tm
    r8 = tm // 8
    nqkv = DN_HEADS * (2 * DN_DK + DN_DV)
    nh = DN_HEADS * DN_DK

    def full(a):
        return pl.BlockSpec(a.shape, lambda b, i: (0,) * a.ndim)

    tok = lambda w: pl.BlockSpec((1, tm, w), lambda b, i: (b, i, 0))
    in_specs = [
        tok(D),
        pl.BlockSpec((1, 8, D), lambda b, i: (b, jnp.maximum(i * r8 - 1, 0), 0)),
        pl.BlockSpec((1, 8, D), lambda b, i: (b, jnp.minimum((i + 1) * r8, S // 8 - 1), 0)),
        pl.BlockSpec((1, 1, D), lambda b, i: (b, 0, 0)),
        pl.BlockSpec((1, 1, D), lambda b, i: (b, 0, 0)),
    ]
    consts = [norm1.reshape(1, D), pw["wqa"], pw["wkva"], pw["wkr"], pw["wqkv"], pw["wz"], pw["wab"], pw["wabt"],
              q_a_norm.reshape(1, -1), kv_a_norm.reshape(1, -1), pw["wqb"], pw["wkb"], pw["wvb"], pw["qn"], pw["kn"]]
    in_specs += [full(a) for a in consts]
    in_specs += [pl.BlockSpec((tm, LANES), lambda b, i: (i, 0))] * 3
    tail = [pw["conv"], pw["gpar"], pw["gpart"]]
    in_specs += [full(a) for a in tail]
    hspec = pl.BlockSpec((1, MLA_HEADS, tm, LANES), lambda b, i: (b, 0, i, 0))
    out_specs = [hspec, hspec, hspec, tok(nh), tok(nh), tok(nh),
                 pl.BlockSpec((1, nh, tm), lambda b, i: (b, 0, i)), tok(nh), tok(LANES),
                 pl.BlockSpec((1, 16, tm), lambda b, i: (b, 0, i))]
    hshape = jax.ShapeDtypeStruct((B, MLA_HEADS, S, LANES), bf16)
    tshape = jax.ShapeDtypeStruct((B, S, nh), f32)
    out_shape = [hshape, hshape, hshape, tshape, tshape, tshape,
                 jax.ShapeDtypeStruct((B, nh, S), f32), tshape,
                 jax.ShapeDtypeStruct((B, S, LANES), f32), jax.ShapeDtypeStruct((B, 16, S), f32)]
    return pl.pallas_call(
        _inproj_kernel,
        grid=(B, nt),
        in_specs=in_specs,
        out_specs=out_specs,
        out_shape=out_shape,
        scratch_shapes=[pltpu.VMEM((tm + 16, nqkv), f32)],
        compiler_params=_cparams(("parallel", "parallel")),
        name="inproj",
    )(x, x, x, sh1, sc1, *consts, *rope, *tail)


def _attn_kernel(q_ref, k_ref, v_ref, o_ref):
    S = k_ref.shape[2]
    tq = q_ref.shape[2]
    nk = S // TK if S >= TK else 1
    tk = S // nk
    outs = []
    for hd in range(2):
        q = q_ref[0, hd]

        def body(j, carry):
            m, acc = carry
            off = pl.multiple_of(j * tk, tk)
            kc = k_ref[0, hd, pl.ds(off, tk), :]
            vc = v_ref[0, hd, pl.ds(off, tk), :]
            s = _dot_nt(q, kc)
            m_new = jnp.maximum(m, jnp.max(s, axis=-1, keepdims=True))
            p = jnp.exp(s - m_new)
            acc = acc * jnp.exp(m - m_new) + _dot(p.astype(bf16), vc)
            return m_new, acc

        m0 = jnp.full((tq, 1), -jnp.inf, f32)
        a0 = jnp.zeros((tq, LANES), f32)
        _, acc = lax.fori_loop(0, nk, body, (m0, a0))
        outs.append(acc[:, :MLA_V] / acc[:, MLA_V:MLA_V + 1])
    o_ref[0] = jnp.concatenate(outs, axis=1).astype(o_ref.dtype)


def _attention(q, k, v):
    B, H, S, _ = q.shape
    tq = min(TQ, S)
    qspec = pl.BlockSpec((1, 2, tq, LANES), lambda b, h, i: (b, h, i, 0))
    kspec = pl.BlockSpec((1, 2, S, LANES), lambda b, h, i: (b, h, 0, 0))
    return pl.pallas_call(
        _attn_kernel,
        grid=(B, H // 2, S // tq),
        in_specs=[qspec, kspec, kspec],
        out_specs=pl.BlockSpec((1, tq, LANES), lambda b, h, i: (b, i, h)),
        out_shape=jax.ShapeDtypeStruct((B, S, H * MLA_V), bf16),
        compiler_params=_cparams(("parallel", "parallel", "arbitrary")),
        name="attn",
    )(q, k, v)


def _unit_tri_inverse(L, sub_mask):
    C = L.shape[0]
    r = lax.broadcasted_iota(i32, (C, C), 0)
    c = lax.broadcasted_iota(i32, (C, C), 1)
    eye = jnp.where(r == c, 1.0, 0.0)
    Ld = jnp.where(sub_mask, L, 0.0)
    Lo = L - Ld
    T = eye - Ld
    P = Ld
    n = 2
    while n < DN_SUB:
        P = _dot3(P, P)
        T = _dot3(T, eye + P)
        n *= 2
    N = _dot3(T, Lo)
    N2 = _dot3(N, N)
    return _dot3(_dot3(eye - N, eye + N2), T)


def _gdn_kernel(q_ref, k_ref, v_ref, kt_ref, gcol_ref, grow_ref, o_ref, s_ref, *, reverse):
    @pl.when(pl.program_id(1) == 0)
    def _():
        s_ref[...] = jnp.zeros_like(s_ref)

    C = DN_CHUNK
    rows = q_ref.shape[1]
    nch = rows // C
    r = lax.broadcasted_iota(i32, (C, C), 0)
    c = lax.broadcasted_iota(i32, (C, C), 1)
    incl = (c >= r) if reverse else (c <= r)
    strict = (c > r) if reverse else (c < r)
    sub_mask = (r // DN_SUB) == (c // DN_SUB)
    goff = 4 if reverse else 0
    order = range(nch - 1, -1, -1) if reverse else range(nch)
    for ch in order:
        rs = slice(ch * C, (ch + 1) * C)
        for hd in range(DN_HEADS):
            hs = slice(hd * DN_DK, (hd + 1) * DN_DK)
            q = q_ref[0, rs, hs]
            k = k_ref[0, rs, hs]
            v = v_ref[0, rs, hs]
            kt = kt_ref[0, hs, rs]
            gc = gcol_ref[0, rs, goff + hd: goff + hd + 1]
            beta = gcol_ref[0, rs, 8 + goff + hd: 8 + goff + hd + 1]
            gr = grow_ref[0, goff + hd: goff + hd + 1, rs]
            dec = jnp.exp(jnp.where(incl, gc - gr, -jnp.inf))
            kb = k * beta
            kbb = kb.astype(bf16)
            kbf = k.astype(bf16)
            L = jnp.where(strict, _dot_nt(kbb, kbf) * dec, 0.0)
            A = _dot_nt(q.astype(bf16), kbf) * dec
            T = _unit_tri_inverse(L, sub_mask)
            eg = jnp.exp(gc)
            rhs = jnp.concatenate([v * beta, kb * eg], axis=1)
            sol = _dot3(T, rhs)
            u = sol[:, :DN_DV]
            w = sol[:, DN_DV:]
            st = s_ref[hd]
            stb = st.astype(bf16)
            wq = jnp.concatenate([w, q * eg], axis=0).astype(bf16)
            ws = _dot(wq, stb)
            v_new = u - ws[:C]
            vnb = v_new.astype(bf16)
            o_ref[0, rs, hs] = ws[C:] + _dot(A.astype(bf16), vnb)
            g_last = gr[:, 0:1] if reverse else gr[:, C - 1:C]
            kdec_t = (kt * jnp.exp(g_last - gr)).astype(bf16)
            s_ref[hd] = st * jnp.exp(g_last) + _dot(kdec_t, vnb)


def _gdn(dq, dk, dv, dkt, gcol, grow, reverse):
    B, S, W = dq.shape
    rows = min(GDN_ROWS, S)
    n = S // rows
    if reverse:
        idx = lambda b, i: (b, n - 1 - i, 0)
        idx_t = lambda b, i: (b, 0, n - 1 - i)
    else:
        idx = lambda b, i: (b, i, 0)
        idx_t = lambda b, i: (b, 0, i)
    tok = pl.BlockSpec((1, rows, W), idx)
    return pl.pallas_call(
        functools.partial(_gdn_kernel, reverse=reverse),
        grid=(B, n),
        in_specs=[tok, tok, tok, pl.BlockSpec((1, W, rows), idx_t),
                  pl.BlockSpec((1, rows, LANES), idx), pl.BlockSpec((1, 16, rows), idx_t)],
        out_specs=tok,
        out_shape=jax.ShapeDtypeStruct((B, S, W), f32),
        scratch_shapes=[pltpu.VMEM((DN_HEADS, DN_DK, DN_DV), f32)],
        compiler_params=_cparams(("parallel", "arbitrary")),
        name="gdn_bwd" if reverse else "gdn_fwd",
    )(dq, dk, dv, dkt, gcol, grow)


def _pack_bf16_pairs(y):
    w = y.shape[1] // 2
    lo = pltpu.bitcast(y[:, :w].astype(bf16).astype(f32), u32)
    hi = pltpu.bitcast(y[:, w:].astype(bf16).astype(f32), u32)
    return (lo >> 16) | (hi & jnp.uint32(0xFFFF0000))


def _unpack_bf16_pairs(p):
    lo = pltpu.bitcast(p << 16, f32)
    hi = pltpu.bitcast(p & jnp.uint32(0xFFFF0000), f32)
    return jnp.concatenate([lo, hi], axis=1)


def _outproj_kernel(x_ref, om_ref, of_ref, ob_ref, z_ref, gt_ref, sc_ref, sh_ref,
                    wo_ref, dnn_ref, n2_ref, wr_ref, br_ref,
                    x1_ref, hp_ref, meta_ref, cnt_ref, carry_ref):
    first = (pl.program_id(0) == 0) & (pl.program_id(1) == 0)

    @pl.when(first)
    def _():
        carry_ref[...] = jnp.zeros_like(carry_ref)

    tm = x_ref.shape[1]
    o = of_ref[0] + ob_ref[0]
    z = z_ref[0]
    dnn = dnn_ref[...]
    parts = []
    for hd in range(DN_HEADS):
        sl = slice(hd * DN_DV, (hd + 1) * DN_DV)
        ob = o[:, sl]
        ob = ob * lax.rsqrt(jnp.mean(ob * ob, axis=-1, keepdims=True) + EPS) * dnn
        parts.append(ob * _silu(z[:, sl]))
    odn = jnp.concatenate(parts, axis=1).astype(bf16)
    nm = om_ref.shape[2]
    mixed = _dot(om_ref[0], wo_ref[pl.ds(0, nm), :]) + _dot(odn, wo_ref[pl.ds(nm, odn.shape[1]), :])
    x1 = x_ref[0] + gt_ref[0] * mixed
    x1_ref[0] = x1
    h2 = x1 * lax.rsqrt(jnp.mean(x1 * x1, axis=-1, keepdims=True) + EPS) * n2_ref[...]
    h2 = h2 * (1.0 + sc_ref[0]) + sh_ref[0]
    hp_ref[...] = _pack_bf16_pairs(h2)

    hh, hl = _split2(h2)
    wr = wr_ref[...]
    p1 = _dot(hh, wr)
    p2 = _dot(hl, wr[:, :LANES])
    lane = lax.broadcasted_iota(i32, (tm, LANES), 1)
    logits = p1[:, :LANES] + p1[:, LANES:] + p2 + br_ref[...]
    logits = jnp.where(lane < N_EXPERTS, logits, -jnp.inf)
    vals, idxs = [], []
    work = logits
    for _ in range(TOP_K):
        mx = jnp.max(work, axis=-1, keepdims=True)
        ix = jnp.min(jnp.where(work == mx, lane, LANES), axis=-1, keepdims=True)
        vals.append(mx)
        idxs.append(ix)
        work = jnp.where(lane == ix, -jnp.inf, work)
    es = [jnp.exp(vv - vals[0]) for vv in vals]
    den = es[0] + es[1] + es[2] + es[3]
    multihot = jnp.where(work != logits, 1.0, 0.0)
    rr = lax.broadcasted_iota(i32, (tm, tm), 0)
    cc = lax.broadcasted_iota(i32, (tm, tm), 1)
    below = jnp.where(cc < rr, 1.0, 0.0).astype(bf16)
    prefix = _dot(below, multihot.astype(bf16)) + carry_ref[0:1]
    meta = jnp.zeros((tm, LANES), f32)
    for kk in range(TOP_K):
        rank = jnp.sum(jnp.where(lane == idxs[kk], prefix, 0.0), axis=-1, keepdims=True)
        meta = jnp.where(lane == kk, idxs[kk].astype(f32), meta)
        meta = jnp.where(lane == TOP_K + kk, rank, meta)
        meta = jnp.where(lane == 2 * TOP_K + kk, es[kk] / den, meta)
    meta_ref[...] = meta
    carry = carry_ref[...] + jnp.sum(multihot, axis=0, keepdims=True)
    carry_ref[...] = carry
    cnt_ref[...] = carry


def _outproj(x, o_mla, o_f, o_b, z, gt1, sc2, sh2, w_o, dn_out_norm, norm2, w_router, b_router):
    B, S, D = x.shape
    tm = min(TM_OUT, S)
    nt = S // tm
    T = B * S
    nh = o_f.shape[2]
    wr_hi = w_router.astype(bf16)
    wr_lo = (w_router - wr_hi.astype(f32)).astype(bf16)
    wr = jnp.zeros((D, 2 * LANES), bf16).at[:, :N_EXPERTS].set(wr_hi).at[:, LANES:LANES + N_EXPERTS].set(wr_lo)
    br = jnp.zeros((1, LANES), f32).at[0, :N_EXPERTS].set(b_router)
    consts = [w_o.astype(bf16), dn_out_norm.reshape(1, -1), norm2.reshape(1, D), wr, br]

    def full(a):
        return pl.BlockSpec(a.shape, lambda b, i: (0,) * a.ndim)

    tok = lambda w: pl.BlockSpec((1, tm, w), lambda b, i: (b, i, 0))
    vec = pl.BlockSpec((1, 1, D), lambda b, i: (b, 0, 0))
    flat = lambda w: pl.BlockSpec((tm, w), lambda b, i: (b * nt + i, 0))
    return pl.pallas_call(
        _outproj_kernel,
        grid=(B, nt),
        in_specs=[tok(D), tok(o_mla.shape[2]), tok(nh), tok(nh), tok(nh), vec, vec, vec] + [full(a) for a in consts],
        out_specs=[tok(D), flat(D // 2), flat(LANES), pl.BlockSpec((8, LANES), lambda b, i: (0, 0))],
        out_shape=[jax.ShapeDtypeStruct((B, S, D), f32), jax.ShapeDtypeStruct((T, D // 2), u32),
                   jax.ShapeDtypeStruct((T, LANES), f32), jax.ShapeDtypeStruct((8, LANES), f32)],
        scratch_shapes=[pltpu.VMEM((8, LANES), f32)],
        compiler_params=_cparams(("arbitrary", "arbitrary")),
        name="outproj",
    )(x, o_mla, o_f, o_b, z, gt1, sc2, sh2, *consts)


def _dispatch_kernel(dest_ref, hp_ref, xb_in_ref, xb_ref, sem):
    del xb_in_ref
    n = dest_ref.shape[0]
    tt = n // TOP_K
    base = pl.program_id(0) * tt

    def issue(j, carry):
        t = base + j // TOP_K
        pltpu.make_async_copy(hp_ref.at[pl.ds(t, 1)], xb_ref.at[pl.ds(dest_ref[j], 1)], sem).start()
        return carry

    lax.fori_loop(0, n, issue, 0)

    def drain(j, carry):
        pltpu.make_async_copy(hp_ref.at[pl.ds(0, 1)], xb_ref.at[pl.ds(0, 1)], sem).wait()
        return carry

    lax.fori_loop(0, n, drain, 0)


def _dispatch(hp, dest, P):
    T, W = hp.shape
    tt = min(TT_DISPATCH, T)
    xb0 = jnp.zeros((P, W), u32)
    return pl.pallas_call(
        _dispatch_kernel,
        grid=(T // tt,),
        in_specs=[pl.BlockSpec((tt * TOP_K,), lambda i: (i,), memory_space=pltpu.SMEM),
                  pl.BlockSpec(memory_space=pl.ANY), pl.BlockSpec(memory_space=pl.ANY)],
        out_specs=pl.BlockSpec(memory_space=pl.ANY),
        out_shape=jax.ShapeDtypeStruct((P, W), u32),
        scratch_shapes=[pltpu.SemaphoreType.DMA(())],
        input_output_aliases={2: 0},
        compiler_params=pltpu.CompilerParams(dimension_semantics=("arbitrary",), has_side_effects=True),
        name="dispatch",
    )(dest, hp, xb0)


def _expert_kernel(be_ref, nb_ref, x_ref, wg_ref, bg_ref, wu_ref, bu_ref, wd_ref, bd_ref, y_ref):
    b = pl.program_id(0)

    @pl.when(b < nb_ref[0])
    def _():
        x = _unpack_bf16_pairs(x_ref[...]).astype(bf16)
        gt = jnp.minimum(_dot(x, wg_ref[0]) + bg_ref[0], SWIGLU_LIMIT)
        up = jnp.clip(_dot(x, wu_ref[0]) + bu_ref[0], -SWIGLU_LIMIT, SWIGLU_LIMIT)
        act = (up + 1.0) * gt * _sigmoid(SWIGLU_ALPHA * gt)
        y = _dot(act.astype(bf16), wd_ref[0]) + bd_ref[0]
        y_ref[...] = _pack_bf16_pairs(y)

    @pl.when(b >= nb_ref[0])
    def _():
        y_ref[...] = jnp.zeros_like(y_ref)


def _experts(xb, block_expert, n_used, wg, bg, wu, bu, wd, bd):
    P, W = xb.shape
    E, D, F = wg.shape
    nb = P // MOE_BLOCK
    wspec = lambda r, c: pl.BlockSpec((1, r, c), lambda b, be, nu: (be[b], 0, 0))
    grid_spec = pltpu.PrefetchScalarGridSpec(
        num_scalar_prefetch=2,
        grid=(nb,),
        in_specs=[pl.BlockSpec((MOE_BLOCK, W), lambda b, be, nu: (b, 0)),
                  wspec(D, F), wspec(1, F), wspec(D, F), wspec(1, F), wspec(F, D), wspec(1, D)],
        out_specs=pl.BlockSpec((MOE_BLOCK, W), lambda b, be, nu: (b, 0)),
    )
    return pl.pallas_call(
        _expert_kernel,
        grid_spec=grid_spec,
        out_shape=jax.ShapeDtypeStruct((P, W), u32),
        compiler_params=_cparams(("arbitrary",)),
        name="experts",
    )(block_expert, n_used, xb, wg, bg.reshape(E, 1, F), wu, bu.reshape(E, 1, F), wd, bd.reshape(E, 1, D))


def _combine_kernel(dest_ref, yb_ref, meta_ref, x1_ref, gt_ref, o_ref, buf_ref, sem):
    n = dest_ref.shape[0]
    tt = n // TOP_K

    def issue(j, carry):
        r = j // TOP_K
        kk = j % TOP_K
        pltpu.make_async_copy(yb_ref.at[pl.ds(dest_ref[j], 1)], buf_ref.at[kk, pl.ds(r, 1)], sem).start()
        return carry

    lax.fori_loop(0, n, issue, 0)

    def drain(j, carry):
        pltpu.make_async_copy(yb_ref.at[pl.ds(0, 1)], buf_ref.at[0, pl.ds(0, 1)], sem).wait()
        return carry

    lax.fori_loop(0, n, drain, 0)
    meta = meta_ref[...]
    moe = jnp.zeros((tt, x1_ref.shape[2]), f32)
    for kk in range(TOP_K):
        gate = meta[:, 2 * TOP_K + kk: 2 * TOP_K + kk + 1]
        moe = moe + gate * _unpack_bf16_pairs(buf_ref[kk])
    o_ref[0] = x1_ref[0] + gt_ref[0] * moe


def _combine(yb, dest, meta, x1, gt2):
    B, S, D = x1.shape
    tt = min(TT_COMBINE, S)
    nt = S // tt
    W = yb.shape[1]
    return pl.pallas_call(
        _combine_kernel,
        grid=(B, nt),
        in_specs=[pl.BlockSpec((tt * TOP_K,), lambda b, i: (b * nt + i,), memory_space=pltpu.SMEM),
                  pl.BlockSpec(memory_space=pl.ANY),
                  pl.BlockSpec((tt, LANES), lambda b, i: (b * nt + i, 0)),
                  pl.BlockSpec((1, tt, D), lambda b, i: (b, i, 0)),
                  pl.BlockSpec((1, 1, D), lambda b, i: (b, 0, 0))],
        out_specs=pl.BlockSpec((1, tt, D), lambda b, i: (b, i, 0)),
        out_shape=jax.ShapeDtypeStruct((B, S, D), f32),
        scratch_shapes=[pltpu.VMEM((TOP_K, tt, W), u32), pltpu.SemaphoreType.DMA(())],
        compiler_params=_cparams(("arbitrary", "arbitrary")),
        name="combine",
    )(dest, yb, meta, x1, gt2)


def _moe(hp, meta, cnt, x1, gt2, ew):
    T = hp.shape[0]
    TK_ = T * TOP_K
    nb = -(-TK_ // MOE_BLOCK) + N_EXPERTS
    P = nb * MOE_BLOCK
    counts = cnt[0, :N_EXPERTS].astype(i32)
    padded = (counts + MOE_BLOCK - 1) // MOE_BLOCK * MOE_BLOCK
    cum_padded = jnp.cumsum(padded)
    pstart = cum_padded - padded
    top_idx = meta[:, :TOP_K].astype(i32)
    rank = meta[:, TOP_K:2 * TOP_K].astype(i32)
    dest = (pstart[top_idx] + rank).reshape(-1)
    block_expert = jnp.minimum(
        jnp.searchsorted(cum_padded, jnp.arange(nb, dtype=i32) * MOE_BLOCK, side="right"), N_EXPERTS - 1).astype(i32)
    n_used = (cum_padded[-1:] // MOE_BLOCK).astype(i32)
    xb = _dispatch(hp, dest, P)
    yb = _experts(xb, block_expert, n_used, *ew)
    return _combine(yb, dest, meta, x1, gt2)


def _layer(x, c, p, pw, ew, rope):
    B, S, D = x.shape
    mod = _ada(c, p["w_ada"], p["b_ada"]).reshape(B, 6, 1, D)
    sh1, sc1, gt1, sh2, sc2, gt2 = (mod[:, j] for j in range(6))
    q, k, v, dq, dk, dv, dkt, z, gcol, grow = _inproj(
        x, sh1, sc1, p["norm1"], p["q_a_norm"], p["kv_a_norm"], pw, rope)
    o_mla = _attention(q, k, v)
    o_f = _gdn(dq, dk, dv, dkt, gcol, grow, False)
    o_b = _gdn(dq, dk, dv, dkt, gcol, grow, True)
    dnn = jnp.tile(p["dn_out_norm"], 1)
    x1, hp, meta, cnt = _outproj(x, o_mla, o_f, o_b, z, gt1, sc2, sh2, p["w_o"], dnn, p["norm2"],
                                 p["w_router"], p["b_router"])
    return _moe(hp, meta, cnt, x1, gt2, ew)


def kernel(x_prompt, x_sample, c_prompt, c_sample, w_ada, b_ada, norm1, w_in, q_a_norm, w_q_b, kv_a_norm, w_kv_b, q_norm, k_norm, dn_conv, dn_a_log, dn_dt_bias, dn_out_norm, w_o, norm2, w_router, b_router, w_gate, b_gate, w_up, b_up, w_down, b_down):
    y_prompt, y_sample = x_prompt, x_sample
    depth = w_ada.shape[0]
    for l in range(depth):
        p = {"w_ada": w_ada[l], "b_ada": b_ada[l], "norm1": norm1[l], "q_a_norm": q_a_norm[l],
             "kv_a_norm": kv_a_norm[l], "dn_out_norm": dn_out_norm[l], "w_o": w_o[l], "norm2": norm2[l],
             "w_router": w_router[l], "b_router": b_router[l]}
        pw = _prep_weights(w_in[l], w_q_b[l], w_kv_b[l], q_norm[l], k_norm[l], dn_conv[l], dn_a_log[l],
                           dn_dt_bias[l])
        ew = (w_gate[l].astype(bf16), b_gate[l], w_up[l].astype(bf16), b_up[l], w_down[l].astype(bf16), b_down[l])
        y_prompt = _layer(y_prompt, c_prompt, p, pw, ew, _rope_tables(y_prompt.shape[1]))
        y_sample = _layer(y_sample, c_sample, p, pw, ew, _rope_tables(y_sample.shape[1]))
    return (y_prompt, y_sample)
```

```python
import functools
import math

import jax
import jax.numpy as jnp
from jax import lax
from jax.experimental import pallas as pl
from jax.experimental.pallas import tpu as pltpu

f32 = jnp.float32
bf16 = jnp.bfloat16
u32 = jnp.uint32
i32 = jnp.int32

LANES = 128
VMEM_LIMIT = 56 * 1024 * 1024

MLA_HEADS = 8
MLA_Q_LORA = 384
MLA_KV_LORA = 256
MLA_NOPE = 64
MLA_ROPE = 32
MLA_QK = MLA_NOPE + MLA_ROPE
MLA_V = 64
ROPE_THETA = 10000.0
DN_HEADS = 4
DN_DK = 128
DN_DV = 128
DN_CONV = 5
DN_CHUNK = 64
DN_SUB = 16
N_EXPERTS = 32
TOP_K = 4
SWIGLU_LIMIT = 7.0
SWIGLU_ALPHA = 1.702
MOE_BLOCK = 256
EPS = 1e-6

TM_IN = 256
TM_OUT = 256
TQ = 256
TK = 512
GDN_ROWS = 128
TT_DISPATCH = 512
TT_COMBINE = 256


def _cparams(sem):
    return pltpu.CompilerParams(dimension_semantics=sem, vmem_limit_bytes=VMEM_LIMIT)


def _split2(x):
    hi = x.astype(bf16)
    lo = (x - hi.astype(f32)).astype(bf16)
    return hi, lo


def _split3(x):
    hi = x.astype(bf16)
    r = x - hi.astype(f32)
    mid = r.astype(bf16)
    lo = (r - mid.astype(f32)).astype(bf16)
    return hi, mid, lo


def _dot(a, b):
    return jnp.dot(a, b, preferred_element_type=f32)


def _dot_nt(a, b):
    return lax.dot_general(a, b, (((1,), (1,)), ((), ())), preferred_element_type=f32)


def _dot3(a, b):
    ah, al = _split2(a)
    bh, bl = _split2(b)
    return _dot(ah, bh) + (_dot(al, bh) + _dot(ah, bl))


def _sigmoid(x):
    return 1.0 / (1.0 + jnp.exp(-x))


def _silu(x):
    return x * _sigmoid(x)


def _softplus(x):
    return jnp.maximum(x, 0.0) + jnp.log(1.0 + jnp.exp(-jnp.abs(x)))


def _ada_kernel(c_ref, w_ref, b_ref, o_ref):
    c = c_ref[...]
    o_ref[...] = _dot3(_silu(c), w_ref[...]) + b_ref[...]


def _ada(c, w_ada, b_ada):
    B, D = c.shape
    N = w_ada.shape[1]
    cp = jnp.zeros((8, D), f32).at[:B].set(c)
    tn = 1024
    out = pl.pallas_call(
        _ada_kernel,
        grid=(N // tn,),
        in_specs=[
            pl.BlockSpec((8, D), lambda j: (0, 0)),
            pl.BlockSpec((D, tn), lambda j: (0, j)),
            pl.BlockSpec((1, tn), lambda j: (0, j)),
        ],
        out_specs=pl.BlockSpec((8, tn), lambda j: (0, j)),
        out_shape=jax.ShapeDtypeStruct((8, N), f32),
        compiler_params=_cparams(("parallel",)),
        name="ada",
    )(cp, w_ada, b_ada.reshape(1, N))
    return out[:B]


def _chunk_tri(n, lower):
    r = lax.broadcasted_iota(i32, (n, n), 0)
    c = lax.broadcasted_iota(i32, (n, n), 1)
    same = (r // DN_CHUNK) == (c // DN_CHUNK)
    tri = (c <= r) if lower else (c >= r)
    return jnp.where(same & tri, 1.0, 0.0).astype(bf16)


def _inproj_kernel(
    x_ref, xp_ref, xn_ref, sh_ref, sc_ref, n1_ref,
    wqa_ref, wkva_ref, wkr_ref, wqkv_ref, wz_ref, wab_ref, wabt_ref,
    qan_ref, kvan_ref, wqb_ref, wkb_ref, wvb_ref, qn_ref, kn_ref,
    cos_ref, sina_ref, sinb_ref, conv_ref, gpar_ref, gpart_ref,
    q_ref, k_ref, v_ref, dq_ref, dk_ref, dv_ref, dkt_ref, z_ref, gcol_ref, grow_ref,
    ext_ref,
):
    i = pl.program_id(1)
    ni = pl.num_programs(1)
    tm = x_ref.shape[1]
    scale1 = 1.0 + sc_ref[0]
    shift1 = sh_ref[0]
    n1 = n1_ref[...]

    def modulate(xv):
        y = xv * lax.rsqrt(jnp.mean(xv * xv, axis=-1, keepdims=True) + EPS)
        return y * n1 * scale1 + shift1

    h = modulate(x_ref[0])
    hb = h.astype(bf16)
    hh = modulate(jnp.concatenate([xp_ref[0], xn_ref[0]], axis=0)).astype(bf16)

    qa = _dot(hb, wqa_ref[...])
    qa = qa * lax.rsqrt(jnp.mean(qa * qa, axis=-1, keepdims=True) + EPS) * qan_ref[...]
    kva = _dot(hb, wkva_ref[...])
    kva = kva * lax.rsqrt(jnp.mean(kva * kva, axis=-1, keepdims=True) + EPS) * kvan_ref[...]
    kr = _dot(hb, wkr_ref[...])
    qh = _dot(qa.astype(bf16), wqb_ref[...])
    kvb = kva.astype(bf16)
    kh = _dot(kvb, wkb_ref[...])
    vh = _dot(kvb, wvb_ref[...])
    cos = cos_ref[...]
    sina = sina_ref[...]
    sinb = sinb_ref[...]
    qg = qn_ref[...]
    kg = kn_ref[...]
    lane = lax.broadcasted_iota(i32, (tm, LANES), 1)
    q_scale = MLA_QK ** -0.5 * math.log2(math.e)

    def norm_rope(blk, gain):
        ss = jnp.sum(blk * blk, axis=-1, keepdims=True) * (1.0 / MLA_QK)
        y = blk * lax.rsqrt(ss + EPS) * gain
        return y * cos + pltpu.roll(y, LANES - MLA_ROPE // 2, 1) * sina + pltpu.roll(y, MLA_ROPE // 2, 1) * sinb

    for hd in range(MLA_HEADS):
        sl = slice(hd * LANES, (hd + 1) * LANES)
        q_ref[0, hd] = (norm_rope(qh[:, sl], qg) * q_scale).astype(bf16)
        k_ref[0, hd] = norm_rope(kh[:, sl] + kr, kg).astype(bf16)
        vblk = vh[:, hd * MLA_V:(hd + 1) * MLA_V]
        vpad = jnp.concatenate([vblk, jnp.zeros((tm, LANES - MLA_V), f32)], axis=1)
        v_ref[0, hd] = jnp.where(lane == MLA_V, 1.0, vpad).astype(bf16)

    ext_ref[pl.ds(8, tm), :] = _dot(hb, wqkv_ref[...])
    halo = _dot(hh, wqkv_ref[...])
    ext_ref[pl.ds(0, 8), :] = jnp.where(i == 0, 0.0, halo[:8])
    ext_ref[pl.ds(8 + tm, 8), :] = jnp.where(i == ni - 1, 0.0, halo[8:])
    pad = (DN_CONV - 1) // 2
    cw = conv_ref[...]
    acc = ext_ref[pl.ds(8 - pad, tm), :] * cw[0:1]
    for j in range(1, DN_CONV):
        acc = acc + ext_ref[pl.ds(8 - pad + j, tm), :] * cw[j:j + 1]
    act = _silu(acc)
    nqk = DN_HEADS * DN_DK
    for hd in range(DN_HEADS):
        sl = slice(hd * DN_DK, (hd + 1) * DN_DK)
        qb = act[:, sl]
        qb = qb * lax.rsqrt(jnp.sum(qb * qb, axis=-1, keepdims=True) + EPS) * (DN_DK ** -0.5)
        dq_ref[0, :, sl] = qb
        kb = act[:, nqk + hd * DN_DK: nqk + (hd + 1) * DN_DK]
        kb = kb * lax.rsqrt(jnp.sum(kb * kb, axis=-1, keepdims=True) + EPS)
        dk_ref[0, :, sl] = kb
        dkt_ref[0, sl, :] = kb.T
    dv_ref[0] = act[:, 2 * nqk:]
    z_ref[0] = _dot(hb, wz_ref[...])

    hlo = (h - hb.astype(f32)).astype(bf16)
    wab = wab_ref[...]
    p1 = _dot(hb, wab)
    p2 = _dot(hlo, wab)
    ab = p1 + pltpu.roll(p1, LANES - 16, 1) + p2
    gpar = gpar_ref[...]
    lane16 = lax.broadcasted_iota(i32, (tm, LANES), 1)
    gval = jnp.where(lane16 < 8, -gpar[0:1] * _softplus(ab + gpar[1:2]), _sigmoid(ab))
    g3 = _split3(gval)
    lo_tri = _chunk_tri(tm, True)
    up_tri = _chunk_tri(tm, False)
    pre = _dot(lo_tri, g3[0]) + (_dot(lo_tri, g3[1]) + _dot(lo_tri, g3[2]))
    suf = _dot(up_tri, g3[0]) + (_dot(up_tri, g3[1]) + _dot(up_tri, g3[2]))
    gcol_ref[0] = jnp.where(lane16 < 4, pre, jnp.where(lane16 < 8, suf, gval))

    wabt = wabt_ref[...]
    r1 = _dot_nt(wabt, hb)
    r2 = _dot_nt(wabt[:16], hlo)
    abt = r1[:16] + r1[16:] + r2
    gpt = gpart_ref[...]
    row16 = lax.broadcasted_iota(i32, (16, tm), 0)
    gvt = jnp.where(row16 < 8, -gpt[:, 0:1] * _softplus(abt + gpt[:, 1:2]), _sigmoid(abt))
    t3 = _split3(gvt)
    pre_t = _dot(t3[0], up_tri) + (_dot(t3[1], up_tri) + _dot(t3[2], up_tri))
    suf_t = _dot(t3[0], lo_tri) + (_dot(t3[1], lo_tri) + _dot(t3[2], lo_tri))
    grow_ref[0] = jnp.where(row16 < 4, pre_t, jnp.where(row16 < 8, suf_t, gvt))


def _prep_weights(w_in, w_q_b, w_kv_b, q_norm, k_norm, dn_conv, dn_a_log, dn_dt_bias):
    D = w_in.shape[0]
    i0 = MLA_Q_LORA
    i1 = i0 + MLA_KV_LORA
    i2 = i1 + MLA_ROPE
    nqkv = DN_HEADS * (2 * DN_DK + DN_DV)
    i3 = i2 + nqkv
    i4 = i3 + DN_HEADS * DN_DV
    wqa = w_in[:, :i0].astype(bf16)
    wkva = w_in[:, i0:i1].astype(bf16)
    wkr = jnp.zeros((D, LANES), f32).at[:, MLA_NOPE:MLA_QK].set(w_in[:, i1:i2]).astype(bf16)
    wqkv = w_in[:, i2:i3].astype(bf16)
    wz = w_in[:, i3:i4].astype(bf16)
    wab_f = w_in[:, i4:]
    wab_hi = wab_f.astype(bf16)
    wab_lo = (wab_f - wab_hi.astype(f32)).astype(bf16)
    wab = jnp.zeros((D, LANES), bf16).at[:, :16].set(wab_hi).at[:, 16:32].set(wab_lo)
    wabt = jnp.concatenate([wab_hi.T, wab_lo.T], axis=0)
    wqb = w_q_b.reshape(MLA_Q_LORA, MLA_HEADS, MLA_QK)
    wqb = jnp.pad(wqb, ((0, 0), (0, 0), (0, LANES - MLA_QK))).reshape(MLA_Q_LORA, MLA_HEADS * LANES).astype(bf16)
    wkv = w_kv_b.reshape(MLA_KV_LORA, MLA_HEADS, MLA_NOPE + MLA_V)
    wkb = jnp.pad(wkv[:, :, :MLA_NOPE], ((0, 0), (0, 0), (0, LANES - MLA_NOPE)))
    wkb = wkb.reshape(MLA_KV_LORA, MLA_HEADS * LANES).astype(bf16)
    wvb = wkv[:, :, MLA_NOPE:].reshape(MLA_KV_LORA, MLA_HEADS * MLA_V).astype(bf16)
    qn = jnp.pad(q_norm, (0, LANES - MLA_QK)).reshape(1, LANES)
    kn = jnp.pad(k_norm, (0, LANES - MLA_QK)).reshape(1, LANES)
    conv = jnp.pad(dn_conv, ((0, 8 - DN_CONV), (0, 0)))
    ea = jnp.exp(dn_a_log.astype(f32)).reshape(-1)
    dtb = dn_dt_bias.astype(f32).reshape(-1)
    gpar = jnp.zeros((8, LANES), f32).at[0, :8].set(ea).at[1, :8].set(dtb)
    gpart = jnp.zeros((16, LANES), f32).at[:8, 0].set(ea).at[:8, 1].set(dtb)
    return dict(wqa=wqa, wkva=wkva, wkr=wkr, wqkv=wqkv, wz=wz, wab=wab, wabt=wabt, wqb=wqb, wkb=wkb,
                wvb=wvb, qn=qn, kn=kn, conv=conv, gpar=gpar, gpart=gpart)


def _rope_tables(S):
    half = MLA_ROPE // 2
    freq = ROPE_THETA ** (-jnp.arange(half, dtype=f32) / half)
    ang = jnp.arange(S, dtype=f32)[:, None] * freq[None, :]
    cos, sin = jnp.cos(ang), jnp.sin(ang)
    z = jnp.zeros((S, LANES), f32)
    cos_t = z.at[:, :MLA_NOPE].set(1.0).at[:, MLA_NOPE:MLA_NOPE + half].set(cos).at[:, MLA_NOPE + half:MLA_QK].set(cos)
    sina = z.at[:, MLA_NOPE:MLA_NOPE + half].set(-sin)
    sinb = z.at[:, MLA_NOPE + half:MLA_QK].set(sin)
    return cos_t, sina, sinb


def _inproj(x, sh1, sc1, norm1, q_a_norm, kv_a_norm, pw, rope):
    B, S, D = x.shape
    tm = min(TM_IN, S)
    nt = S // tm
    r8 = tm // 8
    nqkv = DN_HEADS * (2 * DN_DK + DN_DV)
    nh = DN_HEADS * DN_DK

    def full(a):
        return pl.BlockSpec(a.shape, lambda b, i: (0,) * a.ndim)

    tok = lambda w: pl.BlockSpec((1, tm, w), lambda b, i: (b, i, 0))
    in_specs = [
        tok(D),
        pl.BlockSpec((1, 8, D), lambda b, i: (b, jnp.maximum(i * r8 - 1, 0), 0)),
        pl.BlockSpec((1, 8, D), lambda b, i: (b, jnp.minimum((i + 1) * r8, S // 8 - 1), 0)),
        pl.BlockSpec((1, 1, D), lambda b, i: (b, 0, 0)),
        pl.BlockSpec((1, 1, D), lambda b, i: (b, 0, 0)),
    ]
    consts = [norm1.reshape(1, D), pw["wqa"], pw["wkva"], pw["wkr"], pw["wqkv"], pw["wz"], pw["wab"], pw["wabt"],
              q_a_norm.reshape(1, -1), kv_a_norm.reshape(1, -1), pw["wqb"], pw["wkb"], pw["wvb"], pw["qn"], pw["kn"]]
    in_specs += [full(a) for a in consts]
    in_specs += [pl.BlockSpec((tm, LANES), lambda b, i: (i, 0))] * 3
    tail = [pw["conv"], pw["gpar"], pw["gpart"]]
    in_specs += [full(a) for a in tail]
    hspec = pl.BlockSpec((1, MLA_HEADS, tm, LANES), lambda b, i: (b, 0, i, 0))
    out_specs = [hspec, hspec, hspec, tok(nh), tok(nh), tok(nh),
                 pl.BlockSpec((1, nh, tm), lambda b, i: (b, 0, i)), tok(nh), tok(LANES),
                 pl.BlockSpec((1, 16, tm), lambda b, i: (b, 0, i))]
    hshape = jax.ShapeDtypeStruct((B, MLA_HEADS, S, LANES), bf16)
    tshape = jax.ShapeDtypeStruct((B, S, nh), f32)
    out_shape = [hshape, hshape, hshape, tshape, tshape, tshape,
                 jax.ShapeDtypeStruct((B, nh, S), f32), tshape,
                 jax.ShapeDtypeStruct((B, S, LANES), f32), jax.ShapeDtypeStruct((B, 16, S), f32)]
    return pl.pallas_call(
        _inproj_kernel,
        grid=(B, nt),
        in_specs=in_specs,
        out_specs=out_specs,
        out_shape=out_shape,
        scratch_shapes=[pltpu.VMEM((tm + 16, nqkv), f32)],
        compiler_params=_cparams(("parallel", "parallel")),
        name="inproj",
    )(x, x, x, sh1, sc1, *consts, *rope, *tail)


def _attn_kernel(q_ref, k_ref, v_ref, o_ref, s_scr, m_scr, acc_scr):
    S = k_ref.shape[2]
    tk = s_scr.shape[3]
    nk = S // tk
    nh = q_ref.shape[1]
    qs = [q_ref[0, hd] for hd in range(nh)]
    m_scr[...] = jnp.full(m_scr.shape, -jnp.inf, f32)
    acc_scr[...] = jnp.zeros(acc_scr.shape, f32)

    def scores(slot, j):
        off = pl.multiple_of(j * tk, tk)
        for hd in range(nh):
            s_scr[slot, hd] = _dot_nt(qs[hd], k_ref[0, hd, pl.ds(off, tk), :])

    def accumulate(slot, j):
        off = pl.multiple_of(j * tk, tk)
        for hd in range(nh):
            s = s_scr[slot, hd]
            m = m_scr[hd]
            m_new = jnp.maximum(m, jnp.max(s, axis=-1, keepdims=True))
            p = jnp.exp2(s - m_new)
            m_scr[hd] = m_new
            acc_scr[hd] = acc_scr[hd] * jnp.exp2(m - m_new) + _dot(p.astype(bf16), v_ref[0, hd, pl.ds(off, tk), :])

    scores(0, 0)

    def body(jj, carry):
        j = 2 * jj
        scores(1, j + 1)
        accumulate(0, j)
        scores(0, jnp.minimum(j + 2, nk - 1))
        accumulate(1, j + 1)
        return carry

    lax.fori_loop(0, nk // 2, body, 0)
    outs = [acc_scr[hd][:, :MLA_V] / acc_scr[hd][:, MLA_V:MLA_V + 1] for hd in range(nh)]
    o_ref[0] = jnp.concatenate(outs, axis=1).astype(o_ref.dtype)


def _attention(q, k, v):
    B, H, S, _ = q.shape
    tq = min(TQ, S)
    tk = min(TK, S // 2)
    nh = 2
    qspec = pl.BlockSpec((1, nh, tq, LANES), lambda b, h, i: (b, h, i, 0))
    kspec = pl.BlockSpec((1, nh, S, LANES), lambda b, h, i: (b, h, 0, 0))
    return pl.pallas_call(
        _attn_kernel,
        grid=(B, H // nh, S // tq),
        in_specs=[qspec, kspec, kspec],
        out_specs=pl.BlockSpec((1, tq, LANES), lambda b, h, i: (b, i, h)),
        out_shape=jax.ShapeDtypeStruct((B, S, H * MLA_V), bf16),
        scratch_shapes=[pltpu.VMEM((2, nh, tq, tk), f32), pltpu.VMEM((nh, tq, 1), f32),
                        pltpu.VMEM((nh, tq, LANES), f32)],
        compiler_params=_cparams(("parallel", "parallel", "arbitrary")),
        name="attn",
    )(q, k, v)


def _unit_tri_inverse(L, sub_mask):
    C = L.shape[0]
    r = lax.broadcasted_iota(i32, (C, C), 0)
    c = lax.broadcasted_iota(i32, (C, C), 1)
    eye = jnp.where(r == c, 1.0, 0.0)
    Ld = jnp.where(sub_mask, L, 0.0)
    Lo = L - Ld
    T = eye - Ld
    P = Ld
    n = 2
    while n < DN_SUB:
        P = _dot3(P, P)
        T = _dot3(T, eye + P)
        n *= 2
    N = _dot3(T, Lo)
    N2 = _dot3(N, N)
    return _dot3(_dot3(eye - N, eye + N2), T)


def _gdn_kernel(q_ref, k_ref, v_ref, kt_ref, gcol_ref, grow_ref, o_ref, s_ref, *, reverse):
    @pl.when(pl.program_id(1) == 0)
    def _():
        s_ref[...] = jnp.zeros_like(s_ref)

    C = DN_CHUNK
    rows = q_ref.shape[1]
    nch = rows // C
    r = lax.broadcasted_iota(i32, (C, C), 0)
    c = lax.broadcasted_iota(i32, (C, C), 1)
    incl = (c >= r) if reverse else (c <= r)
    strict = (c > r) if reverse else (c < r)
    sub_mask = (r // DN_SUB) == (c // DN_SUB)
    goff = 4 if reverse else 0
    order = range(nch - 1, -1, -1) if reverse else range(nch)
    for ch in order:
        rs = slice(ch * C, (ch + 1) * C)
        for hd in range(DN_HEADS):
            hs = slice(hd * DN_DK, (hd + 1) * DN_DK)
            q = q_ref[0, rs, hs]
            k = k_ref[0, rs, hs]
            v = v_ref[0, rs, hs]
            kt = kt_ref[0, hs, rs]
            gc = gcol_ref[0, rs, goff + hd: goff + hd + 1]
            beta = gcol_ref[0, rs, 8 + goff + hd: 8 + goff + hd + 1]
            gr = grow_ref[0, goff + hd: goff + hd + 1, rs]
            dec = jnp.exp(jnp.where(incl, gc - gr, -jnp.inf))
            kb = k * beta
            kbb = kb.astype(bf16)
            kbf = k.astype(bf16)
            L = jnp.where(strict, _dot_nt(kbb, kbf) * dec, 0.0)
            A = _dot_nt(q.astype(bf16), kbf) * dec
            T = _unit_tri_inverse(L, sub_mask)
            eg = jnp.exp(gc)
            rhs = jnp.concatenate([v * beta, kb * eg], axis=1)
            sol = _dot3(T, rhs)
            u = sol[:, :DN_DV]
            w = sol[:, DN_DV:]
            st = s_ref[hd]
            stb = st.astype(bf16)
            wq = jnp.concatenate([w, q * eg], axis=0).astype(bf16)
            ws = _dot(wq, stb)
            v_new = u - ws[:C]
            vnb = v_new.astype(bf16)
            o_ref[0, rs, hs] = ws[C:] + _dot(A.astype(bf16), vnb)
            g_last = gr[:, 0:1] if reverse else gr[:, C - 1:C]
            kdec_t = (kt * jnp.exp(g_last - gr)).astype(bf16)
            s_ref[hd] = st * jnp.exp(g_last) + _dot(kdec_t, vnb)


def _gdn(dq, dk, dv, dkt, gcol, grow, reverse):
    B, S, W = dq.shape
    rows = min(GDN_ROWS, S)
    n = S // rows
    if reverse:
        idx = lambda b, i: (b, n - 1 - i, 0)
        idx_t = lambda b, i: (b, 0, n - 1 - i)
    else:
        idx = lambda b, i: (b, i, 0)
        idx_t = lambda b, i: (b, 0, i)
    tok = pl.BlockSpec((1, rows, W), idx)
    return pl.pallas_call(
        functools.partial(_gdn_kernel, reverse=reverse),
        grid=(B, n),
        in_specs=[tok, tok, tok, pl.BlockSpec((1, W, rows), idx_t),
                  pl.BlockSpec((1, rows, LANES), idx), pl.BlockSpec((1, 16, rows), idx_t)],
        out_specs=tok,
        out_shape=jax.ShapeDtypeStruct((B, S, W), f32),
        scratch_shapes=[pltpu.VMEM((DN_HEADS, DN_DK, DN_DV), f32)],
        compiler_params=_cparams(("parallel", "arbitrary")),
        name="gdn_bwd" if reverse else "gdn_fwd",
    )(dq, dk, dv, dkt, gcol, grow)


def _pack_bf16_pairs(y):
    w = y.shape[1] // 2
    lo = pltpu.bitcast(y[:, :w].astype(bf16).astype(f32), u32)
    hi = pltpu.bitcast(y[:, w:].astype(bf16).astype(f32), u32)
    return (lo >> 16) | (hi & jnp.uint32(0xFFFF0000))


def _unpack_bf16_pairs(p):
    lo = pltpu.bitcast(p << 16, f32)
    hi = pltpu.bitcast(p & jnp.uint32(0xFFFF0000), f32)
    return jnp.concatenate([lo, hi], axis=1)


ROW_SLABS = 4


def _store_rows(ref, packed):
    n = packed.shape[0]
    for j in range(ROW_SLABS):
        ref[pl.ds(j, n, stride=ROW_SLABS), :] = packed[:, j * LANES:(j + 1) * LANES]


def _load_rows(ref, n):
    return jnp.concatenate([ref[pl.ds(j, n, stride=ROW_SLABS), :] for j in range(ROW_SLABS)], axis=1)


def _outproj_kernel(x_ref, om_ref, of_ref, ob_ref, z_ref, gt_ref, sc_ref, sh_ref,
                    wo_ref, dnn_ref, n2_ref, wr_ref, br_ref,
                    x1_ref, hp_ref, meta_ref, cnt_ref, carry_ref):
    first = (pl.program_id(0) == 0) & (pl.program_id(1) == 0)

    @pl.when(first)
    def _():
        carry_ref[...] = jnp.zeros_like(carry_ref)

    tm = x_ref.shape[1]
    o = of_ref[0] + ob_ref[0]
    z = z_ref[0]
    dnn = dnn_ref[...]
    parts = []
    for hd in range(DN_HEADS):
        sl = slice(hd * DN_DV, (hd + 1) * DN_DV)
        ob = o[:, sl]
        ob = ob * lax.rsqrt(jnp.mean(ob * ob, axis=-1, keepdims=True) + EPS) * dnn
        parts.append(ob * _silu(z[:, sl]))
    odn = jnp.concatenate(parts, axis=1).astype(bf16)
    nm = om_ref.shape[2]
    mixed = _dot(om_ref[0], wo_ref[pl.ds(0, nm), :]) + _dot(odn, wo_ref[pl.ds(nm, odn.shape[1]), :])
    x1 = x_ref[0] + gt_ref[0] * mixed
    x1_ref[0] = x1
    h2 = x1 * lax.rsqrt(jnp.mean(x1 * x1, axis=-1, keepdims=True) + EPS) * n2_ref[...]
    h2 = h2 * (1.0 + sc_ref[0]) + sh_ref[0]
    _store_rows(hp_ref, _pack_bf16_pairs(h2))

    hh, hl = _split2(h2)
    wr = wr_ref[...]
    p1 = _dot(hh, wr)
    p2 = _dot(hl, wr[:, :LANES])
    lane = lax.broadcasted_iota(i32, (tm, LANES), 1)
    logits = p1[:, :LANES] + p1[:, LANES:] + p2 + br_ref[...]
    logits = jnp.where(lane < N_EXPERTS, logits, -jnp.inf)
    vals, idxs = [], []
    work = logits
    for _ in range(TOP_K):
        mx = jnp.max(work, axis=-1, keepdims=True)
        ix = jnp.min(jnp.where(work == mx, lane, LANES), axis=-1, keepdims=True)
        vals.append(mx)
        idxs.append(ix)
        work = jnp.where(lane == ix, -jnp.inf, work)
    es = [jnp.exp(vv - vals[0]) for vv in vals]
    den = es[0] + es[1] + es[2] + es[3]
    multihot = jnp.where(work != logits, 1.0, 0.0)
    rr = lax.broadcasted_iota(i32, (tm, tm), 0)
    cc = lax.broadcasted_iota(i32, (tm, tm), 1)
    below = jnp.where(cc < rr, 1.0, 0.0).astype(bf16)
    prefix = _dot(below, multihot.astype(bf16)) + carry_ref[0:1]
    meta = jnp.zeros((tm, LANES), f32)
    for kk in range(TOP_K):
        rank = jnp.sum(jnp.where(lane == idxs[kk], prefix, 0.0), axis=-1, keepdims=True)
        meta = jnp.where(lane == kk, idxs[kk].astype(f32), meta)
        meta = jnp.where(lane == TOP_K + kk, rank, meta)
        meta = jnp.where(lane == 2 * TOP_K + kk, es[kk] / den, meta)
    meta_ref[...] = meta
    carry = carry_ref[...] + jnp.sum(multihot, axis=0, keepdims=True)
    carry_ref[...] = carry
    cnt_ref[...] = carry


def _outproj(x, o_mla, o_f, o_b, z, gt1, sc2, sh2, w_o, dn_out_norm, norm2, w_router, b_router):
    B, S, D = x.shape
    tm = min(TM_OUT, S)
    nt = S // tm
    T = B * S
    nh = o_f.shape[2]
    wr_hi = w_router.astype(bf16)
    wr_lo = (w_router - wr_hi.astype(f32)).astype(bf16)
    wr = jnp.zeros((D, 2 * LANES), bf16).at[:, :N_EXPERTS].set(wr_hi).at[:, LANES:LANES + N_EXPERTS].set(wr_lo)
    br = jnp.zeros((1, LANES), f32).at[0, :N_EXPERTS].set(b_router)
    consts = [w_o.astype(bf16), dn_out_norm.reshape(1, -1), norm2.reshape(1, D), wr, br]

    def full(a):
        return pl.BlockSpec(a.shape, lambda b, i: (0,) * a.ndim)

    tok = lambda w: pl.BlockSpec((1, tm, w), lambda b, i: (b, i, 0))
    vec = pl.BlockSpec((1, 1, D), lambda b, i: (b, 0, 0))
    flat = lambda w: pl.BlockSpec((tm, w), lambda b, i: (b * nt + i, 0))
    return pl.pallas_call(
        _outproj_kernel,
        grid=(B, nt),
        in_specs=[tok(D), tok(o_mla.shape[2]), tok(nh), tok(nh), tok(nh), vec, vec, vec] + [full(a) for a in consts],
        out_specs=[tok(D), pl.BlockSpec((tm * ROW_SLABS, LANES), lambda b, i: (b * nt + i, 0)), flat(LANES),
                   pl.BlockSpec((8, LANES), lambda b, i: (0, 0))],
        out_shape=[jax.ShapeDtypeStruct((B, S, D), f32), jax.ShapeDtypeStruct((T * ROW_SLABS, LANES), u32),
                   jax.ShapeDtypeStruct((T, LANES), f32), jax.ShapeDtypeStruct((8, LANES), f32)],
        scratch_shapes=[pltpu.VMEM((8, LANES), f32)],
        compiler_params=_cparams(("arbitrary", "arbitrary")),
        name="outproj",
    )(x, o_mla, o_f, o_b, z, gt1, sc2, sh2, *consts)


DMA_UNROLL = 8


def _row_copy(src_ref, s, dst_ref, d, sem):
    return pltpu.make_async_copy(src_ref.at[pl.ds(pl.multiple_of(s * ROW_SLABS, ROW_SLABS), ROW_SLABS)],
                                 dst_ref.at[pl.ds(pl.multiple_of(d * ROW_SLABS, ROW_SLABS), ROW_SLABS)], sem)


def _dispatch_kernel(dest_ref, hp_ref, xb_in_ref, xb_ref, sem):
    del xb_in_ref
    n = dest_ref.shape[0]

    def issue(j, carry):
        _row_copy(hp_ref, j // TOP_K, xb_ref, dest_ref[j], sem).start()
        return carry

    lax.fori_loop(0, n, issue, 0, unroll=DMA_UNROLL)
    for _ in range(TOP_K):
        pltpu.make_async_copy(hp_ref, xb_ref.at[pl.ds(0, hp_ref.shape[0])], sem).wait()


def _dispatch(hp, dest, P):
    T = hp.shape[0] // ROW_SLABS
    tt = min(TT_DISPATCH, T)
    xb0 = jnp.zeros((P * ROW_SLABS, LANES), u32)
    return pl.pallas_call(
        _dispatch_kernel,
        grid=(T // tt,),
        in_specs=[pl.BlockSpec((tt * TOP_K,), lambda i: (i,), memory_space=pltpu.SMEM),
                  pl.BlockSpec((tt * ROW_SLABS, LANES), lambda i: (i, 0)), pl.BlockSpec(memory_space=pl.ANY)],
        out_specs=pl.BlockSpec(memory_space=pl.ANY),
        out_shape=jax.ShapeDtypeStruct((P * ROW_SLABS, LANES), u32),
        scratch_shapes=[pltpu.SemaphoreType.DMA(())],
        input_output_aliases={2: 0},
        compiler_params=pltpu.CompilerParams(dimension_semantics=("arbitrary",), has_side_effects=True),
        name="dispatch",
    )(dest, hp, xb0)


def _expert_kernel(be_ref, nb_ref, x_ref, wg_ref, bg_ref, wu_ref, bu_ref, wd_ref, bd_ref, y_ref):
    b = pl.program_id(0)

    @pl.when(b < nb_ref[0])
    def _():
        x = _unpack_bf16_pairs(_load_rows(x_ref, MOE_BLOCK)).astype(bf16)
        gt = jnp.minimum(_dot(x, wg_ref[0]) + bg_ref[0], SWIGLU_LIMIT)
        up = jnp.clip(_dot(x, wu_ref[0]) + bu_ref[0], -SWIGLU_LIMIT, SWIGLU_LIMIT)
        act = (up + 1.0) * gt * _sigmoid(SWIGLU_ALPHA * gt)
        y = _dot(act.astype(bf16), wd_ref[0]) + bd_ref[0]
        _store_rows(y_ref, _pack_bf16_pairs(y))

    @pl.when(b >= nb_ref[0])
    def _():
        y_ref[...] = jnp.zeros_like(y_ref)


def _experts(xb, block_expert, n_used, wg, bg, wu, bu, wd, bd):
    E, D, F = wg.shape
    nb = xb.shape[0] // (MOE_BLOCK * ROW_SLABS)
    wspec = lambda r, c: pl.BlockSpec((1, r, c), lambda b, be, nu: (be[b], 0, 0))
    xspec = pl.BlockSpec((MOE_BLOCK * ROW_SLABS, LANES), lambda b, be, nu: (b, 0))
    grid_spec = pltpu.PrefetchScalarGridSpec(
        num_scalar_prefetch=2,
        grid=(nb,),
        in_specs=[xspec, wspec(D, F), wspec(1, F), wspec(D, F), wspec(1, F), wspec(F, D), wspec(1, D)],
        out_specs=xspec,
    )
    return pl.pallas_call(
        _expert_kernel,
        grid_spec=grid_spec,
        out_shape=jax.ShapeDtypeStruct(xb.shape, u32),
        compiler_params=_cparams(("arbitrary",)),
        name="experts",
    )(block_expert, n_used, xb, wg, bg.reshape(E, 1, F), wu, bu.reshape(E, 1, F), wd, bd.reshape(E, 1, D))


def _combine_kernel(dest_ref, dnext_ref, yb_ref, meta_ref, x1_ref, gt_ref, o_ref, buf_ref, sem):
    n = dest_ref.shape[0]
    tt = n // TOP_K
    g = pl.program_id(0)
    slot = g % 2

    def gather(idx_ref, s):
        def issue(j, carry):
            _row_copy(yb_ref, idx_ref[j], buf_ref.at[s, j % TOP_K], j // TOP_K, sem.at[s]).start()
            return carry

        lax.fori_loop(0, n, issue, 0, unroll=DMA_UNROLL)

    @pl.when(g == 0)
    def _():
        gather(dest_ref, slot)

    @pl.when(g + 1 < pl.num_programs(0))
    def _():
        gather(dnext_ref, 1 - slot)

    for kk in range(TOP_K):
        pltpu.make_async_copy(yb_ref.at[pl.ds(0, tt * ROW_SLABS)], buf_ref.at[slot, kk], sem.at[slot]).wait()
    meta = meta_ref[...]
    moe = jnp.zeros((tt, x1_ref.shape[2]), f32)
    for kk in range(TOP_K):
        gate = meta[:, 2 * TOP_K + kk: 2 * TOP_K + kk + 1]
        moe = moe + gate * _unpack_bf16_pairs(_load_rows(buf_ref.at[slot, kk], tt))
    o_ref[0] = x1_ref[0] + gt_ref[0] * moe


def _combine(yb, dest, meta, x1, gt2):
    B, S, D = x1.shape
    tt = min(TT_COMBINE, S)
    nt = S // tt
    ng = B * nt
    return pl.pallas_call(
        _combine_kernel,
        grid=(ng,),
        in_specs=[pl.BlockSpec((tt * TOP_K,), lambda g: (g,), memory_space=pltpu.SMEM),
                  pl.BlockSpec((tt * TOP_K,), lambda g: (jnp.minimum(g + 1, ng - 1),), memory_space=pltpu.SMEM),
                  pl.BlockSpec(memory_space=pl.ANY),
                  pl.BlockSpec((tt, LANES), lambda g: (g, 0)),
                  pl.BlockSpec((1, tt, D), lambda g: (g // nt, g % nt, 0)),
                  pl.BlockSpec((1, 1, D), lambda g: (g // nt, 0, 0))],
        out_specs=pl.BlockSpec((1, tt, D), lambda g: (g // nt, g % nt, 0)),
        out_shape=jax.ShapeDtypeStruct((B, S, D), f32),
        scratch_shapes=[pltpu.VMEM((2, TOP_K, tt * ROW_SLABS, LANES), u32), pltpu.SemaphoreType.DMA((2,))],
        compiler_params=_cparams(("arbitrary",)),
        name="combine",
    )(dest, dest, yb, meta, x1, gt2)


def _moe(hp, meta, cnt, x1, gt2, ew):
    T = meta.shape[0]
    TK_ = T * TOP_K
    nb = -(-TK_ // MOE_BLOCK) + N_EXPERTS
    P = nb * MOE_BLOCK
    counts = cnt[0, :N_EXPERTS].astype(i32)
    padded = (counts + MOE_BLOCK - 1) // MOE_BLOCK * MOE_BLOCK
    cum_padded = jnp.cumsum(padded)
    pstart = cum_padded - padded
    top_idx = meta[:, :TOP_K].astype(i32)
    rank = meta[:, TOP_K:2 * TOP_K].astype(i32)
    dest = (pstart[top_idx] + rank).reshape(-1)
    block_expert = jnp.minimum(
        jnp.searchsorted(cum_padded, jnp.arange(nb, dtype=i32) * MOE_BLOCK, side="right"), N_EXPERTS - 1).astype(i32)
    n_used = (cum_padded[-1:] // MOE_BLOCK).astype(i32)
    xb = _dispatch(hp, dest, P)
    yb = _experts(xb, block_expert, n_used, *ew)
    return _combine(yb, dest, meta, x1, gt2)


def _layer(x, c, p, pw, ew, rope):
    B, S, D = x.shape
    mod = _ada(c, p["w_ada"], p["b_ada"]).reshape(B, 6, 1, D)
    sh1, sc1, gt1, sh2, sc2, gt2 = (mod[:, j] for j in range(6))
    q, k, v, dq, dk, dv, dkt, z, gcol, grow = _inproj(
        x, sh1, sc1, p["norm1"], p["q_a_norm"], p["kv_a_norm"], pw, rope)
    o_mla = _attention(q, k, v)
    o_f = _gdn(dq, dk, dv, dkt, gcol, grow, False)
    o_b = _gdn(dq, dk, dv, dkt, gcol, grow, True)
    dnn = jnp.tile(p["dn_out_norm"], 1)
    x1, hp, meta, cnt = _outproj(x, o_mla, o_f, o_b, z, gt1, sc2, sh2, p["w_o"], dnn, p["norm2"],
                                 p["w_router"], p["b_router"])
    return _moe(hp, meta, cnt, x1, gt2, ew)


def kernel(x_prompt, x_sample, c_prompt, c_sample, w_ada, b_ada, norm1, w_in, q_a_norm, w_q_b, kv_a_norm, w_kv_b, q_norm, k_norm, dn_conv, dn_a_log, dn_dt_bias, dn_out_norm, w_o, norm2, w_router, b_router, w_gate, b_gate, w_up, b_up, w_down, b_down):
    y_prompt, y_sample = x_prompt, x_sample
    depth = w_ada.shape[0]
    for l in range(depth):
        p = {"w_ada": w_ada[l], "b_ada": b_ada[l], "norm1": norm1[l], "q_a_norm": q_a_norm[l],
             "kv_a_norm": kv_a_norm[l], "dn_out_norm": dn_out_norm[l], "w_o": w_o[l], "norm2": norm2[l],
             "w_router": w_router[l], "b_router": b_router[l]}
        pw = _prep_weights(w_in[l], w_q_b[l], w_kv_b[l], q_norm[l], k_norm[l], dn_conv[l], dn_a_log[l],
                           dn_dt_bias[l])
        ew = (w_gate[l].astype(bf16), b_gate[l], w_up[l].astype(bf16), b_up[l], w_down[l].astype(bf16), b_down[l])
        y_prompt = _layer(y_prompt, c_prompt, p, pw, ew, _rope_tables(y_prompt.shape[1]))
        y_sample = _layer(y_sample, c_sample, p, pw, ew, _rope_tables(y_sample.shape[1]))
    return (y_prompt, y_sample)
```

```python
import functools
import math

import jax
import jax.numpy as jnp
from jax import lax
from jax.experimental import pallas as pl
from jax.experimental.pallas import tpu as pltpu

f32 = jnp.float32
bf16 = jnp.bfloat16
u32 = jnp.uint32
i32 = jnp.int32

LANES = 128
VMEM_LIMIT = 56 * 1024 * 1024

MLA_HEADS = 8
MLA_Q_LORA = 384
MLA_KV_LORA = 256
MLA_NOPE = 64
MLA_ROPE = 32
MLA_QK = MLA_NOPE + MLA_ROPE
MLA_V = 64
ROPE_THETA = 10000.0
DN_HEADS = 4
DN_DK = 128
DN_DV = 128
DN_CONV = 5
DN_CHUNK = 64
DN_SUB = 16
N_EXPERTS = 32
TOP_K = 4
SWIGLU_LIMIT = 7.0
SWIGLU_ALPHA = 1.702
MOE_BLOCK = 256
EPS = 1e-6

TM_IN = 256
TM_OUT = 256
TQ = 256
TK = 512
GDN_ROWS = 128
GDN_SCAN_ROWS = 256
TT_DISPATCH = 512
TT_COMBINE = 256


def _cparams(sem):
    return pltpu.CompilerParams(dimension_semantics=sem, vmem_limit_bytes=VMEM_LIMIT)


def _split2(x):
    hi = x.astype(bf16)
    lo = (x - hi.astype(f32)).astype(bf16)
    return hi, lo


def _split3(x):
    hi = x.astype(bf16)
    r = x - hi.astype(f32)
    mid = r.astype(bf16)
    lo = (r - mid.astype(f32)).astype(bf16)
    return hi, mid, lo


def _dot(a, b):
    return jnp.dot(a, b, preferred_element_type=f32)


def _dot_nt(a, b):
    return lax.dot_general(a, b, (((1,), (1,)), ((), ())), preferred_element_type=f32)


def _dot3(a, b):
    ah, al = _split2(a)
    bh, bl = _split2(b)
    return _dot(ah, bh) + (_dot(al, bh) + _dot(ah, bl))


def _sigmoid(x):
    return 1.0 / (1.0 + jnp.exp(-x))


def _silu(x):
    return x * _sigmoid(x)


def _softplus(x):
    return jnp.maximum(x, 0.0) + jnp.log(1.0 + jnp.exp(-jnp.abs(x)))


def _ada_kernel(c_ref, w_ref, b_ref, o_ref):
    c = c_ref[...]
    o_ref[...] = _dot3(_silu(c), w_ref[...]) + b_ref[...]


def _ada(c, w_ada, b_ada):
    B, D = c.shape
    N = w_ada.shape[1]
    cp = jnp.zeros((8, D), f32).at[:B].set(c)
    tn = 1024
    out = pl.pallas_call(
        _ada_kernel,
        grid=(N // tn,),
        in_specs=[
            pl.BlockSpec((8, D), lambda j: (0, 0)),
            pl.BlockSpec((D, tn), lambda j: (0, j)),
            pl.BlockSpec((1, tn), lambda j: (0, j)),
        ],
        out_specs=pl.BlockSpec((8, tn), lambda j: (0, j)),
        out_shape=jax.ShapeDtypeStruct((8, N), f32),
        compiler_params=_cparams(("parallel",)),
        name="ada",
    )(cp, w_ada, b_ada.reshape(1, N))
    return out[:B]


def _chunk_tri(n, lower):
    r = lax.broadcasted_iota(i32, (n, n), 0)
    c = lax.broadcasted_iota(i32, (n, n), 1)
    same = (r // DN_CHUNK) == (c // DN_CHUNK)
    tri = (c <= r) if lower else (c >= r)
    return jnp.where(same & tri, 1.0, 0.0).astype(bf16)


def _inproj_kernel(
    x_ref, xp_ref, xn_ref, sh_ref, sc_ref, n1_ref,
    wqa_ref, wkva_ref, wkr_ref, wqkv_ref, wz_ref, wab_ref, wabt_ref,
    qan_ref, kvan_ref, wqb_ref, wkb_ref, wvb_ref, qn_ref, kn_ref,
    cos_ref, sina_ref, sinb_ref, conv_ref, gpar_ref, gpart_ref,
    q_ref, k_ref, v_ref, dq_ref, dk_ref, dv_ref, dkt_ref, z_ref, gcol_ref, grow_ref,
    ext_ref,
):
    i = pl.program_id(1)
    ni = pl.num_programs(1)
    tm = x_ref.shape[1]
    scale1 = 1.0 + sc_ref[0]
    shift1 = sh_ref[0]
    n1 = n1_ref[...]

    def modulate(xv):
        y = xv * lax.rsqrt(jnp.mean(xv * xv, axis=-1, keepdims=True) + EPS)
        return y * n1 * scale1 + shift1

    h = modulate(x_ref[0])
    hb = h.astype(bf16)
    hh = modulate(jnp.concatenate([xp_ref[0], xn_ref[0]], axis=0)).astype(bf16)

    qa = _dot(hb, wqa_ref[...])
    qa = qa * lax.rsqrt(jnp.mean(qa * qa, axis=-1, keepdims=True) + EPS) * qan_ref[...]
    kva = _dot(hb, wkva_ref[...])
    kva = kva * lax.rsqrt(jnp.mean(kva * kva, axis=-1, keepdims=True) + EPS) * kvan_ref[...]
    kr = _dot(hb, wkr_ref[...])
    qh = _dot(qa.astype(bf16), wqb_ref[...])
    kvb = kva.astype(bf16)
    kh = _dot(kvb, wkb_ref[...])
    vh = _dot(kvb, wvb_ref[...])
    cos = cos_ref[...]
    sina = sina_ref[...]
    sinb = sinb_ref[...]
    qg = qn_ref[...]
    kg = kn_ref[...]
    lane = lax.broadcasted_iota(i32, (tm, LANES), 1)
    q_scale = MLA_QK ** -0.5 * math.log2(math.e)

    def norm_rope(blk, gain):
        ss = jnp.sum(blk * blk, axis=-1, keepdims=True) * (1.0 / MLA_QK)
        y = blk * lax.rsqrt(ss + EPS) * gain
        return y * cos + pltpu.roll(y, LANES - MLA_ROPE // 2, 1) * sina + pltpu.roll(y, MLA_ROPE // 2, 1) * sinb

    for hd in range(MLA_HEADS):
        sl = slice(hd * LANES, (hd + 1) * LANES)
        q_ref[0, hd] = (norm_rope(qh[:, sl], qg) * q_scale).astype(bf16)
        k_ref[0, hd] = norm_rope(kh[:, sl] + kr, kg).astype(bf16)
        vblk = vh[:, hd * MLA_V:(hd + 1) * MLA_V]
        vpad = jnp.concatenate([vblk, jnp.zeros((tm, LANES - MLA_V), f32)], axis=1)
        v_ref[0, hd] = jnp.where(lane == MLA_V, 1.0, vpad).astype(bf16)

    ext_ref[pl.ds(8, tm), :] = _dot(hb, wqkv_ref[...])
    halo = _dot(hh, wqkv_ref[...])
    ext_ref[pl.ds(0, 8), :] = jnp.where(i == 0, 0.0, halo[:8])
    ext_ref[pl.ds(8 + tm, 8), :] = jnp.where(i == ni - 1, 0.0, halo[8:])
    pad = (DN_CONV - 1) // 2
    cw = conv_ref[...]
    acc = ext_ref[pl.ds(8 - pad, tm), :] * cw[0:1]
    for j in range(1, DN_CONV):
        acc = acc + ext_ref[pl.ds(8 - pad + j, tm), :] * cw[j:j + 1]
    act = _silu(acc)
    nqk = DN_HEADS * DN_DK
    for hd in range(DN_HEADS):
        sl = slice(hd * DN_DK, (hd + 1) * DN_DK)
        qb = act[:, sl]
        qb = qb * lax.rsqrt(jnp.sum(qb * qb, axis=-1, keepdims=True) + EPS) * (DN_DK ** -0.5)
        dq_ref[0, :, sl] = qb
        kb = act[:, nqk + hd * DN_DK: nqk + (hd + 1) * DN_DK]
        kb = kb * lax.rsqrt(jnp.sum(kb * kb, axis=-1, keepdims=True) + EPS)
        dk_ref[0, :, sl] = kb
        dkt_ref[0, sl, :] = kb.T
    dv_ref[0] = act[:, 2 * nqk:]
    z_ref[0] = _dot(hb, wz_ref[...])

    hlo = (h - hb.astype(f32)).astype(bf16)
    wab = wab_ref[...]
    p1 = _dot(hb, wab)
    p2 = _dot(hlo, wab)
    ab = p1 + pltpu.roll(p1, LANES - 16, 1) + p2
    gpar = gpar_ref[...]
    lane16 = lax.broadcasted_iota(i32, (tm, LANES), 1)
    gval = jnp.where(lane16 < 8, -gpar[0:1] * _softplus(ab + gpar[1:2]), _sigmoid(ab))
    g3 = _split3(gval)
    lo_tri = _chunk_tri(tm, True)
    up_tri = _chunk_tri(tm, False)
    pre = _dot(lo_tri, g3[0]) + (_dot(lo_tri, g3[1]) + _dot(lo_tri, g3[2]))
    suf = _dot(up_tri, g3[0]) + (_dot(up_tri, g3[1]) + _dot(up_tri, g3[2]))
    gcol_ref[0] = jnp.where(lane16 < 4, pre, jnp.where(lane16 < 8, suf, gval))

    wabt = wabt_ref[...]
    r1 = _dot_nt(wabt, hb)
    r2 = _dot_nt(wabt[:16], hlo)
    abt = r1[:16] + r1[16:] + r2
    gpt = gpart_ref[...]
    row16 = lax.broadcasted_iota(i32, (16, tm), 0)
    gvt = jnp.where(row16 < 8, -gpt[:, 0:1] * _softplus(abt + gpt[:, 1:2]), _sigmoid(abt))
    t3 = _split3(gvt)
    pre_t = _dot(t3[0], up_tri) + (_dot(t3[1], up_tri) + _dot(t3[2], up_tri))
    suf_t = _dot(t3[0], lo_tri) + (_dot(t3[1], lo_tri) + _dot(t3[2], lo_tri))
    grow_ref[0] = jnp.where(row16 < 4, pre_t, jnp.where(row16 < 8, suf_t, gvt))


def _prep_weights(w_in, w_q_b, w_kv_b, q_norm, k_norm, dn_conv, dn_a_log, dn_dt_bias):
    D = w_in.shape[0]
    i0 = MLA_Q_LORA
    i1 = i0 + MLA_KV_LORA
    i2 = i1 + MLA_ROPE
    nqkv = DN_HEADS * (2 * DN_DK + DN_DV)
    i3 = i2 + nqkv
    i4 = i3 + DN_HEADS * DN_DV
    wqa = w_in[:, :i0].astype(bf16)
    wkva = w_in[:, i0:i1].astype(bf16)
    wkr = jnp.zeros((D, LANES), f32).at[:, MLA_NOPE:MLA_QK].set(w_in[:, i1:i2]).astype(bf16)
    wqkv = w_in[:, i2:i3].astype(bf16)
    wz = w_in[:, i3:i4].astype(bf16)
    wab_f = w_in[:, i4:]
    wab_hi = wab_f.astype(bf16)
    wab_lo = (wab_f - wab_hi.astype(f32)).astype(bf16)
    wab = jnp.zeros((D, LANES), bf16).at[:, :16].set(wab_hi).at[:, 16:32].set(wab_lo)
    wabt = jnp.concatenate([wab_hi.T, wab_lo.T], axis=0)
    wqb = w_q_b.reshape(MLA_Q_LORA, MLA_HEADS, MLA_QK)
    wqb = jnp.pad(wqb, ((0, 0), (0, 0), (0, LANES - MLA_QK))).reshape(MLA_Q_LORA, MLA_HEADS * LANES).astype(bf16)
    wkv = w_kv_b.reshape(MLA_KV_LORA, MLA_HEADS, MLA_NOPE + MLA_V)
    wkb = jnp.pad(wkv[:, :, :MLA_NOPE], ((0, 0), (0, 0), (0, LANES - MLA_NOPE)))
    wkb = wkb.reshape(MLA_KV_LORA, MLA_HEADS * LANES).astype(bf16)
    wvb = wkv[:, :, MLA_NOPE:].reshape(MLA_KV_LORA, MLA_HEADS * MLA_V).astype(bf16)
    qn = jnp.pad(q_norm, (0, LANES - MLA_QK)).reshape(1, LANES)
    kn = jnp.pad(k_norm, (0, LANES - MLA_QK)).reshape(1, LANES)
    conv = jnp.pad(dn_conv, ((0, 8 - DN_CONV), (0, 0)))
    ea = jnp.exp(dn_a_log.astype(f32)).reshape(-1)
    dtb = dn_dt_bias.astype(f32).reshape(-1)
    gpar = jnp.zeros((8, LANES), f32).at[0, :8].set(ea).at[1, :8].set(dtb)
    gpart = jnp.zeros((16, LANES), f32).at[:8, 0].set(ea).at[:8, 1].set(dtb)
    return dict(wqa=wqa, wkva=wkva, wkr=wkr, wqkv=wqkv, wz=wz, wab=wab, wabt=wabt, wqb=wqb, wkb=wkb,
                wvb=wvb, qn=qn, kn=kn, conv=conv, gpar=gpar, gpart=gpart)


def _rope_tables(S):
    half = MLA_ROPE // 2
    freq = ROPE_THETA ** (-jnp.arange(half, dtype=f32) / half)
    ang = jnp.arange(S, dtype=f32)[:, None] * freq[None, :]
    cos, sin = jnp.cos(ang), jnp.sin(ang)
    z = jnp.zeros((S, LANES), f32)
    cos_t = z.at[:, :MLA_NOPE].set(1.0).at[:, MLA_NOPE:MLA_NOPE + half].set(cos).at[:, MLA_NOPE + half:MLA_QK].set(cos)
    sina = z.at[:, MLA_NOPE:MLA_NOPE + half].set(-sin)
    sinb = z.at[:, MLA_NOPE + half:MLA_QK].set(sin)
    return cos_t, sina, sinb


def _inproj(x, sh1, sc1, norm1, q_a_norm, kv_a_norm, pw, rope):
    B, S, D = x.shape
    tm = min(TM_IN, S)
    nt = S // tm
    r8 = tm // 8
    nqkv = DN_HEADS * (2 * DN_DK + DN_DV)
    nh = DN_HEADS * DN_DK

    def full(a):
        return pl.BlockSpec(a.shape, lambda b, i: (0,) * a.ndim)

    tok = lambda w: pl.BlockSpec((1, tm, w), lambda b, i: (b, i, 0))
    in_specs = [
        tok(D),
        pl.BlockSpec((1, 8, D), lambda b, i: (b, jnp.maximum(i * r8 - 1, 0), 0)),
        pl.BlockSpec((1, 8, D), lambda b, i: (b, jnp.minimum((i + 1) * r8, S // 8 - 1), 0)),
        pl.BlockSpec((1, 1, D), lambda b, i: (b, 0, 0)),
        pl.BlockSpec((1, 1, D), lambda b, i: (b, 0, 0)),
    ]
    consts = [norm1.reshape(1, D), pw["wqa"], pw["wkva"], pw["wkr"], pw["wqkv"], pw["wz"], pw["wab"], pw["wabt"],
              q_a_norm.reshape(1, -1), kv_a_norm.reshape(1, -1), pw["wqb"], pw["wkb"], pw["wvb"], pw["qn"], pw["kn"]]
    in_specs += [full(a) for a in consts]
    in_specs += [pl.BlockSpec((tm, LANES), lambda b, i: (i, 0))] * 3
    tail = [pw["conv"], pw["gpar"], pw["gpart"]]
    in_specs += [full(a) for a in tail]
    hspec = pl.BlockSpec((1, MLA_HEADS, tm, LANES), lambda b, i: (b, 0, i, 0))
    out_specs = [hspec, hspec, hspec, tok(nh), tok(nh), tok(nh),
                 pl.BlockSpec((1, nh, tm), lambda b, i: (b, 0, i)), tok(nh), tok(LANES),
                 pl.BlockSpec((1, 16, tm), lambda b, i: (b, 0, i))]
    hshape = jax.ShapeDtypeStruct((B, MLA_HEADS, S, LANES), bf16)
    tshape = jax.ShapeDtypeStruct((B, S, nh), f32)
    out_shape = [hshape, hshape, hshape, tshape, tshape, tshape,
                 jax.ShapeDtypeStruct((B, nh, S), f32), tshape,
                 jax.ShapeDtypeStruct((B, S, LANES), f32), jax.ShapeDtypeStruct((B, 16, S), f32)]
    return pl.pallas_call(
        _inproj_kernel,
        grid=(B, nt),
        in_specs=in_specs,
        out_specs=out_specs,
        out_shape=out_shape,
        scratch_shapes=[pltpu.VMEM((tm + 16, nqkv), f32)],
        compiler_params=_cparams(("parallel", "parallel")),
        name="inproj",
    )(x, x, x, sh1, sc1, *consts, *rope, *tail)


def _attn_kernel(q_ref, k_ref, v_ref, o_ref, s_scr, m_scr, acc_scr):
    S = k_ref.shape[2]
    tk = s_scr.shape[3]
    nk = S // tk
    nh = q_ref.shape[1]
    qs = [q_ref[0, hd] for hd in range(nh)]
    m_scr[...] = jnp.full(m_scr.shape, -jnp.inf, f32)
    acc_scr[...] = jnp.zeros(acc_scr.shape, f32)

    def scores(slot, j):
        off = pl.multiple_of(j * tk, tk)
        for hd in range(nh):
            s_scr[slot, hd] = _dot_nt(qs[hd], k_ref[0, hd, pl.ds(off, tk), :])

    def accumulate(slot, j):
        off = pl.multiple_of(j * tk, tk)
        for hd in range(nh):
            s = s_scr[slot, hd]
            m = m_scr[hd]
            m_new = jnp.maximum(m, jnp.max(s, axis=-1, keepdims=True))
            p = jnp.exp2(s - m_new)
            m_scr[hd] = m_new
            acc_scr[hd] = acc_scr[hd] * jnp.exp2(m - m_new) + _dot(p.astype(bf16), v_ref[0, hd, pl.ds(off, tk), :])

    scores(0, 0)

    def body(jj, carry):
        j = 2 * jj
        scores(1, j + 1)
        accumulate(0, j)
        scores(0, jnp.minimum(j + 2, nk - 1))
        accumulate(1, j + 1)
        return carry

    lax.fori_loop(0, nk // 2, body, 0)
    outs = [acc_scr[hd][:, :MLA_V] / acc_scr[hd][:, MLA_V:MLA_V + 1] for hd in range(nh)]
    o_ref[0] = jnp.concatenate(outs, axis=1).astype(o_ref.dtype)


def _attention(q, k, v):
    B, H, S, _ = q.shape
    tq = min(TQ, S)
    tk = min(TK, S // 2)
    nh = 2
    qspec = pl.BlockSpec((1, nh, tq, LANES), lambda b, h, i: (b, h, i, 0))
    kspec = pl.BlockSpec((1, nh, S, LANES), lambda b, h, i: (b, h, 0, 0))
    return pl.pallas_call(
        _attn_kernel,
        grid=(B, H // nh, S // tq),
        in_specs=[qspec, kspec, kspec],
        out_specs=pl.BlockSpec((1, tq, LANES), lambda b, h, i: (b, i, h)),
        out_shape=jax.ShapeDtypeStruct((B, S, H * MLA_V), bf16),
        scratch_shapes=[pltpu.VMEM((2, nh, tq, tk), f32), pltpu.VMEM((nh, tq, 1), f32),
                        pltpu.VMEM((nh, tq, LANES), f32)],
        compiler_params=_cparams(("parallel", "parallel", "arbitrary")),
        name="attn",
    )(q, k, v)


def _unit_tri_inverses(Ls, sub_mask):
    C = Ls[0].shape[0]
    r = lax.broadcasted_iota(i32, (C, C), 0)
    c = lax.broadcasted_iota(i32, (C, C), 1)
    eye = jnp.where(r == c, 1.0, 0.0)
    Lds = [jnp.where(sub_mask, L, 0.0) for L in Ls]
    Los = [L - Ld for L, Ld in zip(Ls, Lds)]
    Ts = [eye - Ld for Ld in Lds]
    Ps = Lds
    n = 2
    while n < DN_SUB:
        Ps = [_dot3k(P, P) for P in Ps]
        Ts = [_dot3k(T, eye + P) for T, P in zip(Ts, Ps)]
        n *= 2
    Ns = [_dot3k(T, Lo) for T, Lo in zip(Ts, Los)]
    N2s = [_dot3k(N, N) for N in Ns]
    Ms = [_dot3k(eye - N, eye + N2) for N, N2 in zip(Ns, N2s)]
    return [_dot3k(M, T) for M, T in zip(Ms, Ts)]


def _dot3k(a, b):
    ah = a.astype(bf16).astype(f32)
    lhs = jnp.concatenate([a, a - ah, a], axis=1).astype(bf16)
    bh = b.astype(bf16)
    bl = (b - bh.astype(f32)).astype(bf16)
    rhs = jnp.concatenate([bh, bh, bl], axis=0)
    return _dot(lhs, rhs)


def _gdn_prep_kernel(q_ref, k_ref, v_ref, kt_ref, gcol_ref, grow_ref,
                     uf_ref, wf_ref, qf_ref, af_ref, ktf_ref, ub_ref, wb_ref, qb_ref, ab_ref, ktb_ref, dl_ref):
    C = DN_CHUNK
    nch = q_ref.shape[1] // C
    r = lax.broadcasted_iota(i32, (C, C), 0)
    c = lax.broadcasted_iota(i32, (C, C), 1)
    sub_mask = (r // DN_SUB) == (c // DN_SUB)
    outs = ((uf_ref, wf_ref, qf_ref, af_ref, ktf_ref), (ub_ref, wb_ref, qb_ref, ab_ref, ktb_ref))
    chains = [(ch, hd) for ch in range(nch) for hd in range(DN_HEADS)]
    kks = {}
    for ch, hd in chains:
        rs = slice(ch * C, (ch + 1) * C)
        hs = slice(hd * DN_DK, (hd + 1) * DN_DK)
        kbf = k_ref[0, rs, hs].astype(bf16)
        kks[ch, hd] = _dot_nt(kbf, kbf)
        qk = _dot_nt(q_ref[0, rs, hs].astype(bf16), kbf)
        kt = kt_ref[0, hs, rs]
        for d, reverse in enumerate((False, True)):
            _, _, _, a_ref, kdt_ref = outs[d]
            incl = (c >= r) if reverse else (c <= r)
            gc = gcol_ref[0, rs, 4 * d + hd: 4 * d + hd + 1]
            gr = grow_ref[0, 4 * d + hd: 4 * d + hd + 1, rs]
            dec = jnp.exp(jnp.where(incl, gc - gr, -jnp.inf))
            a_ref[0, hd, rs, :] = (qk * dec).astype(bf16)
            g_last = gr[:, 0:1] if reverse else gr[:, C - 1:C]
            kdt_ref[0, ch, hs, :] = (kt * jnp.exp(g_last - gr)).astype(bf16)
    for ch in range(nch):
        rs = slice(ch * C, (ch + 1) * C)
        dls = []
        for d, reverse in enumerate((False, True)):
            for hd in range(DN_HEADS):
                gr = grow_ref[0, 4 * d + hd: 4 * d + hd + 1, rs]
                g_last = gr[:, 0:1] if reverse else gr[:, C - 1:C]
                dls.append(jnp.broadcast_to(jnp.exp(g_last), (1, LANES)))
        dl_ref[0, ch] = jnp.concatenate(dls, axis=0)

    full = [(ch, hd, d) for ch, hd in chains for d in range(2)]
    Ls = []
    for ch, hd, d in full:
        rs = slice(ch * C, (ch + 1) * C)
        strict = (c > r) if d else (c < r)
        gc = gcol_ref[0, rs, 4 * d + hd: 4 * d + hd + 1]
        beta = gcol_ref[0, rs, 8 + 4 * d + hd: 8 + 4 * d + hd + 1]
        gr = grow_ref[0, 4 * d + hd: 4 * d + hd + 1, rs]
        dec = jnp.exp(jnp.where(strict, gc - gr, -jnp.inf))
        Ls.append(beta * kks[ch, hd] * dec)
    Ts = _unit_tri_inverses(Ls, sub_mask)
    sols = []
    for (ch, hd, d), T in zip(full, Ts):
        rs = slice(ch * C, (ch + 1) * C)
        hs = slice(hd * DN_DK, (hd + 1) * DN_DK)
        gc = gcol_ref[0, rs, 4 * d + hd: 4 * d + hd + 1]
        beta = gcol_ref[0, rs, 8 + 4 * d + hd: 8 + 4 * d + hd + 1]
        eg = jnp.exp(gc)
        k = k_ref[0, rs, hs]
        outs[d][2][0, rs, hs] = (q_ref[0, rs, hs] * eg).astype(bf16)
        sols.append(_dot3k(T, jnp.concatenate([v_ref[0, rs, hs] * beta, k * (beta * eg)], axis=1)))
    for (ch, hd, d), sol in zip(full, sols):
        rs = slice(ch * C, (ch + 1) * C)
        hs = slice(hd * DN_DK, (hd + 1) * DN_DK)
        outs[d][0][0, rs, hs] = sol[:, :DN_DV]
        outs[d][1][0, rs, hs] = sol[:, DN_DV:].astype(bf16)


def _gdn_scan_kernel(uf_ref, wf_ref, qf_ref, af_ref, ktf_ref, dlf_ref,
                     ub_ref, wb_ref, qb_ref, ab_ref, ktb_ref, dlb_ref, of_ref, ob_ref, s_ref):
    @pl.when(pl.program_id(1) == 0)
    def _():
        s_ref[...] = jnp.zeros_like(s_ref)

    C = DN_CHUNK
    nch = uf_ref.shape[1] // C
    dirs = ((uf_ref, wf_ref, qf_ref, af_ref, ktf_ref, dlf_ref, of_ref),
            (ub_ref, wb_ref, qb_ref, ab_ref, ktb_ref, dlb_ref, ob_ref))
    chains = [(d, hd) for d in range(2) for hd in range(DN_HEADS)]
    sts = [s_ref[d, hd] for d, hd in chains]
    for step in range(nch):
        def sl(d, hd):
            ch = step if d == 0 else nch - 1 - step
            return ch, slice(ch * C, (ch + 1) * C), slice(hd * DN_DK, (hd + 1) * DN_DK)

        wss = []
        for (d, hd), st in zip(chains, sts):
            ch, rs, hs = sl(d, hd)
            wq = jnp.concatenate([dirs[d][1][0, rs, hs], dirs[d][2][0, rs, hs]], axis=0)
            wss.append(_dot(wq, st.astype(bf16)))
        vnbs = []
        for (d, hd), ws in zip(chains, wss):
            ch, rs, hs = sl(d, hd)
            vnbs.append((dirs[d][0][0, rs, hs] - ws[:C]).astype(bf16))
        new = []
        for (d, hd), st, ws, vnb in zip(chains, sts, wss, vnbs):
            ch, rs, hs = sl(d, hd)
            dirs[d][6][0, rs, hs] = ws[C:] + _dot(dirs[d][3][0, hd, rs, :], vnb)
            dl = dirs[d][5][0, ch, d * DN_HEADS + hd: d * DN_HEADS + hd + 1, :]
            new.append(st * dl + _dot(dirs[d][4][0, ch, hs, :], vnb))
        sts = new
    for (d, hd), st in zip(chains, sts):
        s_ref[d, hd] = st


def _gdn(dq, dk, dv, dkt, gcol, grow):
    B, S, W = dq.shape
    C = DN_CHUNK
    rows = min(GDN_ROWS, S)
    n = S // rows
    nch = rows // C
    tok = pl.BlockSpec((1, rows, W), lambda b, i: (b, i, 0))
    aspec = pl.BlockSpec((1, DN_HEADS, rows, C), lambda b, i: (b, 0, i, 0))
    kspec = pl.BlockSpec((1, nch, W, C), lambda b, i: (b, i, 0, 0))
    dspec = pl.BlockSpec((1, nch, 2 * DN_HEADS, LANES), lambda b, i: (b, i, 0, 0))
    t32 = jax.ShapeDtypeStruct((B, S, W), f32)
    t16 = jax.ShapeDtypeStruct((B, S, W), bf16)
    ashape = jax.ShapeDtypeStruct((B, DN_HEADS, S, C), bf16)
    kshape = jax.ShapeDtypeStruct((B, S // C, W, C), bf16)
    per_dir_specs = [tok, tok, tok, aspec, kspec]
    per_dir_shapes = [t32, t16, t16, ashape, kshape]
    prep = pl.pallas_call(
        _gdn_prep_kernel,
        grid=(B, n),
        in_specs=[tok, tok, tok, pl.BlockSpec((1, W, rows), lambda b, i: (b, 0, i)),
                  pl.BlockSpec((1, rows, LANES), lambda b, i: (b, i, 0)),
                  pl.BlockSpec((1, 16, rows), lambda b, i: (b, 0, i))],
        out_specs=per_dir_specs * 2 + [dspec],
        out_shape=per_dir_shapes * 2 + [jax.ShapeDtypeStruct((B, S // C, 2 * DN_HEADS, LANES), f32)],
        compiler_params=_cparams(("parallel", "parallel")),
        name="gdn_prep",
    )(dq, dk, dv, dkt, gcol, grow)
    fwd, bwd, dl = prep[:5], prep[5:10], prep[10]

    rows_s = min(GDN_SCAN_ROWS, S)
    ns = S // rows_s
    nchs = rows_s // C

    def specs(rev):
        blk = (lambda i: ns - 1 - i) if rev else (lambda i: i)
        return [pl.BlockSpec((1, rows_s, W), lambda b, i: (b, blk(i), 0))] * 3 + [
            pl.BlockSpec((1, DN_HEADS, rows_s, C), lambda b, i: (b, 0, blk(i), 0)),
            pl.BlockSpec((1, nchs, W, C), lambda b, i: (b, blk(i), 0, 0)),
            pl.BlockSpec((1, nchs, 2 * DN_HEADS, LANES), lambda b, i: (b, blk(i), 0, 0))]

    return pl.pallas_call(
        _gdn_scan_kernel,
        grid=(B, ns),
        in_specs=specs(False) + specs(True),
        out_specs=[specs(False)[0], specs(True)[0]],
        out_shape=[t32, t32],
        scratch_shapes=[pltpu.VMEM((2, DN_HEADS, DN_DK, DN_DV), f32)],
        compiler_params=_cparams(("parallel", "arbitrary")),
        name="gdn_scan",
    )(*fwd, dl, *bwd, dl)


def _pack_bf16_pairs(y):
    w = y.shape[1] // 2
    lo = pltpu.bitcast(y[:, :w].astype(bf16).astype(f32), u32)
    hi = pltpu.bitcast(y[:, w:].astype(bf16).astype(f32), u32)
    return (lo >> 16) | (hi & jnp.uint32(0xFFFF0000))


def _unpack_bf16_pairs(p):
    lo = pltpu.bitcast(p << 16, f32)
    hi = pltpu.bitcast(p & jnp.uint32(0xFFFF0000), f32)
    return jnp.concatenate([lo, hi], axis=1)


ROW_SLABS = 4


def _store_rows(ref, packed):
    n = packed.shape[0]
    for j in range(ROW_SLABS):
        ref[pl.ds(j, n, stride=ROW_SLABS), :] = packed[:, j * LANES:(j + 1) * LANES]


def _load_rows(ref, n):
    return jnp.concatenate([ref[pl.ds(j, n, stride=ROW_SLABS), :] for j in range(ROW_SLABS)], axis=1)


def _outproj_kernel(x_ref, om_ref, of_ref, ob_ref, z_ref, gt_ref, sc_ref, sh_ref,
                    wo_ref, dnn_ref, n2_ref, wr_ref, br_ref,
                    x1_ref, hp_ref, meta_ref, cnt_ref, carry_ref):
    first = (pl.program_id(0) == 0) & (pl.program_id(1) == 0)

    @pl.when(first)
    def _():
        carry_ref[...] = jnp.zeros_like(carry_ref)

    tm = x_ref.shape[1]
    o = of_ref[0] + ob_ref[0]
    z = z_ref[0]
    dnn = dnn_ref[...]
    parts = []
    for hd in range(DN_HEADS):
        sl = slice(hd * DN_DV, (hd + 1) * DN_DV)
        ob = o[:, sl]
        ob = ob * lax.rsqrt(jnp.mean(ob * ob, axis=-1, keepdims=True) + EPS) * dnn
        parts.append(ob * _silu(z[:, sl]))
    odn = jnp.concatenate(parts, axis=1).astype(bf16)
    nm = om_ref.shape[2]
    mixed = _dot(om_ref[0], wo_ref[pl.ds(0, nm), :]) + _dot(odn, wo_ref[pl.ds(nm, odn.shape[1]), :])
    x1 = x_ref[0] + gt_ref[0] * mixed
    x1_ref[0] = x1
    h2 = x1 * lax.rsqrt(jnp.mean(x1 * x1, axis=-1, keepdims=True) + EPS) * n2_ref[...]
    h2 = h2 * (1.0 + sc_ref[0]) + sh_ref[0]
    _store_rows(hp_ref, _pack_bf16_pairs(h2))

    hh, hl = _split2(h2)
    wr = wr_ref[...]
    p1 = _dot(hh, wr)
    p2 = _dot(hl, wr[:, :LANES])
    lane = lax.broadcasted_iota(i32, (tm, LANES), 1)
    logits = p1[:, :LANES] + p1[:, LANES:] + p2 + br_ref[...]
    logits = jnp.where(lane < N_EXPERTS, logits, -jnp.inf)
    vals, idxs = [], []
    work = logits
    for _ in range(TOP_K):
        mx = jnp.max(work, axis=-1, keepdims=True)
        ix = jnp.min(jnp.where(work == mx, lane, LANES), axis=-1, keepdims=True)
        vals.append(mx)
        idxs.append(ix)
        work = jnp.where(lane == ix, -jnp.inf, work)
    es = [jnp.exp(vv - vals[0]) for vv in vals]
    den = es[0] + es[1] + es[2] + es[3]
    multihot = jnp.where(work != logits, 1.0, 0.0)
    rr = lax.broadcasted_iota(i32, (tm, tm), 0)
    cc = lax.broadcasted_iota(i32, (tm, tm), 1)
    below = jnp.where(cc < rr, 1.0, 0.0).astype(bf16)
    prefix = _dot(below, multihot.astype(bf16)) + carry_ref[0:1]
    meta = jnp.zeros((tm, LANES), f32)
    for kk in range(TOP_K):
        rank = jnp.sum(jnp.where(lane == idxs[kk], prefix, 0.0), axis=-1, keepdims=True)
        meta = jnp.where(lane == kk, idxs[kk].astype(f32), meta)
        meta = jnp.where(lane == TOP_K + kk, rank, meta)
        meta = jnp.where(lane == 2 * TOP_K + kk, es[kk] / den, meta)
    meta_ref[...] = meta
    carry = carry_ref[...] + jnp.sum(multihot, axis=0, keepdims=True)
    carry_ref[...] = carry
    cnt_ref[...] = carry


def _outproj(x, o_mla, o_f, o_b, z, gt1, sc2, sh2, w_o, dn_out_norm, norm2, w_router, b_router):
    B, S, D = x.shape
    tm = min(TM_OUT, S)
    nt = S // tm
    T = B * S
    nh = o_f.shape[2]
    wr_hi = w_router.astype(bf16)
    wr_lo = (w_router - wr_hi.astype(f32)).astype(bf16)
    wr = jnp.zeros((D, 2 * LANES), bf16).at[:, :N_EXPERTS].set(wr_hi).at[:, LANES:LANES + N_EXPERTS].set(wr_lo)
    br = jnp.zeros((1, LANES), f32).at[0, :N_EXPERTS].set(b_router)
    consts = [w_o.astype(bf16), dn_out_norm.reshape(1, -1), norm2.reshape(1, D), wr, br]

    def full(a):
        return pl.BlockSpec(a.shape, lambda b, i: (0,) * a.ndim)

    tok = lambda w: pl.BlockSpec((1, tm, w), lambda b, i: (b, i, 0))
    vec = pl.BlockSpec((1, 1, D), lambda b, i: (b, 0, 0))
    flat = lambda w: pl.BlockSpec((tm, w), lambda b, i: (b * nt + i, 0))
    return pl.pallas_call(
        _outproj_kernel,
        grid=(B, nt),
        in_specs=[tok(D), tok(o_mla.shape[2]), tok(nh), tok(nh), tok(nh), vec, vec, vec] + [full(a) for a in consts],
        out_specs=[tok(D), pl.BlockSpec((tm * ROW_SLABS, LANES), lambda b, i: (b * nt + i, 0)), flat(LANES),
                   pl.BlockSpec((8, LANES), lambda b, i: (0, 0))],
        out_shape=[jax.ShapeDtypeStruct((B, S, D), f32), jax.ShapeDtypeStruct((T * ROW_SLABS, LANES), u32),
                   jax.ShapeDtypeStruct((T, LANES), f32), jax.ShapeDtypeStruct((8, LANES), f32)],
        scratch_shapes=[pltpu.VMEM((8, LANES), f32)],
        compiler_params=_cparams(("arbitrary", "arbitrary")),
        name="outproj",
    )(x, o_mla, o_f, o_b, z, gt1, sc2, sh2, *consts)


DMA_UNROLL = 8


def _row_copy(src_ref, s, dst_ref, d, sem):
    return pltpu.make_async_copy(src_ref.at[pl.ds(pl.multiple_of(s * ROW_SLABS, ROW_SLABS), ROW_SLABS)],
                                 dst_ref.at[pl.ds(pl.multiple_of(d * ROW_SLABS, ROW_SLABS), ROW_SLABS)], sem)


def _dispatch_kernel(dest_ref, hp_ref, xb_in_ref, xb_ref, sem):
    del xb_in_ref
    n = dest_ref.shape[0]

    def issue(j, carry):
        _row_copy(hp_ref, j // TOP_K, xb_ref, dest_ref[j], sem).start()
        return carry

    lax.fori_loop(0, n, issue, 0, unroll=DMA_UNROLL)
    for _ in range(TOP_K):
        pltpu.make_async_copy(hp_ref, xb_ref.at[pl.ds(0, hp_ref.shape[0])], sem).wait()


def _dispatch(hp, dest, P):
    T = hp.shape[0] // ROW_SLABS
    tt = min(TT_DISPATCH, T)
    xb0 = jnp.zeros((P * ROW_SLABS, LANES), u32)
    return pl.pallas_call(
        _dispatch_kernel,
        grid=(T // tt,),
        in_specs=[pl.BlockSpec((tt * TOP_K,), lambda i: (i,), memory_space=pltpu.SMEM),
                  pl.BlockSpec((tt * ROW_SLABS, LANES), lambda i: (i, 0)), pl.BlockSpec(memory_space=pl.ANY)],
        out_specs=pl.BlockSpec(memory_space=pl.ANY),
        out_shape=jax.ShapeDtypeStruct((P * ROW_SLABS, LANES), u32),
        scratch_shapes=[pltpu.SemaphoreType.DMA(())],
        input_output_aliases={2: 0},
        compiler_params=pltpu.CompilerParams(dimension_semantics=("arbitrary",), has_side_effects=True),
        name="dispatch",
    )(dest, hp, xb0)


def _expert_kernel(be_ref, nb_ref, x_ref, wg_ref, bg_ref, wu_ref, bu_ref, wd_ref, bd_ref, y_ref):
    b = pl.program_id(0)

    @pl.when(b < nb_ref[0])
    def _():
        x = _unpack_bf16_pairs(_load_rows(x_ref, MOE_BLOCK)).astype(bf16)
        gt = jnp.minimum(_dot(x, wg_ref[0]) + bg_ref[0], SWIGLU_LIMIT)
        up = jnp.clip(_dot(x, wu_ref[0]) + bu_ref[0], -SWIGLU_LIMIT, SWIGLU_LIMIT)
        act = (up + 1.0) * gt * _sigmoid(SWIGLU_ALPHA * gt)
        y = _dot(act.astype(bf16), wd_ref[0]) + bd_ref[0]
        _store_rows(y_ref, _pack_bf16_pairs(y))

    @pl.when(b >= nb_ref[0])
    def _():
        y_ref[...] = jnp.zeros_like(y_ref)


def _experts(xb, block_expert, n_used, wg, bg, wu, bu, wd, bd):
    E, D, F = wg.shape
    nb = xb.shape[0] // (MOE_BLOCK * ROW_SLABS)
    wspec = lambda r, c: pl.BlockSpec((1, r, c), lambda b, be, nu: (be[b], 0, 0))
    xspec = pl.BlockSpec((MOE_BLOCK * ROW_SLABS, LANES), lambda b, be, nu: (b, 0))
    grid_spec = pltpu.PrefetchScalarGridSpec(
        num_scalar_prefetch=2,
        grid=(nb,),
        in_specs=[xspec, wspec(D, F), wspec(1, F), wspec(D, F), wspec(1, F), wspec(F, D), wspec(1, D)],
        out_specs=xspec,
    )
    return pl.pallas_call(
        _expert_kernel,
        grid_spec=grid_spec,
        out_shape=jax.ShapeDtypeStruct(xb.shape, u32),
        compiler_params=_cparams(("arbitrary",)),
        name="experts",
    )(block_expert, n_used, xb, wg, bg.reshape(E, 1, F), wu, bu.reshape(E, 1, F), wd, bd.reshape(E, 1, D))


def _combine_kernel(dest_ref, dnext_ref, yb_ref, meta_ref, x1_ref, gt_ref, o_ref, buf_ref, sem):
    n = dest_ref.shape[0]
    tt = n // TOP_K
    g = pl.program_id(0)
    slot = g % 2

    def gather(idx_ref, s):
        def issue(j, carry):
            _row_copy(yb_ref, idx_ref[j], buf_ref.at[s, j % TOP_K], j // TOP_K, sem.at[s]).start()
            return carry

        lax.fori_loop(0, n, issue, 0, unroll=DMA_UNROLL)

    @pl.when(g == 0)
    def _():
        gather(dest_ref, slot)

    @pl.when(g + 1 < pl.num_programs(0))
    def _():
        gather(dnext_ref, 1 - slot)

    for kk in range(TOP_K):
        pltpu.make_async_copy(yb_ref.at[pl.ds(0, tt * ROW_SLABS)], buf_ref.at[slot, kk], sem.at[slot]).wait()
    meta = meta_ref[...]
    moe = jnp.zeros((tt, x1_ref.shape[2]), f32)
    for kk in range(TOP_K):
        gate = meta[:, 2 * TOP_K + kk: 2 * TOP_K + kk + 1]
        moe = moe + gate * _unpack_bf16_pairs(_load_rows(buf_ref.at[slot, kk], tt))
    o_ref[0] = x1_ref[0] + gt_ref[0] * moe


def _combine(yb, dest, meta, x1, gt2):
    B, S, D = x1.shape
    tt = min(TT_COMBINE, S)
    nt = S // tt
    ng = B * nt
    return pl.pallas_call(
        _combine_kernel,
        grid=(ng,),
        in_specs=[pl.BlockSpec((tt * TOP_K,), lambda g: (g,), memory_space=pltpu.SMEM),
                  pl.BlockSpec((tt * TOP_K,), lambda g: (jnp.minimum(g + 1, ng - 1),), memory_space=pltpu.SMEM),
                  pl.BlockSpec(memory_space=pl.ANY),
                  pl.BlockSpec((tt, LANES), lambda g: (g, 0)),
                  pl.BlockSpec((1, tt, D), lambda g: (g // nt, g % nt, 0)),
                  pl.BlockSpec((1, 1, D), lambda g: (g // nt, 0, 0))],
        out_specs=pl.BlockSpec((1, tt, D), lambda g: (g // nt, g % nt, 0)),
        out_shape=jax.ShapeDtypeStruct((B, S, D), f32),
        scratch_shapes=[pltpu.VMEM((2, TOP_K, tt * ROW_SLABS, LANES), u32), pltpu.SemaphoreType.DMA((2,))],
        compiler_params=_cparams(("arbitrary",)),
        name="combine",
    )(dest, dest, yb, meta, x1, gt2)


def _moe(hp, meta, cnt, x1, gt2, ew):
    T = meta.shape[0]
    TK_ = T * TOP_K
    nb = -(-TK_ // MOE_BLOCK) + N_EXPERTS
    P = nb * MOE_BLOCK
    counts = cnt[0, :N_EXPERTS].astype(i32)
    padded = (counts + MOE_BLOCK - 1) // MOE_BLOCK * MOE_BLOCK
    cum_padded = jnp.cumsum(padded)
    pstart = cum_padded - padded
    top_idx = meta[:, :TOP_K].astype(i32)
    rank = meta[:, TOP_K:2 * TOP_K].astype(i32)
    dest = (pstart[top_idx] + rank).reshape(-1)
    block_expert = jnp.minimum(
        jnp.searchsorted(cum_padded, jnp.arange(nb, dtype=i32) * MOE_BLOCK, side="right"), N_EXPERTS - 1).astype(i32)
    n_used = (cum_padded[-1:] // MOE_BLOCK).astype(i32)
    xb = _dispatch(hp, dest, P)
    yb = _experts(xb, block_expert, n_used, *ew)
    return _combine(yb, dest, meta, x1, gt2)


def _layer(x, c, p, pw, ew, rope):
    B, S, D = x.shape
    mod = _ada(c, p["w_ada"], p["b_ada"]).reshape(B, 6, 1, D)
    sh1, sc1, gt1, sh2, sc2, gt2 = (mod[:, j] for j in range(6))
    q, k, v, dq, dk, dv, dkt, z, gcol, grow = _inproj(
        x, sh1, sc1, p["norm1"], p["q_a_norm"], p["kv_a_norm"], pw, rope)
    o_mla = _attention(q, k, v)
    o_f, o_b = _gdn(dq, dk, dv, dkt, gcol, grow)
    dnn = jnp.tile(p["dn_out_norm"], 1)
    x1, hp, meta, cnt = _outproj(x, o_mla, o_f, o_b, z, gt1, sc2, sh2, p["w_o"], dnn, p["norm2"],
                                 p["w_router"], p["b_router"])
    return _moe(hp, meta, cnt, x1, gt2, ew)


def kernel(x_prompt, x_sample, c_prompt, c_sample, w_ada, b_ada, norm1, w_in, q_a_norm, w_q_b, kv_a_norm, w_kv_b, q_norm, k_norm, dn_conv, dn_a_log, dn_dt_bias, dn_out_norm, w_o, norm2, w_router, b_router, w_gate, b_gate, w_up, b_up, w_down, b_down):
    y_prompt, y_sample = x_prompt, x_sample
    depth = w_ada.shape[0]
    for l in range(depth):
        p = {"w_ada": w_ada[l], "b_ada": b_ada[l], "norm1": norm1[l], "q_a_norm": q_a_norm[l],
             "kv_a_norm": kv_a_norm[l], "dn_out_norm": dn_out_norm[l], "w_o": w_o[l], "norm2": norm2[l],
             "w_router": w_router[l], "b_router": b_router[l]}
        pw = _prep_weights(w_in[l], w_q_b[l], w_kv_b[l], q_norm[l], k_norm[l], dn_conv[l], dn_a_log[l],
                           dn_dt_bias[l])
        ew = (w_gate[l].astype(bf16), b_gate[l], w_up[l].astype(bf16), b_up[l], w_down[l].astype(bf16), b_down[l])
        y_prompt = _layer(y_prompt, c_prompt, p, pw, ew, _rope_tables(y_prompt.shape[1]))
        y_sample = _layer(y_sample, c_sample, p, pw, ew, _rope_tables(y_sample.shape[1]))
    return (y_prompt, y_sample)
```

```python
import functools
import math

import jax
import jax.numpy as jnp
from jax import lax
from jax.experimental import pallas as pl
from jax.experimental.pallas import tpu as pltpu

f32 = jnp.float32
bf16 = jnp.bfloat16
u32 = jnp.uint32
i32 = jnp.int32

LANES = 128
VMEM_LIMIT = 56 * 1024 * 1024

MLA_HEADS = 8
MLA_Q_LORA = 384
MLA_KV_LORA = 256
MLA_NOPE = 64
MLA_ROPE = 32
MLA_QK = MLA_NOPE + MLA_ROPE
MLA_V = 64
ROPE_THETA = 10000.0
DN_HEADS = 4
DN_DK = 128
DN_DV = 128
DN_CONV = 5
DN_CHUNK = 64
DN_SUB = 16
N_EXPERTS = 32
TOP_K = 4
SWIGLU_LIMIT = 7.0
SWIGLU_ALPHA = 1.702
MOE_BLOCK = 256
EPS = 1e-6

TM_IN = 256
TM_OUT = 256
TQ = 256
TK = 512
GDN_ROWS = 128
GDN_SCAN_ROWS = 256
TT_DISPATCH = 512
TT_COMBINE = 256


def _cparams(sem):
    return pltpu.CompilerParams(dimension_semantics=sem, vmem_limit_bytes=VMEM_LIMIT)


def _split2(x):
    hi = x.astype(bf16)
    lo = (x - hi.astype(f32)).astype(bf16)
    return hi, lo


def _split3(x):
    hi = x.astype(bf16)
    r = x - hi.astype(f32)
    mid = r.astype(bf16)
    lo = (r - mid.astype(f32)).astype(bf16)
    return hi, mid, lo


def _dot(a, b):
    return jnp.dot(a, b, preferred_element_type=f32)


def _dot_nt(a, b):
    return lax.dot_general(a, b, (((1,), (1,)), ((), ())), preferred_element_type=f32)


def _dot3(a, b):
    ah, al = _split2(a)
    bh, bl = _split2(b)
    return _dot(ah, bh) + (_dot(al, bh) + _dot(ah, bl))


def _sigmoid(x):
    return 1.0 / (1.0 + jnp.exp(-x))


def _silu(x):
    return x * _sigmoid(x)


def _softplus(x):
    return jnp.maximum(x, 0.0) + jnp.log(1.0 + jnp.exp(-jnp.abs(x)))


def _ada_kernel(c_ref, w_ref, b_ref, o_ref):
    c = c_ref[...]
    o_ref[...] = _dot3(_silu(c), w_ref[...]) + b_ref[...]


def _ada(c, w_ada, b_ada):
    B, D = c.shape
    N = w_ada.shape[1]
    cp = jnp.zeros((8, D), f32).at[:B].set(c)
    tn = 1024
    out = pl.pallas_call(
        _ada_kernel,
        grid=(N // tn,),
        in_specs=[
            pl.BlockSpec((8, D), lambda j: (0, 0)),
            pl.BlockSpec((D, tn), lambda j: (0, j)),
            pl.BlockSpec((1, tn), lambda j: (0, j)),
        ],
        out_specs=pl.BlockSpec((8, tn), lambda j: (0, j)),
        out_shape=jax.ShapeDtypeStruct((8, N), f32),
        compiler_params=_cparams(("parallel",)),
        name="ada",
    )(cp, w_ada, b_ada.reshape(1, N))
    return out[:B]


def _chunk_tri(n, lower):
    r = lax.broadcasted_iota(i32, (n, n), 0)
    c = lax.broadcasted_iota(i32, (n, n), 1)
    same = (r // DN_CHUNK) == (c // DN_CHUNK)
    tri = (c <= r) if lower else (c >= r)
    return jnp.where(same & tri, 1.0, 0.0).astype(bf16)


def _inproj_kernel(
    x_ref, xp_ref, xn_ref, sh_ref, sc_ref, n1_ref,
    wqa_ref, wkva_ref, wkr_ref, wqkv_ref, wz_ref, wab_ref, wabt_ref,
    qan_ref, kvan_ref, wqb_ref, wkb_ref, wvb_ref, qn_ref, kn_ref,
    cos_ref, sina_ref, sinb_ref, conv_ref, gpar_ref, gpart_ref,
    q_ref, k_ref, v_ref, dq_ref, dk_ref, dv_ref, dkt_ref, z_ref, gcol_ref, grow_ref,
    ext_ref,
):
    i = pl.program_id(1)
    ni = pl.num_programs(1)
    tm = x_ref.shape[1]
    scale1 = 1.0 + sc_ref[0]
    shift1 = sh_ref[0]
    n1 = n1_ref[...]

    def modulate(xv):
        y = xv * lax.rsqrt(jnp.mean(xv * xv, axis=-1, keepdims=True) + EPS)
        return y * n1 * scale1 + shift1

    h = modulate(x_ref[0])
    hb = h.astype(bf16)
    hh = modulate(jnp.concatenate([xp_ref[0], xn_ref[0]], axis=0)).astype(bf16)

    qa = _dot(hb, wqa_ref[...])
    qa = qa * lax.rsqrt(jnp.mean(qa * qa, axis=-1, keepdims=True) + EPS) * qan_ref[...]
    kva = _dot(hb, wkva_ref[...])
    kva = kva * lax.rsqrt(jnp.mean(kva * kva, axis=-1, keepdims=True) + EPS) * kvan_ref[...]
    kr = _dot(hb, wkr_ref[...])
    qh = _dot(qa.astype(bf16), wqb_ref[...])
    kvb = kva.astype(bf16)
    kh = _dot(kvb, wkb_ref[...])
    vh = _dot(kvb, wvb_ref[...])
    cos = cos_ref[...]
    sina = sina_ref[...]
    sinb = sinb_ref[...]
    qg = qn_ref[...]
    kg = kn_ref[...]
    lane = lax.broadcasted_iota(i32, (tm, LANES), 1)
    q_scale = MLA_QK ** -0.5 * math.log2(math.e)

    def norm_rope(blk, gain):
        ss = jnp.sum(blk * blk, axis=-1, keepdims=True) * (1.0 / MLA_QK)
        y = blk * lax.rsqrt(ss + EPS) * gain
        return y * cos + pltpu.roll(y, LANES - MLA_ROPE // 2, 1) * sina + pltpu.roll(y, MLA_ROPE // 2, 1) * sinb

    for hd in range(MLA_HEADS):
        sl = slice(hd * LANES, (hd + 1) * LANES)
        q_ref[0, hd] = (norm_rope(qh[:, sl], qg) * q_scale).astype(bf16)
        k_ref[0, hd] = norm_rope(kh[:, sl] + kr, kg).astype(bf16)
        vblk = vh[:, hd * MLA_V:(hd + 1) * MLA_V]
        vpad = jnp.concatenate([vblk, jnp.zeros((tm, LANES - MLA_V), f32)], axis=1)
        v_ref[0, hd] = jnp.where(lane == MLA_V, 1.0, vpad).astype(bf16)

    ext_ref[pl.ds(8, tm), :] = _dot(hb, wqkv_ref[...])
    halo = _dot(hh, wqkv_ref[...])
    ext_ref[pl.ds(0, 8), :] = jnp.where(i == 0, 0.0, halo[:8])
    ext_ref[pl.ds(8 + tm, 8), :] = jnp.where(i == ni - 1, 0.0, halo[8:])
    pad = (DN_CONV - 1) // 2
    cw = conv_ref[...]
    acc = ext_ref[pl.ds(8 - pad, tm), :] * cw[0:1]
    for j in range(1, DN_CONV):
        acc = acc + ext_ref[pl.ds(8 - pad + j, tm), :] * cw[j:j + 1]
    act = _silu(acc)
    nqk = DN_HEADS * DN_DK
    for hd in range(DN_HEADS):
        sl = slice(hd * DN_DK, (hd + 1) * DN_DK)
        qb = act[:, sl]
        qb = qb * lax.rsqrt(jnp.sum(qb * qb, axis=-1, keepdims=True) + EPS) * (DN_DK ** -0.5)
        dq_ref[0, :, sl] = qb
        kb = act[:, nqk + hd * DN_DK: nqk + (hd + 1) * DN_DK]
        kb = kb * lax.rsqrt(jnp.sum(kb * kb, axis=-1, keepdims=True) + EPS)
        dk_ref[0, :, sl] = kb
        dkt_ref[0, sl, :] = kb.T
    dv_ref[0] = act[:, 2 * nqk:]
    z_ref[0] = _dot(hb, wz_ref[...])

    hlo = (h - hb.astype(f32)).astype(bf16)
    wab = wab_ref[...]
    p1 = _dot(hb, wab)
    p2 = _dot(hlo, wab)
    ab = p1 + pltpu.roll(p1, LANES - 16, 1) + p2
    gpar = gpar_ref[...]
    lane16 = lax.broadcasted_iota(i32, (tm, LANES), 1)
    gval = jnp.where(lane16 < 8, -gpar[0:1] * _softplus(ab + gpar[1:2]), _sigmoid(ab))
    g3 = _split3(gval)
    lo_tri = _chunk_tri(tm, True)
    up_tri = _chunk_tri(tm, False)
    pre = _dot(lo_tri, g3[0]) + (_dot(lo_tri, g3[1]) + _dot(lo_tri, g3[2]))
    suf = _dot(up_tri, g3[0]) + (_dot(up_tri, g3[1]) + _dot(up_tri, g3[2]))
    gcol_ref[0] = jnp.where(lane16 < 4, pre, jnp.where(lane16 < 8, suf, gval))

    wabt = wabt_ref[...]
    r1 = _dot_nt(wabt, hb)
    r2 = _dot_nt(wabt[:16], hlo)
    abt = r1[:16] + r1[16:] + r2
    gpt = gpart_ref[...]
    row16 = lax.broadcasted_iota(i32, (16, tm), 0)
    gvt = jnp.where(row16 < 8, -gpt[:, 0:1] * _softplus(abt + gpt[:, 1:2]), _sigmoid(abt))
    t3 = _split3(gvt)
    pre_t = _dot(t3[0], up_tri) + (_dot(t3[1], up_tri) + _dot(t3[2], up_tri))
    suf_t = _dot(t3[0], lo_tri) + (_dot(t3[1], lo_tri) + _dot(t3[2], lo_tri))
    grow_ref[0] = jnp.where(row16 < 4, pre_t, jnp.where(row16 < 8, suf_t, gvt))


def _prep_weights(w_in, w_q_b, w_kv_b, q_norm, k_norm, dn_conv, dn_a_log, dn_dt_bias):
    D = w_in.shape[0]
    i0 = MLA_Q_LORA
    i1 = i0 + MLA_KV_LORA
    i2 = i1 + MLA_ROPE
    nqkv = DN_HEADS * (2 * DN_DK + DN_DV)
    i3 = i2 + nqkv
    i4 = i3 + DN_HEADS * DN_DV
    wqa = w_in[:, :i0].astype(bf16)
    wkva = w_in[:, i0:i1].astype(bf16)
    wkr = jnp.zeros((D, LANES), f32).at[:, MLA_NOPE:MLA_QK].set(w_in[:, i1:i2]).astype(bf16)
    wqkv = w_in[:, i2:i3].astype(bf16)
    wz = w_in[:, i3:i4].astype(bf16)
    wab_f = w_in[:, i4:]
    wab_hi = wab_f.astype(bf16)
    wab_lo = (wab_f - wab_hi.astype(f32)).astype(bf16)
    wab = jnp.zeros((D, LANES), bf16).at[:, :16].set(wab_hi).at[:, 16:32].set(wab_lo)
    wabt = jnp.concatenate([wab_hi.T, wab_lo.T], axis=0)
    wqb = w_q_b.reshape(MLA_Q_LORA, MLA_HEADS, MLA_QK)
    wqb = jnp.pad(wqb, ((0, 0), (0, 0), (0, LANES - MLA_QK))).reshape(MLA_Q_LORA, MLA_HEADS * LANES).astype(bf16)
    wkv = w_kv_b.reshape(MLA_KV_LORA, MLA_HEADS, MLA_NOPE + MLA_V)
    wkb = jnp.pad(wkv[:, :, :MLA_NOPE], ((0, 0), (0, 0), (0, LANES - MLA_NOPE)))
    wkb = wkb.reshape(MLA_KV_LORA, MLA_HEADS * LANES).astype(bf16)
    wvb = wkv[:, :, MLA_NOPE:].reshape(MLA_KV_LORA, MLA_HEADS * MLA_V).astype(bf16)
    qn = jnp.pad(q_norm, (0, LANES - MLA_QK)).reshape(1, LANES)
    kn = jnp.pad(k_norm, (0, LANES - MLA_QK)).reshape(1, LANES)
    conv = jnp.pad(dn_conv, ((0, 8 - DN_CONV), (0, 0)))
    ea = jnp.exp(dn_a_log.astype(f32)).reshape(-1)
    dtb = dn_dt_bias.astype(f32).reshape(-1)
    gpar = jnp.zeros((8, LANES), f32).at[0, :8].set(ea).at[1, :8].set(dtb)
    gpart = jnp.zeros((16, LANES), f32).at[:8, 0].set(ea).at[:8, 1].set(dtb)
    return dict(wqa=wqa, wkva=wkva, wkr=wkr, wqkv=wqkv, wz=wz, wab=wab, wabt=wabt, wqb=wqb, wkb=wkb,
                wvb=wvb, qn=qn, kn=kn, conv=conv, gpar=gpar, gpart=gpart)


def _rope_tables(S):
    half = MLA_ROPE // 2
    freq = ROPE_THETA ** (-jnp.arange(half, dtype=f32) / half)
    ang = jnp.arange(S, dtype=f32)[:, None] * freq[None, :]
    cos, sin = jnp.cos(ang), jnp.sin(ang)
    z = jnp.zeros((S, LANES), f32)
    cos_t = z.at[:, :MLA_NOPE].set(1.0).at[:, MLA_NOPE:MLA_NOPE + half].set(cos).at[:, MLA_NOPE + half:MLA_QK].set(cos)
    sina = z.at[:, MLA_NOPE:MLA_NOPE + half].set(-sin)
    sinb = z.at[:, MLA_NOPE + half:MLA_QK].set(sin)
    return cos_t, sina, sinb


def _inproj(x, sh1, sc1, norm1, q_a_norm, kv_a_norm, pw, rope):
    B, S, D = x.shape
    tm = min(TM_IN, S)
    nt = S // tm
    r8 = tm // 8
    nqkv = DN_HEADS * (2 * DN_DK + DN_DV)
    nh = DN_HEADS * DN_DK

    def full(a):
        return pl.BlockSpec(a.shape, lambda b, i: (0,) * a.ndim)

    tok = lambda w: pl.BlockSpec((1, tm, w), lambda b, i: (b, i, 0))
    in_specs = [
        tok(D),
        pl.BlockSpec((1, 8, D), lambda b, i: (b, jnp.maximum(i * r8 - 1, 0), 0)),
        pl.BlockSpec((1, 8, D), lambda b, i: (b, jnp.minimum((i + 1) * r8, S // 8 - 1), 0)),
        pl.BlockSpec((1, 1, D), lambda b, i: (b, 0, 0)),
        pl.BlockSpec((1, 1, D), lambda b, i: (b, 0, 0)),
    ]
    consts = [norm1.reshape(1, D), pw["wqa"], pw["wkva"], pw["wkr"], pw["wqkv"], pw["wz"], pw["wab"], pw["wabt"],
              q_a_norm.reshape(1, -1), kv_a_norm.reshape(1, -1), pw["wqb"], pw["wkb"], pw["wvb"], pw["qn"], pw["kn"]]
    in_specs += [full(a) for a in consts]
    in_specs += [pl.BlockSpec((tm, LANES), lambda b, i: (i, 0))] * 3
    tail = [pw["conv"], pw["gpar"], pw["gpart"]]
    in_specs += [full(a) for a in tail]
    hspec = pl.BlockSpec((1, MLA_HEADS, tm, LANES), lambda b, i: (b, 0, i, 0))
    out_specs = [hspec, hspec, hspec, tok(nh), tok(nh), tok(nh),
                 pl.BlockSpec((1, nh, tm), lambda b, i: (b, 0, i)), tok(nh), tok(LANES),
                 pl.BlockSpec((1, 16, tm), lambda b, i: (b, 0, i))]
    hshape = jax.ShapeDtypeStruct((B, MLA_HEADS, S, LANES), bf16)
    tshape = jax.ShapeDtypeStruct((B, S, nh), f32)
    out_shape = [hshape, hshape, hshape, tshape, tshape, tshape,
                 jax.ShapeDtypeStruct((B, nh, S), f32), tshape,
                 jax.ShapeDtypeStruct((B, S, LANES), f32), jax.ShapeDtypeStruct((B, 16, S), f32)]
    return pl.pallas_call(
        _inproj_kernel,
        grid=(B, nt),
        in_specs=in_specs,
        out_specs=out_specs,
        out_shape=out_shape,
        scratch_shapes=[pltpu.VMEM((tm + 16, nqkv), f32)],
        compiler_params=_cparams(("parallel", "parallel")),
        name="inproj",
    )(x, x, x, sh1, sc1, *consts, *rope, *tail)


def _attn_kernel(q_ref, k_ref, v_ref, o_ref, s_scr, m_scr, acc_scr):
    S = k_ref.shape[2]
    tk = s_scr.shape[3]
    nk = S // tk
    nh = q_ref.shape[1]
    qs = [q_ref[0, hd] for hd in range(nh)]
    m_scr[...] = jnp.full(m_scr.shape, -jnp.inf, f32)
    acc_scr[...] = jnp.zeros(acc_scr.shape, f32)

    def scores(slot, j):
        off = pl.multiple_of(j * tk, tk)
        for hd in range(nh):
            s_scr[slot, hd] = _dot_nt(qs[hd], k_ref[0, hd, pl.ds(off, tk), :])

    def accumulate(slot, j):
        off = pl.multiple_of(j * tk, tk)
        for hd in range(nh):
            s = s_scr[slot, hd]
            m = m_scr[hd]
            m_new = jnp.maximum(m, jnp.broadcast_to(jnp.max(s, axis=-1, keepdims=True), m.shape))
            p = jnp.exp2(s - jnp.tile(m_new, (1, tk // LANES)))
            m_scr[hd] = m_new
            acc_scr[hd] = acc_scr[hd] * jnp.exp2(m - m_new) + _dot(p.astype(bf16), v_ref[0, hd, pl.ds(off, tk), :])

    scores(0, 0)

    def body(jj, carry):
        j = 2 * jj
        scores(1, j + 1)
        accumulate(0, j)
        scores(0, jnp.minimum(j + 2, nk - 1))
        accumulate(1, j + 1)
        return carry

    lax.fori_loop(0, nk // 2, body, 0)
    outs = [acc_scr[hd][:, :MLA_V] / acc_scr[hd][:, MLA_V:MLA_V + 1] for hd in range(nh)]
    o_ref[0] = jnp.concatenate(outs, axis=1).astype(o_ref.dtype)


def _attention(q, k, v):
    B, H, S, _ = q.shape
    tq = min(TQ, S)
    tk = min(TK, S // 2)
    nh = 2
    qspec = pl.BlockSpec((1, nh, tq, LANES), lambda b, h, i: (b, h, i, 0))
    kspec = pl.BlockSpec((1, nh, S, LANES), lambda b, h, i: (b, h, 0, 0))
    return pl.pallas_call(
        _attn_kernel,
        grid=(B, H // nh, S // tq),
        in_specs=[qspec, kspec, kspec],
        out_specs=pl.BlockSpec((1, tq, LANES), lambda b, h, i: (b, i, h)),
        out_shape=jax.ShapeDtypeStruct((B, S, H * MLA_V), bf16),
        scratch_shapes=[pltpu.VMEM((2, nh, tq, tk), f32), pltpu.VMEM((nh, tq, LANES), f32),
                        pltpu.VMEM((nh, tq, LANES), f32)],
        compiler_params=_cparams(("parallel", "parallel", "arbitrary")),
        name="attn",
    )(q, k, v)


def _unit_tri_inverses(Ls, sub_mask):
    C = Ls[0].shape[0]
    r = lax.broadcasted_iota(i32, (C, C), 0)
    c = lax.broadcasted_iota(i32, (C, C), 1)
    eye = jnp.where(r == c, 1.0, 0.0)
    Lds = [jnp.where(sub_mask, L, 0.0) for L in Ls]
    Los = [L - Ld for L, Ld in zip(Ls, Lds)]
    Ts = [eye - Ld for Ld in Lds]
    Ps = Lds
    n = 2
    while n < DN_SUB:
        Ps = [_dot3k(P, P) for P in Ps]
        Ts = [_dot3k(T, eye + P) for T, P in zip(Ts, Ps)]
        n *= 2
    Ns = [_dot3k(T, Lo) for T, Lo in zip(Ts, Los)]
    N2s = [_dot3k(N, N) for N in Ns]
    Ms = [_dot3k(eye - N, eye + N2) for N, N2 in zip(Ns, N2s)]
    return [_dot3k(M, T) for M, T in zip(Ms, Ts)]


def _dot3k(a, b):
    ah = a.astype(bf16).astype(f32)
    lhs = jnp.concatenate([a, a - ah, a], axis=1).astype(bf16)
    bh = b.astype(bf16)
    bl = (b - bh.astype(f32)).astype(bf16)
    rhs = jnp.concatenate([bh, bh, bl], axis=0)
    return _dot(lhs, rhs)


def _gdn_prep_kernel(q_ref, k_ref, v_ref, kt_ref, gcol_ref, grow_ref,
                     uf_ref, wf_ref, qf_ref, af_ref, ktf_ref, ub_ref, wb_ref, qb_ref, ab_ref, ktb_ref, dl_ref):
    C = DN_CHUNK
    nch = q_ref.shape[1] // C
    r = lax.broadcasted_iota(i32, (C, C), 0)
    c = lax.broadcasted_iota(i32, (C, C), 1)
    sub_mask = (r // DN_SUB) == (c // DN_SUB)
    outs = ((uf_ref, wf_ref, qf_ref, af_ref, ktf_ref), (ub_ref, wb_ref, qb_ref, ab_ref, ktb_ref))
    chains = [(ch, hd) for ch in range(nch) for hd in range(DN_HEADS)]
    kks = {}
    for ch, hd in chains:
        rs = slice(ch * C, (ch + 1) * C)
        hs = slice(hd * DN_DK, (hd + 1) * DN_DK)
        kbf = k_ref[0, rs, hs].astype(bf16)
        kks[ch, hd] = _dot_nt(kbf, kbf)
        qk = _dot_nt(q_ref[0, rs, hs].astype(bf16), kbf)
        kt = kt_ref[0, hs, rs]
        for d, reverse in enumerate((False, True)):
            _, _, _, a_ref, kdt_ref = outs[d]
            incl = (c >= r) if reverse else (c <= r)
            gc = gcol_ref[0, rs, 4 * d + hd: 4 * d + hd + 1]
            gr = grow_ref[0, 4 * d + hd: 4 * d + hd + 1, rs]
            dec = jnp.exp(jnp.where(incl, gc - gr, -jnp.inf))
            a_ref[0, hd, rs, :] = (qk * dec).astype(bf16)
            g_last = gr[:, 0:1] if reverse else gr[:, C - 1:C]
            kdt_ref[0, ch, hs, :] = (kt * jnp.exp(g_last - gr)).astype(bf16)
    for ch in range(nch):
        rs = slice(ch * C, (ch + 1) * C)
        dls = []
        for d, reverse in enumerate((False, True)):
            for hd in range(DN_HEADS):
                gr = grow_ref[0, 4 * d + hd: 4 * d + hd + 1, rs]
                g_last = gr[:, 0:1] if reverse else gr[:, C - 1:C]
                dls.append(jnp.broadcast_to(jnp.exp(g_last), (1, LANES)))
        dl_ref[0, ch] = jnp.concatenate(dls, axis=0)

    full = [(ch, hd, d) for ch, hd in chains for d in range(2)]
    Ls = []
    for ch, hd, d in full:
        rs = slice(ch * C, (ch + 1) * C)
        strict = (c > r) if d else (c < r)
        gc = gcol_ref[0, rs, 4 * d + hd: 4 * d + hd + 1]
        beta = gcol_ref[0, rs, 8 + 4 * d + hd: 8 + 4 * d + hd + 1]
        gr = grow_ref[0, 4 * d + hd: 4 * d + hd + 1, rs]
        dec = jnp.exp(jnp.where(strict, gc - gr, -jnp.inf))
        Ls.append(beta * kks[ch, hd] * dec)
    Ts = _unit_tri_inverses(Ls, sub_mask)
    sols = []
    for (ch, hd, d), T in zip(full, Ts):
        rs = slice(ch * C, (ch + 1) * C)
        hs = slice(hd * DN_DK, (hd + 1) * DN_DK)
        gc = gcol_ref[0, rs, 4 * d + hd: 4 * d + hd + 1]
        beta = gcol_ref[0, rs, 8 + 4 * d + hd: 8 + 4 * d + hd + 1]
        eg = jnp.exp(gc)
        k = k_ref[0, rs, hs]
        outs[d][2][0, rs, hs] = (q_ref[0, rs, hs] * eg).astype(bf16)
        sols.append(_dot3k(T, jnp.concatenate([v_ref[0, rs, hs] * beta, k * (beta * eg)], axis=1)))
    for (ch, hd, d), sol in zip(full, sols):
        rs = slice(ch * C, (ch + 1) * C)
        hs = slice(hd * DN_DK, (hd + 1) * DN_DK)
        outs[d][0][0, rs, hs] = sol[:, :DN_DV]
        outs[d][1][0, rs, hs] = sol[:, DN_DV:].astype(bf16)


def _gdn_scan_kernel(uf_ref, wf_ref, qf_ref, af_ref, ktf_ref, dlf_ref,
                     ub_ref, wb_ref, qb_ref, ab_ref, ktb_ref, dlb_ref, of_ref, ob_ref, s_ref):
    @pl.when(pl.program_id(1) == 0)
    def _():
        s_ref[...] = jnp.zeros_like(s_ref)

    C = DN_CHUNK
    nch = uf_ref.shape[1] // C
    dirs = ((uf_ref, wf_ref, qf_ref, af_ref, ktf_ref, dlf_ref, of_ref),
            (ub_ref, wb_ref, qb_ref, ab_ref, ktb_ref, dlb_ref, ob_ref))
    chains = [(d, hd) for d in range(2) for hd in range(DN_HEADS)]
    sts = [s_ref[d, hd] for d, hd in chains]
    for step in range(nch):
        def sl(d, hd):
            ch = step if d == 0 else nch - 1 - step
            return ch, slice(ch * C, (ch + 1) * C), slice(hd * DN_DK, (hd + 1) * DN_DK)

        wss = []
        for (d, hd), st in zip(chains, sts):
            ch, rs, hs = sl(d, hd)
            wq = jnp.concatenate([dirs[d][1][0, rs, hs], dirs[d][2][0, rs, hs]], axis=0)
            wss.append(_dot(wq, st.astype(bf16)))
        vnbs = []
        for (d, hd), ws in zip(chains, wss):
            ch, rs, hs = sl(d, hd)
            vnbs.append((dirs[d][0][0, rs, hs] - ws[:C]).astype(bf16))
        new = []
        for (d, hd), st, ws, vnb in zip(chains, sts, wss, vnbs):
            ch, rs, hs = sl(d, hd)
            dirs[d][6][0, rs, hs] = ws[C:] + _dot(dirs[d][3][0, hd, rs, :], vnb)
            dl = dirs[d][5][0, ch, d * DN_HEADS + hd: d * DN_HEADS + hd + 1, :]
            new.append(st * dl + _dot(dirs[d][4][0, ch, hs, :], vnb))
        sts = new
    for (d, hd), st in zip(chains, sts):
        s_ref[d, hd] = st


def _gdn(dq, dk, dv, dkt, gcol, grow):
    B, S, W = dq.shape
    C = DN_CHUNK
    rows = min(GDN_ROWS, S)
    n = S // rows
    nch = rows // C
    tok = pl.BlockSpec((1, rows, W), lambda b, i: (b, i, 0))
    aspec = pl.BlockSpec((1, DN_HEADS, rows, C), lambda b, i: (b, 0, i, 0))
    kspec = pl.BlockSpec((1, nch, W, C), lambda b, i: (b, i, 0, 0))
    dspec = pl.BlockSpec((1, nch, 2 * DN_HEADS, LANES), lambda b, i: (b, i, 0, 0))
    t32 = jax.ShapeDtypeStruct((B, S, W), f32)
    t16 = jax.ShapeDtypeStruct((B, S, W), bf16)
    ashape = jax.ShapeDtypeStruct((B, DN_HEADS, S, C), bf16)
    kshape = jax.ShapeDtypeStruct((B, S // C, W, C), bf16)
    per_dir_specs = [tok, tok, tok, aspec, kspec]
    per_dir_shapes = [t32, t16, t16, ashape, kshape]
    prep = pl.pallas_call(
        _gdn_prep_kernel,
        grid=(B, n),
        in_specs=[tok, tok, tok, pl.BlockSpec((1, W, rows), lambda b, i: (b, 0, i)),
                  pl.BlockSpec((1, rows, LANES), lambda b, i: (b, i, 0)),
                  pl.BlockSpec((1, 16, rows), lambda b, i: (b, 0, i))],
        out_specs=per_dir_specs * 2 + [dspec],
        out_shape=per_dir_shapes * 2 + [jax.ShapeDtypeStruct((B, S // C, 2 * DN_HEADS, LANES), f32)],
        compiler_params=_cparams(("parallel", "parallel")),
        name="gdn_prep",
    )(dq, dk, dv, dkt, gcol, grow)
    fwd, bwd, dl = prep[:5], prep[5:10], prep[10]

    rows_s = min(GDN_SCAN_ROWS, S)
    ns = S // rows_s
    nchs = rows_s // C

    def specs(rev):
        blk = (lambda i: ns - 1 - i) if rev else (lambda i: i)
        return [pl.BlockSpec((1, rows_s, W), lambda b, i: (b, blk(i), 0))] * 3 + [
            pl.BlockSpec((1, DN_HEADS, rows_s, C), lambda b, i: (b, 0, blk(i), 0)),
            pl.BlockSpec((1, nchs, W, C), lambda b, i: (b, blk(i), 0, 0)),
            pl.BlockSpec((1, nchs, 2 * DN_HEADS, LANES), lambda b, i: (b, blk(i), 0, 0))]

    return pl.pallas_call(
        _gdn_scan_kernel,
        grid=(B, ns),
        in_specs=specs(False) + specs(True),
        out_specs=[specs(False)[0], specs(True)[0]],
        out_shape=[t32, t32],
        scratch_shapes=[pltpu.VMEM((2, DN_HEADS, DN_DK, DN_DV), f32)],
        compiler_params=_cparams(("parallel", "arbitrary")),
        name="gdn_scan",
    )(*fwd, dl, *bwd, dl)


def _pack_bf16_pairs(y):
    w = y.shape[1] // 2
    lo = pltpu.bitcast(y[:, :w].astype(bf16).astype(f32), u32)
    hi = pltpu.bitcast(y[:, w:].astype(bf16).astype(f32), u32)
    return (lo >> 16) | (hi & jnp.uint32(0xFFFF0000))


def _unpack_bf16_pairs(p):
    lo = pltpu.bitcast(p << 16, f32)
    hi = pltpu.bitcast(p & jnp.uint32(0xFFFF0000), f32)
    return jnp.concatenate([lo, hi], axis=1)


ROW_SLABS = 4


def _store_rows(ref, packed):
    n = packed.shape[0]
    for j in range(ROW_SLABS):
        ref[pl.ds(j, n, stride=ROW_SLABS), :] = packed[:, j * LANES:(j + 1) * LANES]


def _load_rows(ref, n):
    return jnp.concatenate([ref[pl.ds(j, n, stride=ROW_SLABS), :] for j in range(ROW_SLABS)], axis=1)


def _outproj_kernel(x_ref, om_ref, of_ref, ob_ref, z_ref, gt_ref, sc_ref, sh_ref,
                    wo_ref, dnn_ref, n2_ref, wr_ref, br_ref,
                    x1_ref, hp_ref, meta_ref, cnt_ref, carry_ref):
    first = (pl.program_id(0) == 0) & (pl.program_id(1) == 0)

    @pl.when(first)
    def _():
        carry_ref[...] = jnp.zeros_like(carry_ref)

    tm = x_ref.shape[1]
    o = of_ref[0] + ob_ref[0]
    z = z_ref[0]
    dnn = dnn_ref[...]
    parts = []
    for hd in range(DN_HEADS):
        sl = slice(hd * DN_DV, (hd + 1) * DN_DV)
        ob = o[:, sl]
        ob = ob * lax.rsqrt(jnp.mean(ob * ob, axis=-1, keepdims=True) + EPS) * dnn
        parts.append(ob * _silu(z[:, sl]))
    odn = jnp.concatenate(parts, axis=1).astype(bf16)
    nm = om_ref.shape[2]
    mixed = _dot(om_ref[0], wo_ref[pl.ds(0, nm), :]) + _dot(odn, wo_ref[pl.ds(nm, odn.shape[1]), :])
    x1 = x_ref[0] + gt_ref[0] * mixed
    x1_ref[0] = x1
    h2 = x1 * lax.rsqrt(jnp.mean(x1 * x1, axis=-1, keepdims=True) + EPS) * n2_ref[...]
    h2 = h2 * (1.0 + sc_ref[0]) + sh_ref[0]
    _store_rows(hp_ref, _pack_bf16_pairs(h2))

    hh, hl = _split2(h2)
    wr = wr_ref[...]
    p1 = _dot(hh, wr)
    p2 = _dot(hl, wr[:, :LANES])
    lane = lax.broadcasted_iota(i32, (tm, LANES), 1)
    logits = p1[:, :LANES] + p1[:, LANES:] + p2 + br_ref[...]
    logits = jnp.where(lane < N_EXPERTS, logits, -jnp.inf)
    vals, idxs = [], []
    work = logits
    for _ in range(TOP_K):
        mx = jnp.max(work, axis=-1, keepdims=True)
        ix = jnp.min(jnp.where(work == mx, lane, LANES), axis=-1, keepdims=True)
        vals.append(mx)
        idxs.append(ix)
        work = jnp.where(lane == ix, -jnp.inf, work)
    es = [jnp.exp(vv - vals[0]) for vv in vals]
    den = es[0] + es[1] + es[2] + es[3]
    multihot = jnp.where(work != logits, 1.0, 0.0)
    rr = lax.broadcasted_iota(i32, (tm, tm), 0)
    cc = lax.broadcasted_iota(i32, (tm, tm), 1)
    below = jnp.where(cc < rr, 1.0, 0.0).astype(bf16)
    prefix = _dot(below, multihot.astype(bf16)) + carry_ref[0:1]
    meta = jnp.zeros((tm, LANES), f32)
    for kk in range(TOP_K):
        rank = jnp.sum(jnp.where(lane == idxs[kk], prefix, 0.0), axis=-1, keepdims=True)
        meta = jnp.where(lane == kk, idxs[kk].astype(f32), meta)
        meta = jnp.where(lane == TOP_K + kk, rank, meta)
        meta = jnp.where(lane == 2 * TOP_K + kk, es[kk] / den, meta)
    meta_ref[...] = meta
    carry = carry_ref[...] + jnp.sum(multihot, axis=0, keepdims=True)
    carry_ref[...] = carry
    cnt_ref[...] = carry


def _outproj(x, o_mla, o_f, o_b, z, gt1, sc2, sh2, w_o, dn_out_norm, norm2, w_router, b_router):
    B, S, D = x.shape
    tm = min(TM_OUT, S)
    nt = S // tm
    T = B * S
    nh = o_f.shape[2]
    wr_hi = w_router.astype(bf16)
    wr_lo = (w_router - wr_hi.astype(f32)).astype(bf16)
    wr = jnp.zeros((D, 2 * LANES), bf16).at[:, :N_EXPERTS].set(wr_hi).at[:, LANES:LANES + N_EXPERTS].set(wr_lo)
    br = jnp.zeros((1, LANES), f32).at[0, :N_EXPERTS].set(b_router)
    consts = [w_o.astype(bf16), dn_out_norm.reshape(1, -1), norm2.reshape(1, D), wr, br]

    def full(a):
        return pl.BlockSpec(a.shape, lambda b, i: (0,) * a.ndim)

    tok = lambda w: pl.BlockSpec((1, tm, w), lambda b, i: (b, i, 0))
    vec = pl.BlockSpec((1, 1, D), lambda b, i: (b, 0, 0))
    flat = lambda w: pl.BlockSpec((tm, w), lambda b, i: (b * nt + i, 0))
    return pl.pallas_call(
        _outproj_kernel,
        grid=(B, nt),
        in_specs=[tok(D), tok(o_mla.shape[2]), tok(nh), tok(nh), tok(nh), vec, vec, vec] + [full(a) for a in consts],
        out_specs=[tok(D), pl.BlockSpec((tm * ROW_SLABS, LANES), lambda b, i: (b * nt + i, 0)), flat(LANES),
                   pl.BlockSpec((8, LANES), lambda b, i: (0, 0))],
        out_shape=[jax.ShapeDtypeStruct((B, S, D), f32), jax.ShapeDtypeStruct((T * ROW_SLABS, LANES), u32),
                   jax.ShapeDtypeStruct((T, LANES), f32), jax.ShapeDtypeStruct((8, LANES), f32)],
        scratch_shapes=[pltpu.VMEM((8, LANES), f32)],
        compiler_params=_cparams(("arbitrary", "arbitrary")),
        name="outproj",
    )(x, o_mla, o_f, o_b, z, gt1, sc2, sh2, *consts)


DMA_UNROLL = 2


def _row_copy(src_ref, s, dst_ref, d, sem):
    return pltpu.make_async_copy(src_ref.at[pl.ds(pl.multiple_of(s * ROW_SLABS, ROW_SLABS), ROW_SLABS)],
                                 dst_ref.at[pl.ds(pl.multiple_of(d * ROW_SLABS, ROW_SLABS), ROW_SLABS)], sem)


def _dispatch_kernel(dest_ref, hp_ref, xb_in_ref, xb_ref, sem):
    del xb_in_ref
    n = dest_ref.shape[0]

    def issue(t, carry):
        for kk in range(TOP_K):
            _row_copy(hp_ref, t, xb_ref, dest_ref[t * TOP_K + kk], sem).start(priority=kk % 2)
        return carry

    lax.fori_loop(0, n // TOP_K, issue, 0, unroll=DMA_UNROLL)
    for _ in range(TOP_K):
        pltpu.make_async_copy(hp_ref, xb_ref.at[pl.ds(0, hp_ref.shape[0])], sem).wait()


def _dispatch(hp, dest, P):
    T = hp.shape[0] // ROW_SLABS
    tt = min(TT_DISPATCH, T)
    xb0 = jnp.zeros((P * ROW_SLABS, LANES), u32)
    return pl.pallas_call(
        _dispatch_kernel,
        grid=(T // tt,),
        in_specs=[pl.BlockSpec((tt * TOP_K,), lambda i: (i,), memory_space=pltpu.SMEM),
                  pl.BlockSpec((tt * ROW_SLABS, LANES), lambda i: (i, 0)), pl.BlockSpec(memory_space=pl.ANY)],
        out_specs=pl.BlockSpec(memory_space=pl.ANY),
        out_shape=jax.ShapeDtypeStruct((P * ROW_SLABS, LANES), u32),
        scratch_shapes=[pltpu.SemaphoreType.DMA(())],
        input_output_aliases={2: 0},
        compiler_params=pltpu.CompilerParams(dimension_semantics=("arbitrary",), has_side_effects=True),
        name="dispatch",
    )(dest, hp, xb0)


def _expert_kernel(be_ref, nb_ref, x_ref, wg_ref, bg_ref, wu_ref, bu_ref, wd_ref, bd_ref, y_ref):
    b = pl.program_id(0)

    @pl.when(b < nb_ref[0])
    def _():
        x = _unpack_bf16_pairs(_load_rows(x_ref, MOE_BLOCK)).astype(bf16)
        gt = jnp.minimum(_dot(x, wg_ref[0]) + bg_ref[0], SWIGLU_LIMIT)
        up = jnp.clip(_dot(x, wu_ref[0]) + bu_ref[0], -SWIGLU_LIMIT, SWIGLU_LIMIT)
        act = (up + 1.0) * gt * _sigmoid(SWIGLU_ALPHA * gt)
        y = _dot(act.astype(bf16), wd_ref[0]) + bd_ref[0]
        _store_rows(y_ref, _pack_bf16_pairs(y))

    @pl.when(b >= nb_ref[0])
    def _():
        y_ref[...] = jnp.zeros_like(y_ref)


def _experts(xb, block_expert, n_used, wg, bg, wu, bu, wd, bd):
    E, D, F = wg.shape
    nb = xb.shape[0] // (MOE_BLOCK * ROW_SLABS)
    wspec = lambda r, c: pl.BlockSpec((1, r, c), lambda b, be, nu: (be[b], 0, 0))
    xspec = pl.BlockSpec((MOE_BLOCK * ROW_SLABS, LANES), lambda b, be, nu: (b, 0))
    grid_spec = pltpu.PrefetchScalarGridSpec(
        num_scalar_prefetch=2,
        grid=(nb,),
        in_specs=[xspec, wspec(D, F), wspec(1, F), wspec(D, F), wspec(1, F), wspec(F, D), wspec(1, D)],
        out_specs=xspec,
    )
    return pl.pallas_call(
        _expert_kernel,
        grid_spec=grid_spec,
        out_shape=jax.ShapeDtypeStruct(xb.shape, u32),
        compiler_params=_cparams(("arbitrary",)),
        name="experts",
    )(block_expert, n_used, xb, wg, bg.reshape(E, 1, F), wu, bu.reshape(E, 1, F), wd, bd.reshape(E, 1, D))


def _combine_kernel(dest_ref, dnext_ref, yb_ref, meta_ref, x1_ref, gt_ref, o_ref, buf_ref, sem):
    n = dest_ref.shape[0]
    tt = n // TOP_K
    g = pl.program_id(0)
    slot = g % 2

    def gather(idx_ref, s):
        def issue(t, carry):
            for kk in range(TOP_K):
                _row_copy(yb_ref, idx_ref[t * TOP_K + kk], buf_ref.at[s, kk], t, sem.at[s]).start(priority=kk % 2)
            return carry

        lax.fori_loop(0, tt, issue, 0, unroll=DMA_UNROLL)

    @pl.when(g == 0)
    def _():
        gather(dest_ref, slot)

    @pl.when(g + 1 < pl.num_programs(0))
    def _():
        gather(dnext_ref, 1 - slot)

    for kk in range(TOP_K):
        pltpu.make_async_copy(yb_ref.at[pl.ds(0, tt * ROW_SLABS)], buf_ref.at[slot, kk], sem.at[slot]).wait()
    meta = meta_ref[...]
    moe = jnp.zeros((tt, x1_ref.shape[2]), f32)
    for kk in range(TOP_K):
        gate = meta[:, 2 * TOP_K + kk: 2 * TOP_K + kk + 1]
        moe = moe + gate * _unpack_bf16_pairs(_load_rows(buf_ref.at[slot, kk], tt))
    o_ref[0] = x1_ref[0] + gt_ref[0] * moe


def _combine(yb, dest, meta, x1, gt2):
    B, S, D = x1.shape
    tt = min(TT_COMBINE, S)
    nt = S // tt
    ng = B * nt
    return pl.pallas_call(
        _combine_kernel,
        grid=(ng,),
        in_specs=[pl.BlockSpec((tt * TOP_K,), lambda g: (g,), memory_space=pltpu.SMEM),
                  pl.BlockSpec((tt * TOP_K,), lambda g: (jnp.minimum(g + 1, ng - 1),), memory_space=pltpu.SMEM),
                  pl.BlockSpec(memory_space=pl.ANY),
                  pl.BlockSpec((tt, LANES), lambda g: (g, 0)),
                  pl.BlockSpec((1, tt, D), lambda g: (g // nt, g % nt, 0)),
                  pl.BlockSpec((1, 1, D), lambda g: (g // nt, 0, 0))],
        out_specs=pl.BlockSpec((1, tt, D), lambda g: (g // nt, g % nt, 0)),
        out_shape=jax.ShapeDtypeStruct((B, S, D), f32),
        scratch_shapes=[pltpu.VMEM((2, TOP_K, tt * ROW_SLABS, LANES), u32), pltpu.SemaphoreType.DMA((2,))],
        compiler_params=_cparams(("arbitrary",)),
        name="combine",
    )(dest, dest, yb, meta, x1, gt2)


def _moe(hp, meta, cnt, x1, gt2, ew):
    T = meta.shape[0]
    TK_ = T * TOP_K
    nb = -(-TK_ // MOE_BLOCK) + N_EXPERTS
    P = nb * MOE_BLOCK
    counts = cnt[0, :N_EXPERTS].astype(i32)
    padded = (counts + MOE_BLOCK - 1) // MOE_BLOCK * MOE_BLOCK
    cum_padded = jnp.cumsum(padded)
    pstart = cum_padded - padded
    top_idx = meta[:, :TOP_K].astype(i32)
    rank = meta[:, TOP_K:2 * TOP_K].astype(i32)
    dest = (pstart[top_idx] + rank).reshape(-1)
    block_start = jnp.arange(nb, dtype=i32) * MOE_BLOCK
    block_expert = jnp.minimum(jnp.sum((cum_padded[None, :] <= block_start[:, None]).astype(i32), axis=1),
                               N_EXPERTS - 1)
    n_used = (cum_padded[-1:] // MOE_BLOCK).astype(i32)
    xb = _dispatch(hp, dest, P)
    yb = _experts(xb, block_expert, n_used, *ew)
    return _combine(yb, dest, meta, x1, gt2)


def _layer(x, c, p, pw, ew, rope):
    B, S, D = x.shape
    mod = _ada(c, p["w_ada"], p["b_ada"]).reshape(B, 6, 1, D)
    sh1, sc1, gt1, sh2, sc2, gt2 = (mod[:, j] for j in range(6))
    q, k, v, dq, dk, dv, dkt, z, gcol, grow = _inproj(
        x, sh1, sc1, p["norm1"], p["q_a_norm"], p["kv_a_norm"], pw, rope)
    o_mla = _attention(q, k, v)
    o_f, o_b = _gdn(dq, dk, dv, dkt, gcol, grow)
    dnn = jnp.tile(p["dn_out_norm"], 1)
    x1, hp, meta, cnt = _outproj(x, o_mla, o_f, o_b, z, gt1, sc2, sh2, p["w_o"], dnn, p["norm2"],
                                 p["w_router"], p["b_router"])
    return _moe(hp, meta, cnt, x1, gt2, ew)


def kernel(x_prompt, x_sample, c_prompt, c_sample, w_ada, b_ada, norm1, w_in, q_a_norm, w_q_b, kv_a_norm, w_kv_b, q_norm, k_norm, dn_conv, dn_a_log, dn_dt_bias, dn_out_norm, w_o, norm2, w_router, b_router, w_gate, b_gate, w_up, b_up, w_down, b_down):
    y_prompt, y_sample = x_prompt, x_sample
    depth = w_ada.shape[0]
    for l in range(depth):
        p = {"w_ada": w_ada[l], "b_ada": b_ada[l], "norm1": norm1[l], "q_a_norm": q_a_norm[l],
             "kv_a_norm": kv_a_norm[l], "dn_out_norm": dn_out_norm[l], "w_o": w_o[l], "norm2": norm2[l],
             "w_router": w_router[l], "b_router": b_router[l]}
        pw = _prep_weights(w_in[l], w_q_b[l], w_kv_b[l], q_norm[l], k_norm[l], dn_conv[l], dn_a_log[l],
                           dn_dt_bias[l])
        ew = (w_gate[l].astype(bf16), b_gate[l], w_up[l].astype(bf16), b_up[l], w_down[l].astype(bf16), b_down[l])
        y_prompt = _layer(y_prompt, c_prompt, p, pw, ew, _rope_tables(y_prompt.shape[1]))
        y_sample = _layer(y_sample, c_sample, p, pw, ew, _rope_tables(y_sample.shape[1]))
    return (y_prompt, y_sample)
```

```python
import functools
import math

import jax
import jax.numpy as jnp
from jax import lax
from jax.experimental import pallas as pl
from jax.experimental.pallas import tpu as pltpu

f32 = jnp.float32
bf16 = jnp.bfloat16
u32 = jnp.uint32
i32 = jnp.int32

LANES = 128
VMEM_LIMIT = 56 * 1024 * 1024

MLA_HEADS = 8
MLA_Q_LORA = 384
MLA_KV_LORA = 256
MLA_NOPE = 64
MLA_ROPE = 32
MLA_QK = MLA_NOPE + MLA_ROPE
MLA_V = 64
ROPE_THETA = 10000.0
DN_HEADS = 4
DN_DK = 128
DN_DV = 128
DN_CONV = 5
DN_CHUNK = 64
DN_SUB = 16
N_EXPERTS = 32
TOP_K = 4
SWIGLU_LIMIT = 7.0
SWIGLU_ALPHA = 1.702
MOE_BLOCK = 256
EPS = 1e-6

TM_IN = 256
TM_OUT = 256
TQ = 256
TK = 512
GDN_ROWS = 128
GDN_SCAN_ROWS = 256
TT_DISPATCH = 512
TT_COMBINE = 256


def _cparams(sem):
    return pltpu.CompilerParams(dimension_semantics=sem, vmem_limit_bytes=VMEM_LIMIT)


def _split2(x):
    hi = x.astype(bf16)
    lo = (x - hi.astype(f32)).astype(bf16)
    return hi, lo


def _split3(x):
    hi = x.astype(bf16)
    r = x - hi.astype(f32)
    mid = r.astype(bf16)
    lo = (r - mid.astype(f32)).astype(bf16)
    return hi, mid, lo


def _dot(a, b):
    return jnp.dot(a, b, preferred_element_type=f32)


def _dot_nt(a, b):
    return lax.dot_general(a, b, (((1,), (1,)), ((), ())), preferred_element_type=f32)


def _dot3(a, b):
    ah, al = _split2(a)
    bh, bl = _split2(b)
    return _dot(ah, bh) + (_dot(al, bh) + _dot(ah, bl))


def _sigmoid(x):
    return 1.0 / (1.0 + jnp.exp(-x))


def _silu(x):
    return x * _sigmoid(x)


def _softplus(x):
    return jnp.maximum(x, 0.0) + jnp.log(1.0 + jnp.exp(-jnp.abs(x)))


def _ada_kernel(c_ref, w_ref, b_ref, o_ref):
    c = c_ref[...]
    o_ref[...] = _dot3(_silu(c), w_ref[...]) + b_ref[...]


def _ada(c, w_ada, b_ada):
    B, D = c.shape
    N = w_ada.shape[1]
    cp = jnp.zeros((8, D), f32).at[:B].set(c)
    tn = 1024
    out = pl.pallas_call(
        _ada_kernel,
        grid=(N // tn,),
        in_specs=[
            pl.BlockSpec((8, D), lambda j: (0, 0)),
            pl.BlockSpec((D, tn), lambda j: (0, j)),
            pl.BlockSpec((1, tn), lambda j: (0, j)),
        ],
        out_specs=pl.BlockSpec((8, tn), lambda j: (0, j)),
        out_shape=jax.ShapeDtypeStruct((8, N), f32),
        compiler_params=_cparams(("parallel",)),
        name="ada",
    )(cp, w_ada, b_ada.reshape(1, N))
    return out[:B]


def _chunk_tri(n, lower):
    r = lax.broadcasted_iota(i32, (n, n), 0)
    c = lax.broadcasted_iota(i32, (n, n), 1)
    same = (r // DN_CHUNK) == (c // DN_CHUNK)
    tri = (c <= r) if lower else (c >= r)
    return jnp.where(same & tri, 1.0, 0.0).astype(bf16)


def _inproj_kernel(
    x_ref, xp_ref, xn_ref, sh_ref, sc_ref, n1_ref,
    wqa_ref, wkva_ref, wkr_ref, wqkv_ref, wz_ref, wab_ref, wabt_ref,
    qan_ref, kvan_ref, wqb_ref, wkb_ref, wvb_ref, qn_ref, kn_ref,
    cos_ref, sina_ref, sinb_ref, conv_ref, gpar_ref, gpart_ref,
    q_ref, k_ref, v_ref, dq_ref, dk_ref, dv_ref, dkt_ref, z_ref, gcol_ref, grow_ref,
    ext_ref,
):
    i = pl.program_id(1)
    ni = pl.num_programs(1)
    tm = x_ref.shape[1]
    scale1 = 1.0 + sc_ref[0]
    shift1 = sh_ref[0]
    n1 = n1_ref[...]

    def modulate(xv):
        y = xv * lax.rsqrt(jnp.mean(xv * xv, axis=-1, keepdims=True) + EPS)
        return y * n1 * scale1 + shift1

    h = modulate(x_ref[0])
    hb = h.astype(bf16)
    hh = modulate(jnp.concatenate([xp_ref[0], xn_ref[0]], axis=0)).astype(bf16)

    qa = _dot(hb, wqa_ref[...])
    qa = qa * lax.rsqrt(jnp.mean(qa * qa, axis=-1, keepdims=True) + EPS) * qan_ref[...]
    kva = _dot(hb, wkva_ref[...])
    kva = kva * lax.rsqrt(jnp.mean(kva * kva, axis=-1, keepdims=True) + EPS) * kvan_ref[...]
    kr = _dot(hb, wkr_ref[...])
    qh = _dot(qa.astype(bf16), wqb_ref[...])
    kvb = kva.astype(bf16)
    kh = _dot(kvb, wkb_ref[...])
    vh = _dot(kvb, wvb_ref[...])
    cos = cos_ref[...]
    sina = sina_ref[...]
    sinb = sinb_ref[...]
    qg = qn_ref[...]
    kg = kn_ref[...]
    lane = lax.broadcasted_iota(i32, (tm, LANES), 1)
    q_scale = MLA_QK ** -0.5 * math.log2(math.e)

    def norm_rope(blk, gain):
        ss = jnp.sum(blk * blk, axis=-1, keepdims=True) * (1.0 / MLA_QK)
        y = blk * lax.rsqrt(ss + EPS) * gain
        return y * cos + pltpu.roll(y, LANES - MLA_ROPE // 2, 1) * sina + pltpu.roll(y, MLA_ROPE // 2, 1) * sinb

    for hd in range(MLA_HEADS):
        sl = slice(hd * LANES, (hd + 1) * LANES)
        q_ref[0, hd] = (norm_rope(qh[:, sl], qg) * q_scale).astype(bf16)
        k_ref[0, hd] = norm_rope(kh[:, sl] + kr, kg).astype(bf16)
        vblk = vh[:, hd * MLA_V:(hd + 1) * MLA_V]
        vpad = jnp.concatenate([vblk, jnp.zeros((tm, LANES - MLA_V), f32)], axis=1)
        v_ref[0, hd] = jnp.where(lane == MLA_V, 1.0, vpad).astype(bf16)

    ext_ref[pl.ds(8, tm), :] = _dot(hb, wqkv_ref[...])
    halo = _dot(hh, wqkv_ref[...])
    ext_ref[pl.ds(0, 8), :] = jnp.where(i == 0, 0.0, halo[:8])
    ext_ref[pl.ds(8 + tm, 8), :] = jnp.where(i == ni - 1, 0.0, halo[8:])
    pad = (DN_CONV - 1) // 2
    cw = conv_ref[...]
    acc = ext_ref[pl.ds(8 - pad, tm), :] * cw[0:1]
    for j in range(1, DN_CONV):
        acc = acc + ext_ref[pl.ds(8 - pad + j, tm), :] * cw[j:j + 1]
    act = _silu(acc)
    nqk = DN_HEADS * DN_DK
    for hd in range(DN_HEADS):
        sl = slice(hd * DN_DK, (hd + 1) * DN_DK)
        qb = act[:, sl]
        qb = qb * lax.rsqrt(jnp.sum(qb * qb, axis=-1, keepdims=True) + EPS) * (DN_DK ** -0.5)
        dq_ref[0, :, sl] = qb
        kb = act[:, nqk + hd * DN_DK: nqk + (hd + 1) * DN_DK]
        kb = kb * lax.rsqrt(jnp.sum(kb * kb, axis=-1, keepdims=True) + EPS)
        dk_ref[0, :, sl] = kb
        dkt_ref[0, sl, :] = kb.T
    dv_ref[0] = act[:, 2 * nqk:]
    z_ref[0] = _dot(hb, wz_ref[...])

    hlo = (h - hb.astype(f32)).astype(bf16)
    wab = wab_ref[...]
    p1 = _dot(hb, wab)
    p2 = _dot(hlo, wab)
    ab = p1 + pltpu.roll(p1, LANES - 16, 1) + p2
    gpar = gpar_ref[...]
    lane16 = lax.broadcasted_iota(i32, (tm, LANES), 1)
    gval = jnp.where(lane16 < 8, -gpar[0:1] * _softplus(ab + gpar[1:2]), _sigmoid(ab))
    g3 = _split3(gval)
    lo_tri = _chunk_tri(tm, True)
    up_tri = _chunk_tri(tm, False)
    pre = _dot(lo_tri, g3[0]) + (_dot(lo_tri, g3[1]) + _dot(lo_tri, g3[2]))
    suf = _dot(up_tri, g3[0]) + (_dot(up_tri, g3[1]) + _dot(up_tri, g3[2]))
    gcol_ref[0] = jnp.where(lane16 < 4, pre, jnp.where(lane16 < 8, suf, gval))

    wabt = wabt_ref[...]
    r1 = _dot_nt(wabt, hb)
    r2 = _dot_nt(wabt[:16], hlo)
    abt = r1[:16] + r1[16:] + r2
    gpt = gpart_ref[...]
    row16 = lax.broadcasted_iota(i32, (16, tm), 0)
    gvt = jnp.where(row16 < 8, -gpt[:, 0:1] * _softplus(abt + gpt[:, 1:2]), _sigmoid(abt))
    t3 = _split3(gvt)
    pre_t = _dot(t3[0], up_tri) + (_dot(t3[1], up_tri) + _dot(t3[2], up_tri))
    suf_t = _dot(t3[0], lo_tri) + (_dot(t3[1], lo_tri) + _dot(t3[2], lo_tri))
    grow_ref[0] = jnp.where(row16 < 4, pre_t, jnp.where(row16 < 8, suf_t, gvt))


def _prep_weights(w_in, w_q_b, w_kv_b, q_norm, k_norm, dn_conv, dn_a_log, dn_dt_bias):
    D = w_in.shape[0]
    i0 = MLA_Q_LORA
    i1 = i0 + MLA_KV_LORA
    i2 = i1 + MLA_ROPE
    nqkv = DN_HEADS * (2 * DN_DK + DN_DV)
    i3 = i2 + nqkv
    i4 = i3 + DN_HEADS * DN_DV
    wqa = w_in[:, :i0].astype(bf16)
    wkva = w_in[:, i0:i1].astype(bf16)
    wkr = jnp.zeros((D, LANES), f32).at[:, MLA_NOPE:MLA_QK].set(w_in[:, i1:i2]).astype(bf16)
    wqkv = w_in[:, i2:i3].astype(bf16)
    wz = w_in[:, i3:i4].astype(bf16)
    wab_f = w_in[:, i4:]
    wab_hi = wab_f.astype(bf16)
    wab_lo = (wab_f - wab_hi.astype(f32)).astype(bf16)
    wab = jnp.zeros((D, LANES), bf16).at[:, :16].set(wab_hi).at[:, 16:32].set(wab_lo)
    wabt = jnp.concatenate([wab_hi.T, wab_lo.T], axis=0)
    wqb = w_q_b.reshape(MLA_Q_LORA, MLA_HEADS, MLA_QK)
    wqb = jnp.pad(wqb, ((0, 0), (0, 0), (0, LANES - MLA_QK))).reshape(MLA_Q_LORA, MLA_HEADS * LANES).astype(bf16)
    wkv = w_kv_b.reshape(MLA_KV_LORA, MLA_HEADS, MLA_NOPE + MLA_V)
    wkb = jnp.pad(wkv[:, :, :MLA_NOPE], ((0, 0), (0, 0), (0, LANES - MLA_NOPE)))
    wkb = wkb.reshape(MLA_KV_LORA, MLA_HEADS * LANES).astype(bf16)
    wvb = wkv[:, :, MLA_NOPE:].reshape(MLA_KV_LORA, MLA_HEADS * MLA_V).astype(bf16)
    qn = jnp.pad(q_norm, (0, LANES - MLA_QK)).reshape(1, LANES)
    kn = jnp.pad(k_norm, (0, LANES - MLA_QK)).reshape(1, LANES)
    conv = jnp.pad(dn_conv, ((0, 8 - DN_CONV), (0, 0)))
    ea = jnp.exp(dn_a_log.astype(f32)).reshape(-1)
    dtb = dn_dt_bias.astype(f32).reshape(-1)
    gpar = jnp.zeros((8, LANES), f32).at[0, :8].set(ea).at[1, :8].set(dtb)
    gpart = jnp.zeros((16, LANES), f32).at[:8, 0].set(ea).at[:8, 1].set(dtb)
    return dict(wqa=wqa, wkva=wkva, wkr=wkr, wqkv=wqkv, wz=wz, wab=wab, wabt=wabt, wqb=wqb, wkb=wkb,
                wvb=wvb, qn=qn, kn=kn, conv=conv, gpar=gpar, gpart=gpart)


def _rope_tables(S):
    half = MLA_ROPE // 2
    freq = ROPE_THETA ** (-jnp.arange(half, dtype=f32) / half)
    ang = jnp.arange(S, dtype=f32)[:, None] * freq[None, :]
    cos, sin = jnp.cos(ang), jnp.sin(ang)
    z = jnp.zeros((S, LANES), f32)
    cos_t = z.at[:, :MLA_NOPE].set(1.0).at[:, MLA_NOPE:MLA_NOPE + half].set(cos).at[:, MLA_NOPE + half:MLA_QK].set(cos)
    sina = z.at[:, MLA_NOPE:MLA_NOPE + half].set(-sin)
    sinb = z.at[:, MLA_NOPE + half:MLA_QK].set(sin)
    return cos_t, sina, sinb


def _inproj(x, sh1, sc1, norm1, q_a_norm, kv_a_norm, pw, rope):
    B, S, D = x.shape
    tm = min(TM_IN, S)
    nt = S // tm
    r8 = tm // 8
    nqkv = DN_HEADS * (2 * DN_DK + DN_DV)
    nh = DN_HEADS * DN_DK

    def full(a):
        return pl.BlockSpec(a.shape, lambda b, i: (0,) * a.ndim)

    tok = lambda w: pl.BlockSpec((1, tm, w), lambda b, i: (b, i, 0))
    in_specs = [
        tok(D),
        pl.BlockSpec((1, 8, D), lambda b, i: (b, jnp.maximum(i * r8 - 1, 0), 0)),
        pl.BlockSpec((1, 8, D), lambda b, i: (b, jnp.minimum((i + 1) * r8, S // 8 - 1), 0)),
        pl.BlockSpec((1, 1, D), lambda b, i: (b, 0, 0)),
        pl.BlockSpec((1, 1, D), lambda b, i: (b, 0, 0)),
    ]
    consts = [norm1.reshape(1, D), pw["wqa"], pw["wkva"], pw["wkr"], pw["wqkv"], pw["wz"], pw["wab"], pw["wabt"],
              q_a_norm.reshape(1, -1), kv_a_norm.reshape(1, -1), pw["wqb"], pw["wkb"], pw["wvb"], pw["qn"], pw["kn"]]
    in_specs += [full(a) for a in consts]
    in_specs += [pl.BlockSpec((tm, LANES), lambda b, i: (i, 0))] * 3
    tail = [pw["conv"], pw["gpar"], pw["gpart"]]
    in_specs += [full(a) for a in tail]
    hspec = pl.BlockSpec((1, MLA_HEADS, tm, LANES), lambda b, i: (b, 0, i, 0))
    out_specs = [hspec, hspec, hspec, tok(nh), tok(nh), tok(nh),
                 pl.BlockSpec((1, nh, tm), lambda b, i: (b, 0, i)), tok(nh), tok(LANES),
                 pl.BlockSpec((1, 16, tm), lambda b, i: (b, 0, i))]
    hshape = jax.ShapeDtypeStruct((B, MLA_HEADS, S, LANES), bf16)
    tshape = jax.ShapeDtypeStruct((B, S, nh), f32)
    out_shape = [hshape, hshape, hshape, tshape, tshape, tshape,
                 jax.ShapeDtypeStruct((B, nh, S), f32), tshape,
                 jax.ShapeDtypeStruct((B, S, LANES), f32), jax.ShapeDtypeStruct((B, 16, S), f32)]
    return pl.pallas_call(
        _inproj_kernel,
        grid=(B, nt),
        in_specs=in_specs,
        out_specs=out_specs,
        out_shape=out_shape,
        scratch_shapes=[pltpu.VMEM((tm + 16, nqkv), f32)],
        compiler_params=_cparams(("parallel", "parallel")),
        name="inproj",
    )(x, x, x, sh1, sc1, *consts, *rope, *tail)


def _attn_kernel(q_ref, k_ref, v_ref, o_ref, s_scr, m_scr, acc_scr):
    S = k_ref.shape[2]
    tk = s_scr.shape[3]
    nk = S // tk
    nh = q_ref.shape[1]
    qs = [q_ref[0, hd] for hd in range(nh)]
    m_scr[...] = jnp.full(m_scr.shape, -jnp.inf, f32)
    acc_scr[...] = jnp.zeros(acc_scr.shape, f32)

    def scores(slot, j):
        for hd in range(nh):
            s_scr[slot, hd] = _dot_nt(qs[hd], k_ref[0, hd, pl.ds(j * tk, tk), :])

    def accumulate(slot, j):
        off = j * tk
        for hd in range(nh):
            s = s_scr[slot, hd]
            m = m_scr[hd]
            m_new = jnp.maximum(m, jnp.broadcast_to(jnp.max(s, axis=-1, keepdims=True), m.shape))
            p = jnp.exp2(s - jnp.tile(m_new, (1, tk // LANES)))
            m_scr[hd] = m_new
            acc_scr[hd] = acc_scr[hd] * jnp.exp2(m - m_new) + _dot(p.astype(bf16), v_ref[0, hd, pl.ds(off, tk), :])

    scores(0, 0)
    for j in range(nk):
        if j + 1 < nk:
            scores((j + 1) % 2, j + 1)
        accumulate(j % 2, j)
    outs = [acc_scr[hd][:, :MLA_V] / acc_scr[hd][:, MLA_V:MLA_V + 1] for hd in range(nh)]
    o_ref[0] = jnp.concatenate(outs, axis=1).astype(o_ref.dtype)


def _attention(q, k, v):
    B, H, S, _ = q.shape
    tq = min(TQ, S)
    tk = min(TK, S // 2)
    nh = 2
    qspec = pl.BlockSpec((1, nh, tq, LANES), lambda b, h, i: (b, h, i, 0))
    kspec = pl.BlockSpec((1, nh, S, LANES), lambda b, h, i: (b, h, 0, 0))
    return pl.pallas_call(
        _attn_kernel,
        grid=(B, H // nh, S // tq),
        in_specs=[qspec, kspec, kspec],
        out_specs=pl.BlockSpec((1, tq, LANES), lambda b, h, i: (b, i, h)),
        out_shape=jax.ShapeDtypeStruct((B, S, H * MLA_V), bf16),
        scratch_shapes=[pltpu.VMEM((2, nh, tq, tk), f32), pltpu.VMEM((nh, tq, LANES), f32),
                        pltpu.VMEM((nh, tq, LANES), f32)],
        compiler_params=_cparams(("parallel", "parallel", "arbitrary")),
        name="attn",
    )(q, k, v)


def _unit_tri_inverses(Ls, sub_mask):
    C = Ls[0].shape[0]
    r = lax.broadcasted_iota(i32, (C, C), 0)
    c = lax.broadcasted_iota(i32, (C, C), 1)
    eye = jnp.where(r == c, 1.0, 0.0)
    Lds = [jnp.where(sub_mask, L, 0.0) for L in Ls]
    Los = [L - Ld for L, Ld in zip(Ls, Lds)]
    Ts = [eye - Ld for Ld in Lds]
    Ps = Lds
    n = 2
    while n < DN_SUB:
        Ps = [_dot3k(P, P) for P in Ps]
        Ts = [_dot3k(T, eye + P) for T, P in zip(Ts, Ps)]
        n *= 2
    Ns = [_dot3k(T, Lo) for T, Lo in zip(Ts, Los)]
    N2s = [_dot3k(N, N) for N in Ns]
    Ms = [_dot3k(eye - N, eye + N2) for N, N2 in zip(Ns, N2s)]
    return [_dot3k(M, T) for M, T in zip(Ms, Ts)]


def _dot3k(a, b):
    ah = a.astype(bf16).astype(f32)
    lhs = jnp.concatenate([a, a - ah, a], axis=1).astype(bf16)
    bh = b.astype(bf16)
    bl = (b - bh.astype(f32)).astype(bf16)
    rhs = jnp.concatenate([bh, bh, bl], axis=0)
    return _dot(lhs, rhs)


def _gdn_prep_kernel(q_ref, k_ref, v_ref, kt_ref, gcol_ref, grow_ref,
                     uf_ref, wf_ref, qf_ref, af_ref, ktf_ref, ub_ref, wb_ref, qb_ref, ab_ref, ktb_ref, dl_ref):
    C = DN_CHUNK
    nch = q_ref.shape[1] // C
    r = lax.broadcasted_iota(i32, (C, C), 0)
    c = lax.broadcasted_iota(i32, (C, C), 1)
    sub_mask = (r // DN_SUB) == (c // DN_SUB)
    outs = ((uf_ref, wf_ref, qf_ref, af_ref, ktf_ref), (ub_ref, wb_ref, qb_ref, ab_ref, ktb_ref))
    chains = [(ch, hd) for ch in range(nch) for hd in range(DN_HEADS)]
    kks, qks = {}, {}
    for ch, hd in chains:
        rs = slice(ch * C, (ch + 1) * C)
        hs = slice(hd * DN_DK, (hd + 1) * DN_DK)
        kbf = k_ref[0, rs, hs].astype(bf16)
        kks[ch, hd] = _dot_nt(kbf, kbf)
        qks[ch, hd] = _dot_nt(q_ref[0, rs, hs].astype(bf16), kbf)
    for ch in range(nch):
        rs = slice(ch * C, (ch + 1) * C)
        dls = []
        for d, reverse in enumerate((False, True)):
            for hd in range(DN_HEADS):
                gr = grow_ref[0, 4 * d + hd: 4 * d + hd + 1, rs]
                g_last = gr[:, 0:1] if reverse else gr[:, C - 1:C]
                dls.append(jnp.broadcast_to(jnp.exp(g_last), (1, LANES)))
        dl_ref[0, ch] = jnp.concatenate(dls, axis=0)

    full = [(ch, hd, d) for ch, hd in chains for d in range(2)]
    Ls, rhss = [], []
    for ch, hd, d in full:
        rs = slice(ch * C, (ch + 1) * C)
        hs = slice(hd * DN_DK, (hd + 1) * DN_DK)
        incl = (c >= r) if d else (c <= r)
        gc = gcol_ref[0, rs, 4 * d + hd: 4 * d + hd + 1]
        beta = gcol_ref[0, rs, 8 + 4 * d + hd: 8 + 4 * d + hd + 1]
        gr = grow_ref[0, 4 * d + hd: 4 * d + hd + 1, rs]
        dec = jnp.exp(jnp.where(incl, gc - gr, -jnp.inf))
        outs[d][3][0, hd, rs, :] = (qks[ch, hd] * dec).astype(bf16)
        Ls.append(jnp.where(r == c, 0.0, beta * kks[ch, hd] * dec))
        g_last = gr[:, 0:1] if d else gr[:, C - 1:C]
        outs[d][4][0, ch, hs, :] = (kt_ref[0, hs, rs] * jnp.exp(g_last - gr)).astype(bf16)
        eg = jnp.exp(gc)
        outs[d][2][0, rs, hs] = (q_ref[0, rs, hs] * eg).astype(bf16)
        rhss.append(jnp.concatenate([v_ref[0, rs, hs] * beta, k_ref[0, rs, hs] * (beta * eg)], axis=1))
    Ts = _unit_tri_inverses(Ls, sub_mask)
    sols = [_dot3k(T, rhs) for T, rhs in zip(Ts, rhss)]
    for (ch, hd, d), sol in zip(full, sols):
        rs = slice(ch * C, (ch + 1) * C)
        hs = slice(hd * DN_DK, (hd + 1) * DN_DK)
        outs[d][0][0, rs, hs] = sol[:, :DN_DV]
        outs[d][1][0, rs, hs] = sol[:, DN_DV:].astype(bf16)


def _gdn_scan_kernel(uf_ref, wf_ref, qf_ref, af_ref, ktf_ref, dlf_ref,
                     ub_ref, wb_ref, qb_ref, ab_ref, ktb_ref, dlb_ref, of_ref, ob_ref, s_ref):
    @pl.when(pl.program_id(1) == 0)
    def _():
        s_ref[...] = jnp.zeros_like(s_ref)

    C = DN_CHUNK
    nch = uf_ref.shape[1] // C
    dirs = ((uf_ref, wf_ref, qf_ref, af_ref, ktf_ref, dlf_ref, of_ref),
            (ub_ref, wb_ref, qb_ref, ab_ref, ktb_ref, dlb_ref, ob_ref))
    chains = [(d, hd) for d in range(2) for hd in range(DN_HEADS)]
    sts = [s_ref[d, hd] for d, hd in chains]
    for step in range(nch):
        def sl(d, hd):
            ch = step if d == 0 else nch - 1 - step
            return ch, slice(ch * C, (ch + 1) * C), slice(hd * DN_DK, (hd + 1) * DN_DK)

        wss = []
        for (d, hd), st in zip(chains, sts):
            ch, rs, hs = sl(d, hd)
            wq = jnp.concatenate([dirs[d][1][0, rs, hs], dirs[d][2][0, rs, hs]], axis=0)
            wss.append(_dot(wq, st.astype(bf16)))
        vnbs = []
        for (d, hd), ws in zip(chains, wss):
            ch, rs, hs = sl(d, hd)
            vnbs.append((dirs[d][0][0, rs, hs] - ws[:C]).astype(bf16))
        new = []
        for (d, hd), st, ws, vnb in zip(chains, sts, wss, vnbs):
            ch, rs, hs = sl(d, hd)
            dirs[d][6][0, rs, hs] = ws[C:] + _dot(dirs[d][3][0, hd, rs, :], vnb)
            dl = dirs[d][5][0, ch, d * DN_HEADS + hd: d * DN_HEADS + hd + 1, :]
            new.append(st * dl + _dot(dirs[d][4][0, ch, hs, :], vnb))
        sts = new
    for (d, hd), st in zip(chains, sts):
        s_ref[d, hd] = st


def _gdn(dq, dk, dv, dkt, gcol, grow):
    B, S, W = dq.shape
    C = DN_CHUNK
    rows = min(GDN_ROWS, S)
    n = S // rows
    nch = rows // C
    tok = pl.BlockSpec((1, rows, W), lambda b, i: (b, i, 0))
    aspec = pl.BlockSpec((1, DN_HEADS, rows, C), lambda b, i: (b, 0, i, 0))
    kspec = pl.BlockSpec((1, nch, W, C), lambda b, i: (b, i, 0, 0))
    dspec = pl.BlockSpec((1, nch, 2 * DN_HEADS, LANES), lambda b, i: (b, i, 0, 0))
    t32 = jax.ShapeDtypeStruct((B, S, W), f32)
    t16 = jax.ShapeDtypeStruct((B, S, W), bf16)
    ashape = jax.ShapeDtypeStruct((B, DN_HEADS, S, C), bf16)
    kshape = jax.ShapeDtypeStruct((B, S // C, W, C), bf16)
    per_dir_specs = [tok, tok, tok, aspec, kspec]
    per_dir_shapes = [t32, t16, t16, ashape, kshape]
    prep = pl.pallas_call(
        _gdn_prep_kernel,
        grid=(B, n),
        in_specs=[tok, tok, tok, pl.BlockSpec((1, W, rows), lambda b, i: (b, 0, i)),
                  pl.BlockSpec((1, rows, LANES), lambda b, i: (b, i, 0)),
                  pl.BlockSpec((1, 16, rows), lambda b, i: (b, 0, i))],
        out_specs=per_dir_specs * 2 + [dspec],
        out_shape=per_dir_shapes * 2 + [jax.ShapeDtypeStruct((B, S // C, 2 * DN_HEADS, LANES), f32)],
        compiler_params=_cparams(("parallel", "parallel")),
        name="gdn_prep",
    )(dq, dk, dv, dkt, gcol, grow)
    fwd, bwd, dl = prep[:5], prep[5:10], prep[10]

    rows_s = min(GDN_SCAN_ROWS, S)
    ns = S // rows_s
    nchs = rows_s // C

    def specs(rev):
        blk = (lambda i: ns - 1 - i) if rev else (lambda i: i)
        return [pl.BlockSpec((1, rows_s, W), lambda b, i: (b, blk(i), 0))] * 3 + [
            pl.BlockSpec((1, DN_HEADS, rows_s, C), lambda b, i: (b, 0, blk(i), 0)),
            pl.BlockSpec((1, nchs, W, C), lambda b, i: (b, blk(i), 0, 0)),
            pl.BlockSpec((1, nchs, 2 * DN_HEADS, LANES), lambda b, i: (b, blk(i), 0, 0))]

    return pl.pallas_call(
        _gdn_scan_kernel,
        grid=(B, ns),
        in_specs=specs(False) + specs(True),
        out_specs=[specs(False)[0], specs(True)[0]],
        out_shape=[t32, t32],
        scratch_shapes=[pltpu.VMEM((2, DN_HEADS, DN_DK, DN_DV), f32)],
        compiler_params=_cparams(("parallel", "arbitrary")),
        name="gdn_scan",
    )(*fwd, dl, *bwd, dl)


def _pack_bf16_pairs(y):
    w = y.shape[1] // 2
    lo = pltpu.bitcast(y[:, :w].astype(bf16).astype(f32), u32)
    hi = pltpu.bitcast(y[:, w:].astype(bf16).astype(f32), u32)
    return (lo >> 16) | (hi & jnp.uint32(0xFFFF0000))


def _unpack_bf16_pairs(p):
    lo = pltpu.bitcast(p << 16, f32)
    hi = pltpu.bitcast(p & jnp.uint32(0xFFFF0000), f32)
    return jnp.concatenate([lo, hi], axis=1)


ROW_SLABS = 4


def _store_rows(ref, packed):
    n = packed.shape[0]
    for j in range(ROW_SLABS):
        ref[pl.ds(j, n, stride=ROW_SLABS), :] = packed[:, j * LANES:(j + 1) * LANES]


def _load_rows(ref, n):
    return jnp.concatenate([ref[pl.ds(j, n, stride=ROW_SLABS), :] for j in range(ROW_SLABS)], axis=1)


def _outproj_kernel(x_ref, om_ref, of_ref, ob_ref, z_ref, gt_ref, sc_ref, sh_ref,
                    wo_ref, dnn_ref, n2_ref, wr_ref, br_ref,
                    x1_ref, hp_ref, meta_ref, cnt_ref, carry_ref):
    first = (pl.program_id(0) == 0) & (pl.program_id(1) == 0)

    @pl.when(first)
    def _():
        carry_ref[...] = jnp.zeros_like(carry_ref)

    tm = x_ref.shape[1]
    o = of_ref[0] + ob_ref[0]
    z = z_ref[0]
    dnn = dnn_ref[...]
    parts = []
    for hd in range(DN_HEADS):
        sl = slice(hd * DN_DV, (hd + 1) * DN_DV)
        ob = o[:, sl]
        ob = ob * lax.rsqrt(jnp.mean(ob * ob, axis=-1, keepdims=True) + EPS) * dnn
        parts.append(ob * _silu(z[:, sl]))
    odn = jnp.concatenate(parts, axis=1).astype(bf16)
    nm = om_ref.shape[2]
    mixed = _dot(om_ref[0], wo_ref[pl.ds(0, nm), :]) + _dot(odn, wo_ref[pl.ds(nm, odn.shape[1]), :])
    x1 = x_ref[0] + gt_ref[0] * mixed
    x1_ref[0] = x1
    h2 = x1 * lax.rsqrt(jnp.mean(x1 * x1, axis=-1, keepdims=True) + EPS) * n2_ref[...]
    h2 = h2 * (1.0 + sc_ref[0]) + sh_ref[0]
    _store_rows(hp_ref, _pack_bf16_pairs(h2))

    hh, hl = _split2(h2)
    wr = wr_ref[...]
    p1 = _dot(hh, wr)
    p2 = _dot(hl, wr[:, :LANES])
    lane = lax.broadcasted_iota(i32, (tm, LANES), 1)
    logits = p1[:, :LANES] + p1[:, LANES:] + p2 + br_ref[...]
    logits = jnp.where(lane < N_EXPERTS, logits, -jnp.inf)
    vals, idxs = [], []
    work = logits
    for _ in range(TOP_K):
        mx = jnp.max(work, axis=-1, keepdims=True)
        ix = jnp.min(jnp.where(work == mx, lane, LANES), axis=-1, keepdims=True)
        vals.append(mx)
        idxs.append(ix)
        work = jnp.where(lane == ix, -jnp.inf, work)
    es = [jnp.exp(vv - vals[0]) for vv in vals]
    den = es[0] + es[1] + es[2] + es[3]
    multihot = jnp.where(work != logits, 1.0, 0.0)
    rr = lax.broadcasted_iota(i32, (tm, tm), 0)
    cc = lax.broadcasted_iota(i32, (tm, tm), 1)
    below = jnp.where(cc < rr, 1.0, 0.0).astype(bf16)
    prefix = _dot(below, multihot.astype(bf16)) + carry_ref[0:1]
    meta = jnp.zeros((tm, LANES), f32)
    for kk in range(TOP_K):
        rank = jnp.sum(jnp.where(lane == idxs[kk], prefix, 0.0), axis=-1, keepdims=True)
        meta = jnp.where(lane == kk, idxs[kk].astype(f32), meta)
        meta = jnp.where(lane == TOP_K + kk, rank, meta)
        meta = jnp.where(lane == 2 * TOP_K + kk, es[kk] / den, meta)
    meta_ref[...] = meta
    carry = carry_ref[...] + jnp.sum(multihot, axis=0, keepdims=True)
    carry_ref[...] = carry
    cnt_ref[...] = carry


def _outproj(x, o_mla, o_f, o_b, z, gt1, sc2, sh2, w_o, dn_out_norm, norm2, w_router, b_router):
    B, S, D = x.shape
    tm = min(TM_OUT, S)
    nt = S // tm
    T = B * S
    nh = o_f.shape[2]
    wr_hi = w_router.astype(bf16)
    wr_lo = (w_router - wr_hi.astype(f32)).astype(bf16)
    wr = jnp.zeros((D, 2 * LANES), bf16).at[:, :N_EXPERTS].set(wr_hi).at[:, LANES:LANES + N_EXPERTS].set(wr_lo)
    br = jnp.zeros((1, LANES), f32).at[0, :N_EXPERTS].set(b_router)
    consts = [w_o.astype(bf16), dn_out_norm.reshape(1, -1), norm2.reshape(1, D), wr, br]

    def full(a):
        return pl.BlockSpec(a.shape, lambda b, i: (0,) * a.ndim)

    tok = lambda w: pl.BlockSpec((1, tm, w), lambda b, i: (b, i, 0))
    vec = pl.BlockSpec((1, 1, D), lambda b, i: (b, 0, 0))
    flat = lambda w: pl.BlockSpec((tm, w), lambda b, i: (b * nt + i, 0))
    return pl.pallas_call(
        _outproj_kernel,
        grid=(B, nt),
        in_specs=[tok(D), tok(o_mla.shape[2]), tok(nh), tok(nh), tok(nh), vec, vec, vec] + [full(a) for a in consts],
        out_specs=[tok(D), pl.BlockSpec((tm * ROW_SLABS, LANES), lambda b, i: (b * nt + i, 0)), flat(LANES),
                   pl.BlockSpec((8, LANES), lambda b, i: (0, 0))],
        out_shape=[jax.ShapeDtypeStruct((B, S, D), f32), jax.ShapeDtypeStruct((T * ROW_SLABS, LANES), u32),
                   jax.ShapeDtypeStruct((T, LANES), f32), jax.ShapeDtypeStruct((8, LANES), f32)],
        scratch_shapes=[pltpu.VMEM((8, LANES), f32)],
        compiler_params=_cparams(("arbitrary", "arbitrary")),
        name="outproj",
    )(x, o_mla, o_f, o_b, z, gt1, sc2, sh2, *consts)


DMA_UNROLL = 2


def _row_copy(src_ref, s, dst_ref, d, sem):
    return pltpu.make_async_copy(src_ref.at[pl.ds(pl.multiple_of(s * ROW_SLABS, ROW_SLABS), ROW_SLABS)],
                                 dst_ref.at[pl.ds(pl.multiple_of(d * ROW_SLABS, ROW_SLABS), ROW_SLABS)], sem)


def _dispatch_kernel(dest_ref, hp_ref, xb_in_ref, xb_ref, sem):
    del xb_in_ref
    n = dest_ref.shape[0]

    def issue(t, carry):
        for kk in range(TOP_K):
            _row_copy(hp_ref, t, xb_ref, dest_ref[t * TOP_K + kk], sem).start(priority=kk % 2)
        return carry

    lax.fori_loop(0, n // TOP_K, issue, 0, unroll=DMA_UNROLL)
    for _ in range(TOP_K):
        pltpu.make_async_copy(hp_ref, xb_ref.at[pl.ds(0, hp_ref.shape[0])], sem).wait()


def _dispatch(hp, dest, P):
    T = hp.shape[0] // ROW_SLABS
    tt = min(TT_DISPATCH, T)
    xb0 = jnp.zeros((P * ROW_SLABS, LANES), u32)
    return pl.pallas_call(
        _dispatch_kernel,
        grid=(T // tt,),
        in_specs=[pl.BlockSpec((tt * TOP_K,), lambda i: (i,), memory_space=pltpu.SMEM),
                  pl.BlockSpec((tt * ROW_SLABS, LANES), lambda i: (i, 0)), pl.BlockSpec(memory_space=pl.ANY)],
        out_specs=pl.BlockSpec(memory_space=pl.ANY),
        out_shape=jax.ShapeDtypeStruct((P * ROW_SLABS, LANES), u32),
        scratch_shapes=[pltpu.SemaphoreType.DMA(())],
        input_output_aliases={2: 0},
        compiler_params=pltpu.CompilerParams(dimension_semantics=("arbitrary",), has_side_effects=True),
        name="dispatch",
    )(dest, hp, xb0)


def _expert_kernel(be_ref, nb_ref, x_ref, wg_ref, bg_ref, wu_ref, bu_ref, wd_ref, bd_ref, y_ref):
    b = pl.program_id(0)

    @pl.when(b < nb_ref[0])
    def _():
        x = _unpack_bf16_pairs(_load_rows(x_ref, MOE_BLOCK)).astype(bf16)
        gt = jnp.minimum(_dot(x, wg_ref[0]) + bg_ref[0], SWIGLU_LIMIT)
        up = jnp.clip(_dot(x, wu_ref[0]) + bu_ref[0], -SWIGLU_LIMIT, SWIGLU_LIMIT)
        act = (up + 1.0) * gt * _sigmoid(SWIGLU_ALPHA * gt)
        y = _dot(act.astype(bf16), wd_ref[0]) + bd_ref[0]
        _store_rows(y_ref, _pack_bf16_pairs(y))

    @pl.when(b >= nb_ref[0])
    def _():
        y_ref[...] = jnp.zeros_like(y_ref)


def _experts(xb, block_expert, n_used, wg, bg, wu, bu, wd, bd):
    E, D, F = wg.shape
    nb = xb.shape[0] // (MOE_BLOCK * ROW_SLABS)
    wspec = lambda r, c: pl.BlockSpec((1, r, c), lambda b, be, nu: (be[b], 0, 0))
    xspec = pl.BlockSpec((MOE_BLOCK * ROW_SLABS, LANES), lambda b, be, nu: (b, 0))
    grid_spec = pltpu.PrefetchScalarGridSpec(
        num_scalar_prefetch=2,
        grid=(nb,),
        in_specs=[xspec, wspec(D, F), wspec(1, F), wspec(D, F), wspec(1, F), wspec(F, D), wspec(1, D)],
        out_specs=xspec,
    )
    return pl.pallas_call(
        _expert_kernel,
        grid_spec=grid_spec,
        out_shape=jax.ShapeDtypeStruct(xb.shape, u32),
        compiler_params=_cparams(("arbitrary",)),
        name="experts",
    )(block_expert, n_used, xb, wg, bg.reshape(E, 1, F), wu, bu.reshape(E, 1, F), wd, bd.reshape(E, 1, D))


def _combine_kernel(dest_ref, dnext_ref, yb_ref, meta_ref, x1_ref, gt_ref, o_ref, buf_ref, sem):
    n = dest_ref.shape[0]
    tt = n // TOP_K
    g = pl.program_id(0)
    slot = g % 2

    def gather(idx_ref, s):
        def issue(t, carry):
            for kk in range(TOP_K):
                _row_copy(yb_ref, idx_ref[t * TOP_K + kk], buf_ref.at[s, kk], t, sem.at[s]).start(priority=kk % 2)
            return carry

        lax.fori_loop(0, tt, issue, 0, unroll=DMA_UNROLL)

    @pl.when(g == 0)
    def _():
        gather(dest_ref, slot)

    @pl.when(g + 1 < pl.num_programs(0))
    def _():
        gather(dnext_ref, 1 - slot)

    for kk in range(TOP_K):
        pltpu.make_async_copy(yb_ref.at[pl.ds(0, tt * ROW_SLABS)], buf_ref.at[slot, kk], sem.at[slot]).wait()
    meta = meta_ref[...]
    moe = jnp.zeros((tt, x1_ref.shape[2]), f32)
    for kk in range(TOP_K):
        gate = meta[:, 2 * TOP_K + kk: 2 * TOP_K + kk + 1]
        moe = moe + gate * _unpack_bf16_pairs(_load_rows(buf_ref.at[slot, kk], tt))
    o_ref[0] = x1_ref[0] + gt_ref[0] * moe


def _combine(yb, dest, meta, x1, gt2):
    B, S, D = x1.shape
    tt = min(TT_COMBINE, S)
    nt = S // tt
    ng = B * nt
    return pl.pallas_call(
        _combine_kernel,
        grid=(ng,),
        in_specs=[pl.BlockSpec((tt * TOP_K,), lambda g: (g,), memory_space=pltpu.SMEM),
                  pl.BlockSpec((tt * TOP_K,), lambda g: (jnp.minimum(g + 1, ng - 1),), memory_space=pltpu.SMEM),
                  pl.BlockSpec(memory_space=pl.ANY),
                  pl.BlockSpec((tt, LANES), lambda g: (g, 0)),
                  pl.BlockSpec((1, tt, D), lambda g: (g // nt, g % nt, 0)),
                  pl.BlockSpec((1, 1, D), lambda g: (g // nt, 0, 0))],
        out_specs=pl.BlockSpec((1, tt, D), lambda g: (g // nt, g % nt, 0)),
        out_shape=jax.ShapeDtypeStruct((B, S, D), f32),
        scratch_shapes=[pltpu.VMEM((2, TOP_K, tt * ROW_SLABS, LANES), u32), pltpu.SemaphoreType.DMA((2,))],
        compiler_params=_cparams(("arbitrary",)),
        name="combine",
    )(dest, dest, yb, meta, x1, gt2)


def _moe(hp, meta, cnt, x1, gt2, ew):
    T = meta.shape[0]
    TK_ = T * TOP_K
    nb = -(-TK_ // MOE_BLOCK) + N_EXPERTS
    P = nb * MOE_BLOCK
    counts = cnt[0, :N_EXPERTS].astype(i32)
    padded = (counts + MOE_BLOCK - 1) // MOE_BLOCK * MOE_BLOCK
    cum_padded = jnp.cumsum(padded)
    pstart = cum_padded - padded
    top_idx = meta[:, :TOP_K].astype(i32)
    rank = meta[:, TOP_K:2 * TOP_K].astype(i32)
    dest = (pstart[top_idx] + rank).reshape(-1)
    block_start = jnp.arange(nb, dtype=i32) * MOE_BLOCK
    block_expert = jnp.minimum(jnp.sum((cum_padded[None, :] <= block_start[:, None]).astype(i32), axis=1),
                               N_EXPERTS - 1)
    n_used = (cum_padded[-1:] // MOE_BLOCK).astype(i32)
    xb = _dispatch(hp, dest, P)
    yb = _experts(xb, block_expert, n_used, *ew)
    return _combine(yb, dest, meta, x1, gt2)


def _layer(x, c, p, pw, ew, rope):
    B, S, D = x.shape
    mod = _ada(c, p["w_ada"], p["b_ada"]).reshape(B, 6, 1, D)
    sh1, sc1, gt1, sh2, sc2, gt2 = (mod[:, j] for j in range(6))
    q, k, v, dq, dk, dv, dkt, z, gcol, grow = _inproj(
        x, sh1, sc1, p["norm1"], p["q_a_norm"], p["kv_a_norm"], pw, rope)
    o_mla = _attention(q, k, v)
    o_f, o_b = _gdn(dq, dk, dv, dkt, gcol, grow)
    dnn = jnp.tile(p["dn_out_norm"], 1)
    x1, hp, meta, cnt = _outproj(x, o_mla, o_f, o_b, z, gt1, sc2, sh2, p["w_o"], dnn, p["norm2"],
                                 p["w_router"], p["b_router"])
    return _moe(hp, meta, cnt, x1, gt2, ew)


def kernel(x_prompt, x_sample, c_prompt, c_sample, w_ada, b_ada, norm1, w_in, q_a_norm, w_q_b, kv_a_norm, w_kv_b, q_norm, k_norm, dn_conv, dn_a_log, dn_dt_bias, dn_out_norm, w_o, norm2, w_router, b_router, w_gate, b_gate, w_up, b_up, w_down, b_down):
    y_prompt, y_sample = x_prompt, x_sample
    depth = w_ada.shape[0]
    for l in range(depth):
        p = {"w_ada": w_ada[l], "b_ada": b_ada[l], "norm1": norm1[l], "q_a_norm": q_a_norm[l],
             "kv_a_norm": kv_a_norm[l], "dn_out_norm": dn_out_norm[l], "w_o": w_o[l], "norm2": norm2[l],
             "w_router": w_router[l], "b_router": b_router[l]}
        pw = _prep_weights(w_in[l], w_q_b[l], w_kv_b[l], q_norm[l], k_norm[l], dn_conv[l], dn_a_log[l],
                           dn_dt_bias[l])
        ew = (w_gate[l].astype(bf16), b_gate[l], w_up[l].astype(bf16), b_up[l], w_down[l].astype(bf16), b_down[l])
        y_prompt = _layer(y_prompt, c_prompt, p, pw, ew, _rope_tables(y_prompt.shape[1]))
        y_sample = _layer(y_sample, c_sample, p, pw, ew, _rope_tables(y_sample.shape[1]))
    return (y_prompt, y_sample)
```

```python
import functools
import math

import jax
import jax.numpy as jnp
from jax import lax
from jax.experimental import pallas as pl
from jax.experimental.pallas import tpu as pltpu

f32 = jnp.float32
bf16 = jnp.bfloat16
u32 = jnp.uint32
i32 = jnp.int32

LANES = 128
VMEM_LIMIT = 56 * 1024 * 1024

MLA_HEADS = 8
MLA_Q_LORA = 384
MLA_KV_LORA = 256
MLA_NOPE = 64
MLA_ROPE = 32
MLA_QK = MLA_NOPE + MLA_ROPE
MLA_V = 64
ROPE_THETA = 10000.0
DN_HEADS = 4
DN_DK = 128
DN_DV = 128
DN_CONV = 5
DN_CHUNK = 64
DN_SUB = 16
N_EXPERTS = 32
TOP_K = 4
SWIGLU_LIMIT = 7.0
SWIGLU_ALPHA = 1.702
MOE_BLOCK = 512
MOE_SUB = 256
EPS = 1e-6

TM_IN = 256
TM_OUT = 256
TQ = 256
TK = 512
GDN_ROWS = 128
GDN_SCAN_ROWS = 512
TT_DISPATCH = 512
TT_COMBINE = 256


def _cparams(sem):
    return pltpu.CompilerParams(dimension_semantics=sem, vmem_limit_bytes=VMEM_LIMIT)


def _split2(x):
    hi = x.astype(bf16)
    lo = (x - hi.astype(f32)).astype(bf16)
    return hi, lo


def _split3(x):
    hi = x.astype(bf16)
    r = x - hi.astype(f32)
    mid = r.astype(bf16)
    lo = (r - mid.astype(f32)).astype(bf16)
    return hi, mid, lo


def _dot(a, b):
    return jnp.dot(a, b, preferred_element_type=f32)


def _dot_nt(a, b):
    return lax.dot_general(a, b, (((1,), (1,)), ((), ())), preferred_element_type=f32)


def _dot3(a, b):
    ah, al = _split2(a)
    bh, bl = _split2(b)
    return _dot(ah, bh) + (_dot(al, bh) + _dot(ah, bl))


def _sigmoid(x):
    return 1.0 / (1.0 + jnp.exp(-x))


def _silu(x):
    return x * _sigmoid(x)


def _softplus(x):
    return jnp.maximum(x, 0.0) + jnp.log(1.0 + jnp.exp(-jnp.abs(x)))


def _ada_kernel(c_ref, w_ref, b_ref, o_ref):
    c = c_ref[...]
    o_ref[...] = _dot3(_silu(c), w_ref[...]) + b_ref[...]


def _ada(c, w_ada, b_ada):
    B, D = c.shape
    N = w_ada.shape[1]
    cp = jnp.pad(c, ((0, 8 - B), (0, 0)))
    tn = 1024
    out = pl.pallas_call(
        _ada_kernel,
        grid=(N // tn,),
        in_specs=[
            pl.BlockSpec((8, D), lambda j: (0, 0)),
            pl.BlockSpec((D, tn), lambda j: (0, j)),
            pl.BlockSpec((1, tn), lambda j: (0, j)),
        ],
        out_specs=pl.BlockSpec((8, tn), lambda j: (0, j)),
        out_shape=jax.ShapeDtypeStruct((8, N), f32),
        compiler_params=_cparams(("parallel",)),
        name="ada",
    )(cp, w_ada, b_ada.reshape(1, N))
    return out[:B]


def _chunk_tri(n, lower):
    r = lax.broadcasted_iota(i32, (n, n), 0)
    c = lax.broadcasted_iota(i32, (n, n), 1)
    same = (r // DN_CHUNK) == (c // DN_CHUNK)
    tri = (c <= r) if lower else (c >= r)
    return jnp.where(same & tri, 1.0, 0.0).astype(bf16)


def _inproj_kernel(
    x_ref, xp_ref, xn_ref, sh_ref, sc_ref, n1_ref,
    wqa_ref, wkva_ref, wkr_ref, wqkv_ref, wz_ref, wab_ref, wabt_ref,
    qan_ref, kvan_ref, wqb_ref, wkb_ref, wvb_ref, qn_ref, kn_ref,
    cos_ref, sina_ref, sinb_ref, conv_ref, gpar_ref, gpart_ref, lo_ref, up_ref,
    q_ref, k_ref, v_ref, dq_ref, dk_ref, dv_ref, dkt_ref, z_ref, gcol_ref, grow_ref,
    ext_ref,
):
    i = pl.program_id(1)
    ni = pl.num_programs(1)
    tm = x_ref.shape[1]
    scale1 = 1.0 + sc_ref[0]
    shift1 = sh_ref[0]
    n1 = n1_ref[...]

    def modulate(xv):
        y = xv * lax.rsqrt(jnp.mean(xv * xv, axis=-1, keepdims=True) + EPS)
        return y * n1 * scale1 + shift1

    h = modulate(x_ref[0])
    hb = h.astype(bf16)
    hh = modulate(jnp.concatenate([xp_ref[0], xn_ref[0]], axis=0)).astype(bf16)

    qa = _dot(hb, wqa_ref[...])
    qa = qa * lax.rsqrt(jnp.mean(qa * qa, axis=-1, keepdims=True) + EPS) * qan_ref[...]
    kva = _dot(hb, wkva_ref[...])
    kva = kva * lax.rsqrt(jnp.mean(kva * kva, axis=-1, keepdims=True) + EPS) * kvan_ref[...]
    kr = _dot(hb, wkr_ref[...])
    qh = _dot(qa.astype(bf16), wqb_ref[...])
    kvb = kva.astype(bf16)
    kh = _dot(kvb, wkb_ref[...])
    vh = _dot(kvb, wvb_ref[...])
    cos = cos_ref[...]
    sina = sina_ref[...]
    sinb = sinb_ref[...]
    qg = qn_ref[...]
    kg = kn_ref[...]
    lane = lax.broadcasted_iota(i32, (tm, LANES), 1)
    q_scale = MLA_QK ** -0.5 * math.log2(math.e)

    def norm_rope(blk, gain):
        ss = jnp.sum(blk * blk, axis=-1, keepdims=True) * (1.0 / MLA_QK)
        y = blk * lax.rsqrt(ss + EPS) * gain
        return y * cos + pltpu.roll(y, LANES - MLA_ROPE // 2, 1) * sina + pltpu.roll(y, MLA_ROPE // 2, 1) * sinb

    for hd in range(MLA_HEADS):
        sl = slice(hd * LANES, (hd + 1) * LANES)
        q_ref[0, hd] = (norm_rope(qh[:, sl], qg) * q_scale).astype(bf16)
        k_ref[0, hd] = norm_rope(kh[:, sl] + kr, kg).astype(bf16)
        vblk = vh[:, hd * MLA_V:(hd + 1) * MLA_V]
        vpad = jnp.concatenate([vblk, jnp.zeros((tm, LANES - MLA_V), f32)], axis=1)
        v_ref[0, hd] = jnp.where(lane == MLA_V, 1.0, vpad).astype(bf16)

    ext_ref[pl.ds(8, tm), :] = _dot(hb, wqkv_ref[...])
    halo = _dot(hh, wqkv_ref[...])
    ext_ref[pl.ds(0, 8), :] = jnp.where(i == 0, 0.0, halo[:8])
    ext_ref[pl.ds(8 + tm, 8), :] = jnp.where(i == ni - 1, 0.0, halo[8:])
    pad = (DN_CONV - 1) // 2
    cw = conv_ref[...]
    acc = ext_ref[pl.ds(8 - pad, tm), :] * cw[0:1]
    for j in range(1, DN_CONV):
        acc = acc + ext_ref[pl.ds(8 - pad + j, tm), :] * cw[j:j + 1]
    act = _silu(acc)
    nqk = DN_HEADS * DN_DK
    for hd in range(DN_HEADS):
        sl = slice(hd * DN_DK, (hd + 1) * DN_DK)
        qb = act[:, sl]
        qb = qb * lax.rsqrt(jnp.sum(qb * qb, axis=-1, keepdims=True) + EPS) * (DN_DK ** -0.5)
        dq_ref[0, :, sl] = qb
        kb = act[:, nqk + hd * DN_DK: nqk + (hd + 1) * DN_DK]
        kb = kb * lax.rsqrt(jnp.sum(kb * kb, axis=-1, keepdims=True) + EPS)
        dk_ref[0, :, sl] = kb
        dkt_ref[0, sl, :] = kb.T
    dv_ref[0] = act[:, 2 * nqk:]
    z_ref[0] = _dot(hb, wz_ref[...])

    hlo = (h - hb.astype(f32)).astype(bf16)
    wab = wab_ref[...]
    p1 = _dot(hb, wab)
    p2 = _dot(hlo, wab)
    ab = p1 + pltpu.roll(p1, LANES - 16, 1) + p2
    gpar = gpar_ref[...]
    lane16 = lax.broadcasted_iota(i32, (tm, LANES), 1)
    gval = jnp.where(lane16 < 8, -gpar[0:1] * _softplus(ab + gpar[1:2]), _sigmoid(ab))
    g3 = _split3(gval)
    lo_tri = lo_ref[...]
    up_tri = up_ref[...]
    pre = _dot(lo_tri, g3[0]) + (_dot(lo_tri, g3[1]) + _dot(lo_tri, g3[2]))
    suf = _dot(up_tri, g3[0]) + (_dot(up_tri, g3[1]) + _dot(up_tri, g3[2]))
    gcol_ref[0] = jnp.where(lane16 < 4, pre, jnp.where(lane16 < 8, suf, gval))

    wabt = wabt_ref[...]
    r1 = _dot_nt(wabt, hb)
    r2 = _dot_nt(wabt[:16], hlo)
    abt = r1[:16] + r1[16:] + r2
    gpt = gpart_ref[...]
    row16 = lax.broadcasted_iota(i32, (16, tm), 0)
    gvt = jnp.where(row16 < 8, -gpt[:, 0:1] * _softplus(abt + gpt[:, 1:2]), _sigmoid(abt))
    t3 = _split3(gvt)
    pre_t = _dot(t3[0], up_tri) + (_dot(t3[1], up_tri) + _dot(t3[2], up_tri))
    suf_t = _dot(t3[0], lo_tri) + (_dot(t3[1], lo_tri) + _dot(t3[2], lo_tri))
    grow_ref[0] = jnp.where(row16 < 4, pre_t, jnp.where(row16 < 8, suf_t, gvt))


def _prep_weights(w_in, w_q_b, w_kv_b, q_norm, k_norm, dn_conv, dn_a_log, dn_dt_bias):
    D = w_in.shape[0]
    i0 = MLA_Q_LORA
    i1 = i0 + MLA_KV_LORA
    i2 = i1 + MLA_ROPE
    nqkv = DN_HEADS * (2 * DN_DK + DN_DV)
    i3 = i2 + nqkv
    i4 = i3 + DN_HEADS * DN_DV
    wqa = w_in[:, :i0].astype(bf16)
    wkva = w_in[:, i0:i1].astype(bf16)
    wkr = jnp.pad(w_in[:, i1:i2], ((0, 0), (MLA_NOPE, LANES - MLA_QK))).astype(bf16)
    wqkv = w_in[:, i2:i3].astype(bf16)
    wz = w_in[:, i3:i4].astype(bf16)
    wab_f = w_in[:, i4:]
    wab_hi = wab_f.astype(bf16)
    wab_lo = (wab_f - wab_hi.astype(f32)).astype(bf16)
    wab = jnp.pad(jnp.concatenate([wab_hi, wab_lo], axis=1), ((0, 0), (0, LANES - 32)))
    wabt = jnp.concatenate([wab_hi.T, wab_lo.T], axis=0)
    wqb = w_q_b.reshape(MLA_Q_LORA, MLA_HEADS, MLA_QK)
    wqb = jnp.pad(wqb, ((0, 0), (0, 0), (0, LANES - MLA_QK))).reshape(MLA_Q_LORA, MLA_HEADS * LANES).astype(bf16)
    wkv = w_kv_b.reshape(MLA_KV_LORA, MLA_HEADS, MLA_NOPE + MLA_V)
    wkb = jnp.pad(wkv[:, :, :MLA_NOPE], ((0, 0), (0, 0), (0, LANES - MLA_NOPE)))
    wkb = wkb.reshape(MLA_KV_LORA, MLA_HEADS * LANES).astype(bf16)
    wvb = wkv[:, :, MLA_NOPE:].reshape(MLA_KV_LORA, MLA_HEADS * MLA_V).astype(bf16)
    qn = jnp.pad(q_norm, (0, LANES - MLA_QK)).reshape(1, LANES)
    kn = jnp.pad(k_norm, (0, LANES - MLA_QK)).reshape(1, LANES)
    conv = jnp.pad(dn_conv, ((0, 8 - DN_CONV), (0, 0)))
    ea = jnp.exp(dn_a_log.astype(f32)).reshape(-1)
    dtb = dn_dt_bias.astype(f32).reshape(-1)
    gpar = jnp.pad(jnp.stack([ea, dtb]), ((0, 6), (0, LANES - 8)))
    gpart = jnp.pad(jnp.stack([ea, dtb], axis=1), ((0, 8), (0, LANES - 2)))
    return dict(wqa=wqa, wkva=wkva, wkr=wkr, wqkv=wqkv, wz=wz, wab=wab, wabt=wabt, wqb=wqb, wkb=wkb,
                wvb=wvb, qn=qn, kn=kn, conv=conv, gpar=gpar, gpart=gpart)


def _rope_tables(S):
    half = MLA_ROPE // 2
    freq = ROPE_THETA ** (-jnp.arange(half, dtype=f32) / half)
    ang = jnp.arange(S, dtype=f32)[:, None] * freq[None, :]
    cos, sin = jnp.cos(ang), jnp.sin(ang)
    zeros = lambda w: jnp.zeros((S, w), f32)
    cos_t = jnp.concatenate([jnp.ones((S, MLA_NOPE), f32), cos, cos, zeros(LANES - MLA_QK)], axis=1)
    sina = jnp.concatenate([zeros(MLA_NOPE), -sin, zeros(LANES - MLA_NOPE - half)], axis=1)
    sinb = jnp.concatenate([zeros(MLA_NOPE + half), sin, zeros(LANES - MLA_QK)], axis=1)
    return cos_t, sina, sinb


def _inproj(x, sh1, sc1, norm1, q_a_norm, kv_a_norm, pw, rope):
    B, S, D = x.shape
    tm = min(TM_IN, S)
    nt = S // tm
    r8 = tm // 8
    nqkv = DN_HEADS * (2 * DN_DK + DN_DV)
    nh = DN_HEADS * DN_DK

    def full(a):
        return pl.BlockSpec(a.shape, lambda b, i: (0,) * a.ndim)

    tok = lambda w: pl.BlockSpec((1, tm, w), lambda b, i: (b, i, 0))
    in_specs = [
        tok(D),
        pl.BlockSpec((1, 8, D), lambda b, i: (b, jnp.maximum(i * r8 - 1, 0), 0)),
        pl.BlockSpec((1, 8, D), lambda b, i: (b, jnp.minimum((i + 1) * r8, S // 8 - 1), 0)),
        pl.BlockSpec((1, 1, D), lambda b, i: (b, 0, 0)),
        pl.BlockSpec((1, 1, D), lambda b, i: (b, 0, 0)),
    ]
    consts = [norm1.reshape(1, D), pw["wqa"], pw["wkva"], pw["wkr"], pw["wqkv"], pw["wz"], pw["wab"], pw["wabt"],
              q_a_norm.reshape(1, -1), kv_a_norm.reshape(1, -1), pw["wqb"], pw["wkb"], pw["wvb"], pw["qn"], pw["kn"]]
    in_specs += [full(a) for a in consts]
    in_specs += [pl.BlockSpec((tm, LANES), lambda b, i: (i, 0))] * 3
    tail = [pw["conv"], pw["gpar"], pw["gpart"], _chunk_tri(tm, True), _chunk_tri(tm, False)]
    in_specs += [full(a) for a in tail]
    hspec = pl.BlockSpec((1, MLA_HEADS, tm, LANES), lambda b, i: (b, 0, i, 0))
    out_specs = [hspec, hspec, hspec, tok(nh), tok(nh), tok(nh),
                 pl.BlockSpec((1, nh, tm), lambda b, i: (b, 0, i)), tok(nh), tok(LANES),
                 pl.BlockSpec((1, 16, tm), lambda b, i: (b, 0, i))]
    hshape = jax.ShapeDtypeStruct((B, MLA_HEADS, S, LANES), bf16)
    tshape = jax.ShapeDtypeStruct((B, S, nh), f32)
    out_shape = [hshape, hshape, hshape, tshape, tshape, tshape,
                 jax.ShapeDtypeStruct((B, nh, S), f32), tshape,
                 jax.ShapeDtypeStruct((B, S, LANES), f32), jax.ShapeDtypeStruct((B, 16, S), f32)]
    return pl.pallas_call(
        _inproj_kernel,
        grid=(B, nt),
        in_specs=in_specs,
        out_specs=out_specs,
        out_shape=out_shape,
        scratch_shapes=[pltpu.VMEM((tm + 16, nqkv), f32)],
        compiler_params=_cparams(("parallel", "parallel")),
        name="inproj",
    )(x, x, x, sh1, sc1, *consts, *rope, *tail)


def _attn_kernel(q_ref, k_ref, v_ref, o_ref, s_scr, m_scr, acc_scr):
    S = k_ref.shape[2]
    tk = s_scr.shape[3]
    nk = S // tk
    nh = q_ref.shape[1]
    qs = [q_ref[0, hd] for hd in range(nh)]
    m_scr[...] = jnp.full(m_scr.shape, -jnp.inf, f32)
    acc_scr[...] = jnp.zeros(acc_scr.shape, f32)

    def scores(slot, j):
        for hd in range(nh):
            s_scr[slot, hd] = _dot_nt(qs[hd], k_ref[0, hd, pl.ds(j * tk, tk), :])

    def accumulate(slot, j):
        off = j * tk
        for hd in range(nh):
            s = s_scr[slot, hd]
            m = m_scr[hd]
            m_new = jnp.maximum(m, jnp.broadcast_to(jnp.max(s, axis=-1, keepdims=True), m.shape))
            p = jnp.exp2(s - jnp.tile(m_new, (1, tk // LANES)))
            m_scr[hd] = m_new
            acc_scr[hd] = acc_scr[hd] * jnp.exp2(m - m_new) + _dot(p.astype(bf16), v_ref[0, hd, pl.ds(off, tk), :])

    scores(0, 0)
    for j in range(nk):
        if j + 1 < nk:
            scores((j + 1) % 2, j + 1)
        accumulate(j % 2, j)
    outs = [acc_scr[hd][:, :MLA_V] / acc_scr[hd][:, MLA_V:MLA_V + 1] for hd in range(nh)]
    o_ref[0] = jnp.concatenate(outs, axis=1).astype(o_ref.dtype)


def _attention(q, k, v):
    B, H, S, _ = q.shape
    tq = min(TQ, S)
    tk = min(TK, S // 2)
    nh = 2
    qspec = pl.BlockSpec((1, nh, tq, LANES), lambda b, h, i: (b, h, i, 0))
    kspec = pl.BlockSpec((1, nh, S, LANES), lambda b, h, i: (b, h, 0, 0))
    return pl.pallas_call(
        _attn_kernel,
        grid=(B, H // nh, S // tq),
        in_specs=[qspec, kspec, kspec],
        out_specs=pl.BlockSpec((1, tq, LANES), lambda b, h, i: (b, i, h)),
        out_shape=jax.ShapeDtypeStruct((B, S, H * MLA_V), bf16),
        scratch_shapes=[pltpu.VMEM((2, nh, tq, tk), f32), pltpu.VMEM((nh, tq, LANES), f32),
                        pltpu.VMEM((nh, tq, LANES), f32)],
        compiler_params=_cparams(("parallel", "parallel", "arbitrary")),
        name="attn",
    )(q, k, v)


def _unit_tri_inverses(Ls, sub_mask):
    C = Ls[0].shape[0]
    r = lax.broadcasted_iota(i32, (C, C), 0)
    c = lax.broadcasted_iota(i32, (C, C), 1)
    diag = r == c

    def with_eye(s, sign):
        hi, lo = (s[0], s[1]) if sign > 0 else (-s[0], -s[1])
        return jnp.where(diag, jnp.ones((), bf16), hi), lo

    sLds = [_split2(jnp.where(sub_mask, L, 0.0)) for L in Ls]
    sLos = [_split2(jnp.where(sub_mask, 0.0, L)) for L in Ls]
    sTs = [with_eye(s, -1) for s in sLds]
    sPs = sLds
    n = 2
    while n < DN_SUB:
        sPs = [_split2(_dot3s(s, s)) for s in sPs]
        sTs = [_split2(_dot3s(sT, with_eye(sP, 1))) for sT, sP in zip(sTs, sPs)]
        n *= 2
    sNs = [_split2(_dot3s(sT, sLo)) for sT, sLo in zip(sTs, sLos)]
    sN2s = [_split2(_dot3s(s, s)) for s in sNs]
    sMs = [_split2(_dot3s(with_eye(sN, -1), with_eye(sN2, 1))) for sN, sN2 in zip(sNs, sN2s)]
    return [_dot3s(sM, sT) for sM, sT in zip(sMs, sTs)]


def _dot3s(sa, sb):
    lhs = jnp.concatenate([sa[0], sa[1], sa[0]], axis=1)
    rhs = jnp.concatenate([sb[0], sb[0], sb[1]], axis=0)
    return _dot(lhs, rhs)


def _dot3k(a, b):
    return _dot3s(_split2(a), _split2(b))


def _gdn_prep_kernel(q_ref, k_ref, v_ref, kt_ref, gcol_ref, grow_ref,
                     uf_ref, wf_ref, qf_ref, af_ref, ktf_ref, ub_ref, wb_ref, qb_ref, ab_ref, ktb_ref, dl_ref):
    C = DN_CHUNK
    nch = q_ref.shape[1] // C
    r = lax.broadcasted_iota(i32, (C, C), 0)
    c = lax.broadcasted_iota(i32, (C, C), 1)
    sub_mask = (r // DN_SUB) == (c // DN_SUB)
    outs = ((uf_ref, wf_ref, qf_ref, af_ref, ktf_ref), (ub_ref, wb_ref, qb_ref, ab_ref, ktb_ref))
    chains = [(ch, hd) for ch in range(nch) for hd in range(DN_HEADS)]
    kks, qks = {}, {}
    for ch, hd in chains:
        rs = slice(ch * C, (ch + 1) * C)
        hs = slice(hd * DN_DK, (hd + 1) * DN_DK)
        kbf = k_ref[0, rs, hs].astype(bf16)
        kks[ch, hd] = _dot_nt(kbf, kbf)
        qks[ch, hd] = _dot_nt(q_ref[0, rs, hs].astype(bf16), kbf)
    for ch in range(nch):
        rs = slice(ch * C, (ch + 1) * C)
        dls = []
        for d, reverse in enumerate((False, True)):
            for hd in range(DN_HEADS):
                gr = grow_ref[0, 4 * d + hd: 4 * d + hd + 1, rs]
                g_last = gr[:, 0:1] if reverse else gr[:, C - 1:C]
                dls.append(jnp.broadcast_to(jnp.exp(g_last), (1, LANES)))
        dl_ref[0, ch] = jnp.concatenate(dls, axis=0)

    full = [(ch, hd, d) for ch, hd in chains for d in range(2)]
    Ls, rhss = [], []
    for ch, hd, d in full:
        rs = slice(ch * C, (ch + 1) * C)
        hs = slice(hd * DN_DK, (hd + 1) * DN_DK)
        incl = (c >= r) if d else (c <= r)
        gc = gcol_ref[0, rs, 4 * d + hd: 4 * d + hd + 1]
        beta = gcol_ref[0, rs, 8 + 4 * d + hd: 8 + 4 * d + hd + 1]
        gr = grow_ref[0, 4 * d + hd: 4 * d + hd + 1, rs]
        dec = jnp.exp(jnp.where(incl, gc - gr, -jnp.inf))
        outs[d][3][0, hd, rs, :] = (qks[ch, hd] * dec).astype(bf16)
        Ls.append(jnp.where(r == c, 0.0, beta * kks[ch, hd] * dec))
        g_last = gr[:, 0:1] if d else gr[:, C - 1:C]
        outs[d][4][0, ch, hs, :] = (kt_ref[0, hs, rs] * jnp.exp(g_last - gr)).astype(bf16)
        eg = jnp.exp(gc)
        outs[d][2][0, rs, hs] = (q_ref[0, rs, hs] * eg).astype(bf16)
        rhss.append(jnp.concatenate([v_ref[0, rs, hs] * beta, k_ref[0, rs, hs] * (beta * eg)], axis=1))
    Ts = _unit_tri_inverses(Ls, sub_mask)
    sols = [_dot3k(T, rhs) for T, rhs in zip(Ts, rhss)]
    for (ch, hd, d), sol in zip(full, sols):
        rs = slice(ch * C, (ch + 1) * C)
        hs = slice(hd * DN_DK, (hd + 1) * DN_DK)
        outs[d][0][0, rs, hs] = sol[:, :DN_DV]
        outs[d][1][0, rs, hs] = sol[:, DN_DV:].astype(bf16)


def _gdn_scan_kernel(uf_ref, wf_ref, qf_ref, af_ref, ktf_ref, dlf_ref,
                     ub_ref, wb_ref, qb_ref, ab_ref, ktb_ref, dlb_ref, of_ref, ob_ref, s_ref):
    @pl.when(pl.program_id(1) == 0)
    def _():
        s_ref[...] = jnp.zeros_like(s_ref)

    C = DN_CHUNK
    nch = uf_ref.shape[1] // C
    dirs = ((uf_ref, wf_ref, qf_ref, af_ref, ktf_ref, dlf_ref, of_ref),
            (ub_ref, wb_ref, qb_ref, ab_ref, ktb_ref, dlb_ref, ob_ref))
    chains = [(d, hd) for d in range(2) for hd in range(DN_HEADS)]
    sts = [s_ref[d, hd] for d, hd in chains]
    for step in range(nch):
        def sl(d, hd):
            ch = step if d == 0 else nch - 1 - step
            return ch, slice(ch * C, (ch + 1) * C), slice(hd * DN_DK, (hd + 1) * DN_DK)

        wss = []
        for (d, hd), st in zip(chains, sts):
            ch, rs, hs = sl(d, hd)
            wq = jnp.concatenate([dirs[d][1][0, rs, hs], dirs[d][2][0, rs, hs]], axis=0)
            wss.append(_dot(wq, st.astype(bf16)))
        vnbs = []
        for (d, hd), ws in zip(chains, wss):
            ch, rs, hs = sl(d, hd)
            vnbs.append((dirs[d][0][0, rs, hs] - ws[:C]).astype(bf16))
        new = []
        for (d, hd), st, ws, vnb in zip(chains, sts, wss, vnbs):
            ch, rs, hs = sl(d, hd)
            dirs[d][6][0, rs, hs] = ws[C:] + _dot(dirs[d][3][0, hd, rs, :], vnb)
            dl = dirs[d][5][0, ch, d * DN_HEADS + hd: d * DN_HEADS + hd + 1, :]
            new.append(st * dl + _dot(dirs[d][4][0, ch, hs, :], vnb))
        sts = new
    for (d, hd), st in zip(chains, sts):
        s_ref[d, hd] = st


def _gdn(dq, dk, dv, dkt, gcol, grow):
    B, S, W = dq.shape
    C = DN_CHUNK
    rows = min(GDN_ROWS, S)
    n = S // rows
    nch = rows // C
    tok = pl.BlockSpec((1, rows, W), lambda b, i: (b, i, 0))
    aspec = pl.BlockSpec((1, DN_HEADS, rows, C), lambda b, i: (b, 0, i, 0))
    kspec = pl.BlockSpec((1, nch, W, C), lambda b, i: (b, i, 0, 0))
    dspec = pl.BlockSpec((1, nch, 2 * DN_HEADS, LANES), lambda b, i: (b, i, 0, 0))
    t32 = jax.ShapeDtypeStruct((B, S, W), f32)
    t16 = jax.ShapeDtypeStruct((B, S, W), bf16)
    ashape = jax.ShapeDtypeStruct((B, DN_HEADS, S, C), bf16)
    kshape = jax.ShapeDtypeStruct((B, S // C, W, C), bf16)
    per_dir_specs = [tok, tok, tok, aspec, kspec]
    per_dir_shapes = [t32, t16, t16, ashape, kshape]
    prep = pl.pallas_call(
        _gdn_prep_kernel,
        grid=(B, n),
        in_specs=[tok, tok, tok, pl.BlockSpec((1, W, rows), lambda b, i: (b, 0, i)),
                  pl.BlockSpec((1, rows, LANES), lambda b, i: (b, i, 0)),
                  pl.BlockSpec((1, 16, rows), lambda b, i: (b, 0, i))],
        out_specs=per_dir_specs * 2 + [dspec],
        out_shape=per_dir_shapes * 2 + [jax.ShapeDtypeStruct((B, S // C, 2 * DN_HEADS, LANES), f32)],
        compiler_params=_cparams(("parallel", "parallel")),
        name="gdn_prep",
    )(dq, dk, dv, dkt, gcol, grow)
    fwd, bwd, dl = prep[:5], prep[5:10], prep[10]

    rows_s = min(GDN_SCAN_ROWS, S)
    ns = S // rows_s
    nchs = rows_s // C

    def specs(rev):
        blk = (lambda i: ns - 1 - i) if rev else (lambda i: i)
        return [pl.BlockSpec((1, rows_s, W), lambda b, i: (b, blk(i), 0))] * 3 + [
            pl.BlockSpec((1, DN_HEADS, rows_s, C), lambda b, i: (b, 0, blk(i), 0)),
            pl.BlockSpec((1, nchs, W, C), lambda b, i: (b, blk(i), 0, 0)),
            pl.BlockSpec((1, nchs, 2 * DN_HEADS, LANES), lambda b, i: (b, blk(i), 0, 0))]

    return pl.pallas_call(
        _gdn_scan_kernel,
        grid=(B, ns),
        in_specs=specs(False) + specs(True),
        out_specs=[specs(False)[0], specs(True)[0]],
        out_shape=[t32, t32],
        scratch_shapes=[pltpu.VMEM((2, DN_HEADS, DN_DK, DN_DV), f32)],
        compiler_params=_cparams(("parallel", "arbitrary")),
        name="gdn_scan",
    )(*fwd, dl, *bwd, dl)


def _pack_bf16_pairs(y):
    w = y.shape[1] // 2
    lo = pltpu.bitcast(y[:, :w].astype(bf16).astype(f32), u32)
    hi = pltpu.bitcast(y[:, w:].astype(bf16).astype(f32), u32)
    return (lo >> 16) | (hi & jnp.uint32(0xFFFF0000))


def _unpack_bf16_pairs(p):
    lo = pltpu.bitcast(p << 16, f32)
    hi = pltpu.bitcast(p & jnp.uint32(0xFFFF0000), f32)
    return jnp.concatenate([lo, hi], axis=1)


ROW_SLABS = 4


def _store_rows(ref, packed):
    n = packed.shape[0]
    for j in range(ROW_SLABS):
        ref[pl.ds(j, n, stride=ROW_SLABS), :] = packed[:, j * LANES:(j + 1) * LANES]


def _load_rows(ref, n):
    return jnp.concatenate([ref[pl.ds(j, n, stride=ROW_SLABS), :] for j in range(ROW_SLABS)], axis=1)


def _outproj_kernel(x_ref, om_ref, of_ref, ob_ref, z_ref, gt_ref, sc_ref, sh_ref,
                    wo_ref, dnn_ref, n2_ref, wr_ref, br_ref, below_ref,
                    x1_ref, hp_ref, meta_ref, metat_ref, cnt_ref, carry_ref):
    first = (pl.program_id(0) == 0) & (pl.program_id(1) == 0)

    @pl.when(first)
    def _():
        carry_ref[...] = jnp.zeros_like(carry_ref)

    tm = x_ref.shape[1]
    o = of_ref[0] + ob_ref[0]
    z = z_ref[0]
    dnn = dnn_ref[...]
    parts = []
    for hd in range(DN_HEADS):
        sl = slice(hd * DN_DV, (hd + 1) * DN_DV)
        ob = o[:, sl]
        ob = ob * lax.rsqrt(jnp.mean(ob * ob, axis=-1, keepdims=True) + EPS) * dnn
        parts.append(ob * _silu(z[:, sl]))
    odn = jnp.concatenate(parts, axis=1).astype(bf16)
    nm = om_ref.shape[2]
    mixed = _dot(om_ref[0], wo_ref[pl.ds(0, nm), :]) + _dot(odn, wo_ref[pl.ds(nm, odn.shape[1]), :])
    x1 = x_ref[0] + gt_ref[0] * mixed
    x1_ref[0] = x1
    h2 = x1 * lax.rsqrt(jnp.mean(x1 * x1, axis=-1, keepdims=True) + EPS) * n2_ref[...]
    h2 = h2 * (1.0 + sc_ref[0]) + sh_ref[0]
    _store_rows(hp_ref, _pack_bf16_pairs(h2))

    hh, hl = _split2(h2)
    wr = wr_ref[...]
    p1 = _dot(hh, wr)
    p2 = _dot(hl, wr[:, :LANES])
    lane = lax.broadcasted_iota(i32, (tm, LANES), 1)
    logits = p1[:, :LANES] + p1[:, LANES:] + p2 + br_ref[...]
    logits = jnp.where(lane < N_EXPERTS, logits, -jnp.inf)
    vals, idxs = [], []
    work = logits
    for _ in range(TOP_K):
        mx = jnp.max(work, axis=-1, keepdims=True)
        ix = jnp.min(jnp.where(work == mx, lane, LANES), axis=-1, keepdims=True)
        vals.append(mx)
        idxs.append(ix)
        work = jnp.where(lane == ix, -jnp.inf, work)
    es = [jnp.exp(vv - vals[0]) for vv in vals]
    den = es[0] + es[1] + es[2] + es[3]
    multihot = jnp.where(work != logits, 1.0, 0.0)
    prefix = _dot(below_ref[...], multihot.astype(bf16)) + carry_ref[0:1]
    meta = jnp.zeros((tm, LANES), f32)
    for kk in range(TOP_K):
        rank = jnp.sum(jnp.where(lane == idxs[kk], prefix, 0.0), axis=-1, keepdims=True)
        meta = jnp.where(lane == kk, idxs[kk].astype(f32), meta)
        meta = jnp.where(lane == TOP_K + kk, rank, meta)
        meta = jnp.where(lane == 2 * TOP_K + kk, es[kk] / den, meta)
    meta_ref[...] = meta
    metat_ref[...] = meta.T[:metat_ref.shape[0]]
    carry = carry_ref[...] + jnp.sum(multihot, axis=0, keepdims=True)
    carry_ref[...] = carry
    cnt_ref[...] = carry


def _outproj(x, o_mla, o_f, o_b, z, gt1, sc2, sh2, w_o, dn_out_norm, norm2, w_router, b_router):
    B, S, D = x.shape
    tm = min(TM_OUT, S)
    nt = S // tm
    T = B * S
    nh = o_f.shape[2]
    wr_hi = w_router.astype(bf16)
    wr_lo = (w_router - wr_hi.astype(f32)).astype(bf16)
    zpad = jnp.zeros((D, LANES - N_EXPERTS), bf16)
    wr = jnp.concatenate([wr_hi, zpad, wr_lo, zpad], axis=1)
    br = jnp.pad(b_router, (0, LANES - N_EXPERTS)).reshape(1, LANES)
    below = jnp.tril(jnp.ones((tm, tm), bf16), -1)
    consts = [w_o.astype(bf16), dn_out_norm.reshape(1, -1), norm2.reshape(1, D), wr, br, below]

    def full(a):
        return pl.BlockSpec(a.shape, lambda b, i: (0,) * a.ndim)

    tok = lambda w: pl.BlockSpec((1, tm, w), lambda b, i: (b, i, 0))
    vec = pl.BlockSpec((1, 1, D), lambda b, i: (b, 0, 0))
    flat = lambda w: pl.BlockSpec((tm, w), lambda b, i: (b * nt + i, 0))
    return pl.pallas_call(
        _outproj_kernel,
        grid=(B, nt),
        in_specs=[tok(D), tok(o_mla.shape[2]), tok(nh), tok(nh), tok(nh), vec, vec, vec] + [full(a) for a in consts],
        out_specs=[tok(D), pl.BlockSpec((tm * ROW_SLABS, LANES), lambda b, i: (b * nt + i, 0)), flat(LANES),
                   pl.BlockSpec((2 * TOP_K, tm), lambda b, i: (0, b * nt + i)),
                   pl.BlockSpec((8, LANES), lambda b, i: (0, 0))],
        out_shape=[jax.ShapeDtypeStruct((B, S, D), f32), jax.ShapeDtypeStruct((T * ROW_SLABS, LANES), u32),
                   jax.ShapeDtypeStruct((T, LANES), f32), jax.ShapeDtypeStruct((2 * TOP_K, T), f32),
                   jax.ShapeDtypeStruct((8, LANES), f32)],
        scratch_shapes=[pltpu.VMEM((8, LANES), f32)],
        compiler_params=_cparams(("arbitrary", "arbitrary")),
        name="outproj",
    )(x, o_mla, o_f, o_b, z, gt1, sc2, sh2, *consts)


DMA_UNROLL = 2


def _row_copy(src_ref, s, dst_ref, d, sem):
    return pltpu.make_async_copy(src_ref.at[pl.ds(pl.multiple_of(s * ROW_SLABS, ROW_SLABS), ROW_SLABS)],
                                 dst_ref.at[pl.ds(pl.multiple_of(d * ROW_SLABS, ROW_SLABS), ROW_SLABS)], sem)


def _dispatch_kernel(ends_ref, dest_ref, hp_ref, xb_ref, zero_ref, sem, zsem):
    n = dest_ref.shape[0]

    @pl.when(pl.program_id(0) == 0)
    def _():
        zero_ref[...] = jnp.zeros_like(zero_ref)
        rows = zero_ref.shape[0]

        def zero_copy(e):
            start = pl.multiple_of(ends_ref[e] * ROW_SLABS - rows, ROW_SLABS * 8)
            return pltpu.make_async_copy(zero_ref, xb_ref.at[pl.ds(start, rows)], zsem)

        nb = xb_ref.shape[0] // rows
        n_used = ends_ref[N_EXPERTS]

        def tail_copy(j):
            return pltpu.make_async_copy(zero_ref, xb_ref.at[pl.ds(j * rows, rows)], zsem)

        for e in range(N_EXPERTS):
            @pl.when(ends_ref[e] >= 0)
            def _():
                zero_copy(e).start()
        for j in range(nb - N_EXPERTS, nb):
            @pl.when(j >= n_used)
            def _():
                tail_copy(j).start()
        for e in range(N_EXPERTS):
            @pl.when(ends_ref[e] >= 0)
            def _():
                zero_copy(e).wait()
        for j in range(nb - N_EXPERTS, nb):
            @pl.when(j >= n_used)
            def _():
                tail_copy(j).wait()

    def issue(t, carry):
        for kk in range(TOP_K):
            _row_copy(hp_ref, t, xb_ref, dest_ref[t * TOP_K + kk], sem).start(priority=kk % 2)
        return carry

    lax.fori_loop(0, n // TOP_K, issue, 0, unroll=DMA_UNROLL)
    for _ in range(TOP_K):
        pltpu.make_async_copy(hp_ref, xb_ref.at[pl.ds(0, hp_ref.shape[0])], sem).wait()


def _dispatch(hp, dest, ends, P):
    T = hp.shape[0] // ROW_SLABS
    tt = min(TT_DISPATCH, T)
    grid_spec = pltpu.PrefetchScalarGridSpec(
        num_scalar_prefetch=1,
        grid=(T // tt,),
        in_specs=[pl.BlockSpec((tt * TOP_K,), lambda i, ends: (i,), memory_space=pltpu.SMEM),
                  pl.BlockSpec((tt * ROW_SLABS, LANES), lambda i, ends: (i, 0))],
        out_specs=pl.BlockSpec(memory_space=pl.ANY),
        scratch_shapes=[pltpu.VMEM((MOE_BLOCK * ROW_SLABS, LANES), u32),
                        pltpu.SemaphoreType.DMA(()), pltpu.SemaphoreType.DMA(())],
    )
    return pl.pallas_call(
        _dispatch_kernel,
        grid_spec=grid_spec,
        out_shape=jax.ShapeDtypeStruct((P * ROW_SLABS, LANES), u32),
        compiler_params=pltpu.CompilerParams(dimension_semantics=("arbitrary",), has_side_effects=True),
        name="dispatch",
    )(ends, dest, hp)


def _expert_kernel(be_ref, nb_ref, x_ref, wg_ref, bg_ref, wu_ref, bu_ref, wd_ref, bd_ref, y_ref):
    b = pl.program_id(0)

    @pl.when(b < nb_ref[0])
    def _():
        subs = [pl.ds(h * MOE_SUB * ROW_SLABS, MOE_SUB * ROW_SLABS) for h in range(MOE_BLOCK // MOE_SUB)]
        xs = [_unpack_bf16_pairs(_load_rows(x_ref.at[sl], MOE_SUB)).astype(bf16) for sl in subs]
        gts = [jnp.minimum(_dot(x, wg_ref[0]) + bg_ref[0], SWIGLU_LIMIT) for x in xs]
        ups = [jnp.clip(_dot(x, wu_ref[0]) + bu_ref[0], -SWIGLU_LIMIT, SWIGLU_LIMIT) for x in xs]
        acts = [((up + 1.0) * gt * _sigmoid(SWIGLU_ALPHA * gt)).astype(bf16) for gt, up in zip(gts, ups)]
        ys = [_dot(act, wd_ref[0]) + bd_ref[0] for act in acts]
        for sl, y in zip(subs, ys):
            _store_rows(y_ref.at[sl], _pack_bf16_pairs(y))

    @pl.when(b >= nb_ref[0])
    def _():
        y_ref[...] = jnp.zeros_like(y_ref)


def _experts(xb, block_expert, n_used, wg, bg, wu, bu, wd, bd):
    E, D, F = wg.shape
    nb = xb.shape[0] // (MOE_BLOCK * ROW_SLABS)
    wspec = lambda r, c: pl.BlockSpec((1, r, c), lambda b, be, nu: (be[b], 0, 0))
    xspec = pl.BlockSpec((MOE_BLOCK * ROW_SLABS, LANES), lambda b, be, nu: (b, 0))
    grid_spec = pltpu.PrefetchScalarGridSpec(
        num_scalar_prefetch=2,
        grid=(nb,),
        in_specs=[xspec, wspec(D, F), wspec(1, F), wspec(D, F), wspec(1, F), wspec(F, D), wspec(1, D)],
        out_specs=xspec,
    )
    return pl.pallas_call(
        _expert_kernel,
        grid_spec=grid_spec,
        out_shape=jax.ShapeDtypeStruct(xb.shape, u32),
        compiler_params=_cparams(("arbitrary",)),
        name="experts",
    )(block_expert, n_used, xb, wg, bg.reshape(E, 1, F), wu, bu.reshape(E, 1, F), wd, bd.reshape(E, 1, D))


def _combine_kernel(dest_ref, dnext_ref, yb_ref, meta_ref, x1_ref, gt_ref, o_ref, buf_ref, sem):
    n = dest_ref.shape[0]
    tt = n // TOP_K
    g = pl.program_id(0)
    slot = g % 2

    def gather(idx_ref, s):
        def issue(t, carry):
            for kk in range(TOP_K):
                _row_copy(yb_ref, idx_ref[t * TOP_K + kk], buf_ref.at[s, kk], t, sem.at[s]).start(priority=kk % 2)
            return carry

        lax.fori_loop(0, tt, issue, 0, unroll=DMA_UNROLL)

    @pl.when(g == 0)
    def _():
        gather(dest_ref, slot)

    @pl.when(g + 1 < pl.num_programs(0))
    def _():
        gather(dnext_ref, 1 - slot)

    for kk in range(TOP_K):
        pltpu.make_async_copy(yb_ref.at[pl.ds(0, tt * ROW_SLABS)], buf_ref.at[slot, kk], sem.at[slot]).wait()
    meta = meta_ref[...]
    moe = jnp.zeros((tt, x1_ref.shape[2]), f32)
    for kk in range(TOP_K):
        gate = meta[:, 2 * TOP_K + kk: 2 * TOP_K + kk + 1]
        moe = moe + gate * _unpack_bf16_pairs(_load_rows(buf_ref.at[slot, kk], tt))
    o_ref[0] = x1_ref[0] + gt_ref[0] * moe


def _combine(yb, dest, meta, x1, gt2):
    B, S, D = x1.shape
    tt = min(TT_COMBINE, S)
    nt = S // tt
    ng = B * nt
    return pl.pallas_call(
        _combine_kernel,
        grid=(ng,),
        in_specs=[pl.BlockSpec((tt * TOP_K,), lambda g: (g,), memory_space=pltpu.SMEM),
                  pl.BlockSpec((tt * TOP_K,), lambda g: (jnp.minimum(g + 1, ng - 1),), memory_space=pltpu.SMEM),
                  pl.BlockSpec(memory_space=pl.ANY),
                  pl.BlockSpec((tt, LANES), lambda g: (g, 0)),
                  pl.BlockSpec((1, tt, D), lambda g: (g // nt, g % nt, 0)),
                  pl.BlockSpec((1, 1, D), lambda g: (g // nt, 0, 0))],
        out_specs=pl.BlockSpec((1, tt, D), lambda g: (g // nt, g % nt, 0)),
        out_shape=jax.ShapeDtypeStruct((B, S, D), f32),
        scratch_shapes=[pltpu.VMEM((2, TOP_K, tt * ROW_SLABS, LANES), u32), pltpu.SemaphoreType.DMA((2,))],
        compiler_params=_cparams(("arbitrary",)),
        name="combine",
    )(dest, dest, yb, meta, x1, gt2)


def _moe(hp, meta, meta_t, cnt, x1, gt2, ew):
    T = meta.shape[0]
    TK_ = T * TOP_K
    nb = -(-TK_ // MOE_BLOCK) + N_EXPERTS
    P = nb * MOE_BLOCK
    counts = cnt[0, :N_EXPERTS].astype(i32)
    padded = (counts + MOE_BLOCK - 1) // MOE_BLOCK * MOE_BLOCK
    cum_padded = jnp.cumsum(padded)
    pstart = cum_padded - padded
    top_idx = meta_t[:TOP_K].astype(i32)
    rank = meta_t[TOP_K:].astype(i32)
    dest = (pstart[top_idx] + rank).T.reshape(-1)
    block_start = jnp.arange(nb, dtype=i32) * MOE_BLOCK
    block_expert = jnp.minimum(jnp.sum((cum_padded[None, :] <= block_start[:, None]).astype(i32), axis=1),
                               N_EXPERTS - 1)
    n_used = (cum_padded[-1:] // MOE_BLOCK).astype(i32)
    ends = jnp.concatenate([jnp.where(padded > 0, cum_padded, -1).astype(i32), n_used])
    xb = _dispatch(hp, dest, ends, P)
    yb = _experts(xb, block_expert, n_used, *ew)
    return _combine(yb, dest, meta, x1, gt2)


def _layer(x, c, p, pw, ew, rope):
    B, S, D = x.shape
    mod = _ada(c, p["w_ada"], p["b_ada"]).reshape(B, 6, 1, D)
    sh1, sc1, gt1, sh2, sc2, gt2 = (mod[:, j] for j in range(6))
    q, k, v, dq, dk, dv, dkt, z, gcol, grow = _inproj(
        x, sh1, sc1, p["norm1"], p["q_a_norm"], p["kv_a_norm"], pw, rope)
    o_mla = _attention(q, k, v)
    o_f, o_b = _gdn(dq, dk, dv, dkt, gcol, grow)
    x1, hp, meta, meta_t, cnt = _outproj(x, o_mla, o_f, o_b, z, gt1, sc2, sh2, p["w_o"], p["dn_out_norm"],
                                         p["norm2"], p["w_router"], p["b_router"])
    return _moe(hp, meta, meta_t, cnt, x1, gt2, ew)


def kernel(x_prompt, x_sample, c_prompt, c_sample, w_ada, b_ada, norm1, w_in, q_a_norm, w_q_b, kv_a_norm, w_kv_b, q_norm, k_norm, dn_conv, dn_a_log, dn_dt_bias, dn_out_norm, w_o, norm2, w_router, b_router, w_gate, b_gate, w_up, b_up, w_down, b_down):
    y_prompt, y_sample = x_prompt, x_sample
    depth = w_ada.shape[0]
    for l in range(depth):
        p = {"w_ada": w_ada[l], "b_ada": b_ada[l], "norm1": norm1[l], "q_a_norm": q_a_norm[l],
             "kv_a_norm": kv_a_norm[l], "dn_out_norm": dn_out_norm[l], "w_o": w_o[l], "norm2": norm2[l],
             "w_router": w_router[l], "b_router": b_router[l]}
        pw = _prep_weights(w_in[l], w_q_b[l], w_kv_b[l], q_norm[l], k_norm[l], dn_conv[l], dn_a_log[l],
                           dn_dt_bias[l])
        ew = (w_gate[l].astype(bf16), b_gate[l], w_up[l].astype(bf16), b_up[l], w_down[l].astype(bf16), b_down[l])
        y_prompt = _layer(y_prompt, c_prompt, p, pw, ew, _rope_tables(y_prompt.shape[1]))
        y_sample = _layer(y_sample, c_sample, p, pw, ew, _rope_tables(y_sample.shape[1]))
    return (y_prompt, y_sample)
```

```python
import functools
import math

import jax
import jax.numpy as jnp
from jax import lax
from jax.experimental import pallas as pl
from jax.experimental.pallas import tpu as pltpu

f32 = jnp.float32
bf16 = jnp.bfloat16
u32 = jnp.uint32
i32 = jnp.int32

LANES = 128
VMEM_LIMIT = 56 * 1024 * 1024

MLA_HEADS = 8
MLA_Q_LORA = 384
MLA_KV_LORA = 256
MLA_NOPE = 64
MLA_ROPE = 32
MLA_QK = MLA_NOPE + MLA_ROPE
MLA_V = 64
ROPE_THETA = 10000.0
DN_HEADS = 4
DN_DK = 128
DN_DV = 128
DN_CONV = 5
DN_CHUNK = 64
DN_SUB = 16
N_EXPERTS = 32
TOP_K = 4
SWIGLU_LIMIT = 7.0
SWIGLU_ALPHA = 1.702
MOE_BLOCK = 512
MOE_SUB = 256
EPS = 1e-6

TM_IN = 256
TM_OUT = 256
TQ = 256
TK = 512
GDN_ROWS = 128
GDN_SCAN_ROWS = 512
TT_DISPATCH = 512
TT_COMBINE = 256


def _cparams(sem):
    return pltpu.CompilerParams(dimension_semantics=sem, vmem_limit_bytes=VMEM_LIMIT)


def _split2(x):
    hi = x.astype(bf16)
    lo = (x - hi.astype(f32)).astype(bf16)
    return hi, lo


def _split3(x):
    hi = x.astype(bf16)
    r = x - hi.astype(f32)
    mid = r.astype(bf16)
    lo = (r - mid.astype(f32)).astype(bf16)
    return hi, mid, lo


def _dot(a, b):
    return jnp.dot(a, b, preferred_element_type=f32)


def _dot_nt(a, b):
    return lax.dot_general(a, b, (((1,), (1,)), ((), ())), preferred_element_type=f32)


def _dot3(a, b):
    ah, al = _split2(a)
    bh, bl = _split2(b)
    return _dot(ah, bh) + (_dot(al, bh) + _dot(ah, bl))


def _sigmoid(x):
    return 1.0 / (1.0 + jnp.exp(-x))


def _silu(x):
    return x * _sigmoid(x)


def _softplus(x):
    return jnp.maximum(x, 0.0) + jnp.log(1.0 + jnp.exp(-jnp.abs(x)))


def _ada_kernel(c_ref, w_ref, b_ref, o_ref):
    c = c_ref[...]
    o_ref[...] = _dot3(_silu(c), w_ref[...]) + b_ref[...]


def _ada(c, w_ada, b_ada):
    B, D = c.shape
    N = w_ada.shape[1]
    cp = jnp.pad(c, ((0, 8 - B), (0, 0)))
    tn = 1024
    out = pl.pallas_call(
        _ada_kernel,
        grid=(N // tn,),
        in_specs=[
            pl.BlockSpec((8, D), lambda j: (0, 0)),
            pl.BlockSpec((D, tn), lambda j: (0, j)),
            pl.BlockSpec((1, tn), lambda j: (0, j)),
        ],
        out_specs=pl.BlockSpec((8, tn), lambda j: (0, j)),
        out_shape=jax.ShapeDtypeStruct((8, N), f32),
        compiler_params=_cparams(("parallel",)),
        name="ada",
    )(cp, w_ada, b_ada.reshape(1, N))
    return out[:B]


def _chunk_tri(n, lower):
    r = lax.broadcasted_iota(i32, (n, n), 0)
    c = lax.broadcasted_iota(i32, (n, n), 1)
    same = (r // DN_CHUNK) == (c // DN_CHUNK)
    tri = (c <= r) if lower else (c >= r)
    return jnp.where(same & tri, 1.0, 0.0).astype(bf16)


def _inproj_kernel(
    x_ref, xp_ref, xn_ref, sh_ref, sc_ref, n1_ref,
    wqa_ref, wkva_ref, wkr_ref, wqkv_ref, wz_ref, wab_ref, wabt_ref,
    qan_ref, kvan_ref, wqb_ref, wkb_ref, wvb_ref, qn_ref, kn_ref,
    cos_ref, sina_ref, sinb_ref, conv_ref, gpar_ref, gpart_ref, lo_ref, up_ref,
    q_ref, k_ref, v_ref, dq_ref, dk_ref, dv_ref, dkt_ref, z_ref, gcol_ref, grow_ref,
    ext_ref,
):
    i = pl.program_id(1)
    ni = pl.num_programs(1)
    tm = x_ref.shape[1]
    scale1 = 1.0 + sc_ref[0]
    shift1 = sh_ref[0]
    n1 = n1_ref[...]

    def modulate(xv):
        y = xv * lax.rsqrt(jnp.mean(xv * xv, axis=-1, keepdims=True) + EPS)
        return y * n1 * scale1 + shift1

    h = modulate(x_ref[0])
    hb = h.astype(bf16)
    hh = modulate(jnp.concatenate([xp_ref[0], xn_ref[0]], axis=0)).astype(bf16)

    qa = _dot(hb, wqa_ref[...])
    qa = qa * lax.rsqrt(jnp.mean(qa * qa, axis=-1, keepdims=True) + EPS) * qan_ref[...]
    kva = _dot(hb, wkva_ref[...])
    kva = kva * lax.rsqrt(jnp.mean(kva * kva, axis=-1, keepdims=True) + EPS) * kvan_ref[...]
    kr = _dot(hb, wkr_ref[...])
    qh = _dot(qa.astype(bf16), wqb_ref[...])
    kvb = kva.astype(bf16)
    kh = _dot(kvb, wkb_ref[...])
    vh = _dot(kvb, wvb_ref[...])
    cos = cos_ref[...]
    sina = sina_ref[...]
    sinb = sinb_ref[...]
    qg = qn_ref[...]
    kg = kn_ref[...]
    lane = lax.broadcasted_iota(i32, (tm, LANES), 1)
    q_scale = MLA_QK ** -0.5 * math.log2(math.e)

    def norm_rope(blk, gain):
        ss = jnp.sum(blk * blk, axis=-1, keepdims=True) * (1.0 / MLA_QK)
        y = blk * lax.rsqrt(ss + EPS) * gain
        return y * cos + pltpu.roll(y, LANES - MLA_ROPE // 2, 1) * sina + pltpu.roll(y, MLA_ROPE // 2, 1) * sinb

    for hd in range(MLA_HEADS):
        sl = slice(hd * LANES, (hd + 1) * LANES)
        q_ref[0, hd] = (norm_rope(qh[:, sl], qg) * q_scale).astype(bf16)
        k_ref[0, hd] = norm_rope(kh[:, sl] + kr, kg).astype(bf16)
        vblk = vh[:, hd * MLA_V:(hd + 1) * MLA_V]
        vpad = jnp.concatenate([vblk, jnp.zeros((tm, LANES - MLA_V), f32)], axis=1)
        v_ref[0, hd] = jnp.where(lane == MLA_V, 1.0, vpad).astype(bf16)

    ext_ref[pl.ds(8, tm), :] = _dot(hb, wqkv_ref[...])
    halo = _dot(hh, wqkv_ref[...])
    ext_ref[pl.ds(0, 8), :] = jnp.where(i == 0, 0.0, halo[:8])
    ext_ref[pl.ds(8 + tm, 8), :] = jnp.where(i == ni - 1, 0.0, halo[8:])
    pad = (DN_CONV - 1) // 2
    cw = conv_ref[...]
    acc = ext_ref[pl.ds(8 - pad, tm), :] * cw[0:1]
    for j in range(1, DN_CONV):
        acc = acc + ext_ref[pl.ds(8 - pad + j, tm), :] * cw[j:j + 1]
    act = _silu(acc)
    nqk = DN_HEADS * DN_DK
    for hd in range(DN_HEADS):
        sl = slice(hd * DN_DK, (hd + 1) * DN_DK)
        qb = act[:, sl]
        qb = qb * lax.rsqrt(jnp.sum(qb * qb, axis=-1, keepdims=True) + EPS) * (DN_DK ** -0.5)
        dq_ref[0, :, sl] = qb
        kb = act[:, nqk + hd * DN_DK: nqk + (hd + 1) * DN_DK]
        kb = kb * lax.rsqrt(jnp.sum(kb * kb, axis=-1, keepdims=True) + EPS)
        dk_ref[0, :, sl] = kb
        dkt_ref[0, sl, :] = kb.T
    dv_ref[0] = act[:, 2 * nqk:]
    z_ref[0] = _dot(hb, wz_ref[...])

    hlo = (h - hb.astype(f32)).astype(bf16)
    wab = wab_ref[...]
    p1 = _dot(hb, wab)
    p2 = _dot(hlo, wab)
    ab = p1 + pltpu.roll(p1, LANES - 16, 1) + p2
    gpar = gpar_ref[...]
    lane16 = lax.broadcasted_iota(i32, (tm, LANES), 1)
    gval = jnp.where(lane16 < 8, -gpar[0:1] * _softplus(ab + gpar[1:2]), _sigmoid(ab))
    g3 = _split3(gval)
    lo_tri = lo_ref[...]
    up_tri = up_ref[...]
    pre = _dot(lo_tri, g3[0]) + (_dot(lo_tri, g3[1]) + _dot(lo_tri, g3[2]))
    suf = _dot(up_tri, g3[0]) + (_dot(up_tri, g3[1]) + _dot(up_tri, g3[2]))
    gcol_ref[0] = jnp.where(lane16 < 4, pre, jnp.where(lane16 < 8, suf, gval))

    wabt = wabt_ref[...]
    r1 = _dot_nt(wabt, hb)
    r2 = _dot_nt(wabt[:16], hlo)
    abt = r1[:16] + r1[16:] + r2
    gpt = gpart_ref[...]
    row16 = lax.broadcasted_iota(i32, (16, tm), 0)
    gvt = jnp.where(row16 < 8, -gpt[:, 0:1] * _softplus(abt + gpt[:, 1:2]), _sigmoid(abt))
    t3 = _split3(gvt)
    pre_t = _dot(t3[0], up_tri) + (_dot(t3[1], up_tri) + _dot(t3[2], up_tri))
    suf_t = _dot(t3[0], lo_tri) + (_dot(t3[1], lo_tri) + _dot(t3[2], lo_tri))
    grow_ref[0] = jnp.where(row16 < 4, pre_t, jnp.where(row16 < 8, suf_t, gvt))


def _prep_weights(w_in, w_q_b, w_kv_b, q_norm, k_norm, dn_conv, dn_a_log, dn_dt_bias):
    D = w_in.shape[0]
    i0 = MLA_Q_LORA
    i1 = i0 + MLA_KV_LORA
    i2 = i1 + MLA_ROPE
    nqkv = DN_HEADS * (2 * DN_DK + DN_DV)
    i3 = i2 + nqkv
    i4 = i3 + DN_HEADS * DN_DV
    wqa = w_in[:, :i0].astype(bf16)
    wkva = w_in[:, i0:i1].astype(bf16)
    wkr = jnp.pad(w_in[:, i1:i2], ((0, 0), (MLA_NOPE, LANES - MLA_QK))).astype(bf16)
    wqkv = w_in[:, i2:i3].astype(bf16)
    wz = w_in[:, i3:i4].astype(bf16)
    wab_f = w_in[:, i4:]
    wab_hi = wab_f.astype(bf16)
    wab_lo = (wab_f - wab_hi.astype(f32)).astype(bf16)
    wab = jnp.pad(jnp.concatenate([wab_hi, wab_lo], axis=1), ((0, 0), (0, LANES - 32)))
    wabt = jnp.concatenate([wab_hi.T, wab_lo.T], axis=0)
    wqb = w_q_b.reshape(MLA_Q_LORA, MLA_HEADS, MLA_QK)
    wqb = jnp.pad(wqb, ((0, 0), (0, 0), (0, LANES - MLA_QK))).reshape(MLA_Q_LORA, MLA_HEADS * LANES).astype(bf16)
    wkv = w_kv_b.reshape(MLA_KV_LORA, MLA_HEADS, MLA_NOPE + MLA_V)
    wkb = jnp.pad(wkv[:, :, :MLA_NOPE], ((0, 0), (0, 0), (0, LANES - MLA_NOPE)))
    wkb = wkb.reshape(MLA_KV_LORA, MLA_HEADS * LANES).astype(bf16)
    wvb = wkv[:, :, MLA_NOPE:].reshape(MLA_KV_LORA, MLA_HEADS * MLA_V).astype(bf16)
    qn = jnp.pad(q_norm, (0, LANES - MLA_QK)).reshape(1, LANES)
    kn = jnp.pad(k_norm, (0, LANES - MLA_QK)).reshape(1, LANES)
    conv = jnp.pad(dn_conv, ((0, 8 - DN_CONV), (0, 0)))
    ea = jnp.exp(dn_a_log.astype(f32)).reshape(-1)
    dtb = dn_dt_bias.astype(f32).reshape(-1)
    gpar = jnp.pad(jnp.stack([ea, dtb]), ((0, 6), (0, LANES - 8)))
    gpart = jnp.pad(jnp.stack([ea, dtb], axis=1), ((0, 8), (0, LANES - 2)))
    return dict(wqa=wqa, wkva=wkva, wkr=wkr, wqkv=wqkv, wz=wz, wab=wab, wabt=wabt, wqb=wqb, wkb=wkb,
                wvb=wvb, qn=qn, kn=kn, conv=conv, gpar=gpar, gpart=gpart)


def _rope_tables(S):
    half = MLA_ROPE // 2
    freq = ROPE_THETA ** (-jnp.arange(half, dtype=f32) / half)
    ang = jnp.arange(S, dtype=f32)[:, None] * freq[None, :]
    cos, sin = jnp.cos(ang), jnp.sin(ang)
    zeros = lambda w: jnp.zeros((S, w), f32)
    cos_t = jnp.concatenate([jnp.ones((S, MLA_NOPE), f32), cos, cos, zeros(LANES - MLA_QK)], axis=1)
    sina = jnp.concatenate([zeros(MLA_NOPE), -sin, zeros(LANES - MLA_NOPE - half)], axis=1)
    sinb = jnp.concatenate([zeros(MLA_NOPE + half), sin, zeros(LANES - MLA_QK)], axis=1)
    return cos_t, sina, sinb


def _inproj(x, sh1, sc1, norm1, q_a_norm, kv_a_norm, pw, rope):
    B, S, D = x.shape
    tm = min(TM_IN, S)
    nt = S // tm
    r8 = tm // 8
    nqkv = DN_HEADS * (2 * DN_DK + DN_DV)
    nh = DN_HEADS * DN_DK

    def full(a):
        return pl.BlockSpec(a.shape, lambda b, i: (0,) * a.ndim)

    tok = lambda w: pl.BlockSpec((1, tm, w), lambda b, i: (b, i, 0))
    in_specs = [
        tok(D),
        pl.BlockSpec((1, 8, D), lambda b, i: (b, jnp.maximum(i * r8 - 1, 0), 0)),
        pl.BlockSpec((1, 8, D), lambda b, i: (b, jnp.minimum((i + 1) * r8, S // 8 - 1), 0)),
        pl.BlockSpec((1, 1, D), lambda b, i: (b, 0, 0)),
        pl.BlockSpec((1, 1, D), lambda b, i: (b, 0, 0)),
    ]
    consts = [norm1.reshape(1, D), pw["wqa"], pw["wkva"], pw["wkr"], pw["wqkv"], pw["wz"], pw["wab"], pw["wabt"],
              q_a_norm.reshape(1, -1), kv_a_norm.reshape(1, -1), pw["wqb"], pw["wkb"], pw["wvb"], pw["qn"], pw["kn"]]
    in_specs += [full(a) for a in consts]
    in_specs += [pl.BlockSpec((tm, LANES), lambda b, i: (i, 0))] * 3
    tail = [pw["conv"], pw["gpar"], pw["gpart"], _chunk_tri(tm, True), _chunk_tri(tm, False)]
    in_specs += [full(a) for a in tail]
    hspec = pl.BlockSpec((1, MLA_HEADS, tm, LANES), lambda b, i: (b, 0, i, 0))
    out_specs = [hspec, hspec, hspec, tok(nh), tok(nh), tok(nh),
                 pl.BlockSpec((1, nh, tm), lambda b, i: (b, 0, i)), tok(nh), tok(LANES),
                 pl.BlockSpec((1, 16, tm), lambda b, i: (b, 0, i))]
    hshape = jax.ShapeDtypeStruct((B, MLA_HEADS, S, LANES), bf16)
    tshape = jax.ShapeDtypeStruct((B, S, nh), f32)
    out_shape = [hshape, hshape, hshape, tshape, tshape, tshape,
                 jax.ShapeDtypeStruct((B, nh, S), f32), tshape,
                 jax.ShapeDtypeStruct((B, S, LANES), f32), jax.ShapeDtypeStruct((B, 16, S), f32)]
    return pl.pallas_call(
        _inproj_kernel,
        grid=(B, nt),
        in_specs=in_specs,
        out_specs=out_specs,
        out_shape=out_shape,
        scratch_shapes=[pltpu.VMEM((tm + 16, nqkv), f32)],
        compiler_params=_cparams(("parallel", "parallel")),
        name="inproj",
    )(x, x, x, sh1, sc1, *consts, *rope, *tail)


def _attn_kernel(q_ref, k_ref, v_ref, o_ref, s_scr, m_scr, acc_scr):
    S = k_ref.shape[2]
    tk = s_scr.shape[3]
    nk = S // tk
    nh = q_ref.shape[1]
    qs = [q_ref[0, hd] for hd in range(nh)]
    m_scr[...] = jnp.full(m_scr.shape, -jnp.inf, f32)
    acc_scr[...] = jnp.zeros(acc_scr.shape, f32)

    def scores(slot, j):
        for hd in range(nh):
            s_scr[slot, hd] = _dot_nt(qs[hd], k_ref[0, hd, pl.ds(j * tk, tk), :])

    def accumulate(slot, j):
        off = j * tk
        for hd in range(nh):
            s = s_scr[slot, hd]
            m = m_scr[hd]
            m_new = jnp.maximum(m, jnp.broadcast_to(jnp.max(s, axis=-1, keepdims=True), m.shape))
            p = jnp.exp2(s - jnp.tile(m_new, (1, tk // LANES)))
            m_scr[hd] = m_new
            acc_scr[hd] = acc_scr[hd] * jnp.exp2(m - m_new) + _dot(p.astype(bf16), v_ref[0, hd, pl.ds(off, tk), :])

    scores(0, 0)
    for j in range(nk):
        if j + 1 < nk:
            scores((j + 1) % 2, j + 1)
        accumulate(j % 2, j)
    outs = [acc_scr[hd][:, :MLA_V] / acc_scr[hd][:, MLA_V:MLA_V + 1] for hd in range(nh)]
    o_ref[0] = jnp.concatenate(outs, axis=1).astype(o_ref.dtype)


def _attention(q, k, v):
    B, H, S, _ = q.shape
    tq = min(TQ, S)
    tk = min(TK, S // 2)
    nh = 2
    qspec = pl.BlockSpec((1, nh, tq, LANES), lambda b, h, i: (b, h, i, 0))
    kspec = pl.BlockSpec((1, nh, S, LANES), lambda b, h, i: (b, h, 0, 0))
    return pl.pallas_call(
        _attn_kernel,
        grid=(B, H // nh, S // tq),
        in_specs=[qspec, kspec, kspec],
        out_specs=pl.BlockSpec((1, tq, LANES), lambda b, h, i: (b, i, h)),
        out_shape=jax.ShapeDtypeStruct((B, S, H * MLA_V), bf16),
        scratch_shapes=[pltpu.VMEM((2, nh, tq, tk), f32), pltpu.VMEM((nh, tq, LANES), f32),
                        pltpu.VMEM((nh, tq, LANES), f32)],
        compiler_params=_cparams(("parallel", "parallel", "arbitrary")),
        name="attn",
    )(q, k, v)


def _unit_tri_inverses(Ls, sub_mask):
    C = Ls[0].shape[0]
    r = lax.broadcasted_iota(i32, (C, C), 0)
    c = lax.broadcasted_iota(i32, (C, C), 1)
    diag = r == c

    def with_eye(s, sign):
        hi, lo = (s[0], s[1]) if sign > 0 else (-s[0], -s[1])
        return jnp.where(diag, jnp.ones((), bf16), hi), lo

    sLds = [_split2(jnp.where(sub_mask, L, 0.0)) for L in Ls]
    sLos = [_split2(jnp.where(sub_mask, 0.0, L)) for L in Ls]
    sTs = [with_eye(s, -1) for s in sLds]
    sPs = sLds
    n = 2
    while n < DN_SUB:
        sPs = [_split2(_dot3s(s, s)) for s in sPs]
        sTs = [_split2(_dot3s(sT, with_eye(sP, 1))) for sT, sP in zip(sTs, sPs)]
        n *= 2
    sNs = [_split2(_dot3s(sT, sLo)) for sT, sLo in zip(sTs, sLos)]
    sN2s = [_split2(_dot3s(s, s)) for s in sNs]
    sMs = [_split2(_dot3s(with_eye(sN, -1), with_eye(sN2, 1))) for sN, sN2 in zip(sNs, sN2s)]
    return [_dot3s(sM, sT) for sM, sT in zip(sMs, sTs)]


def _dot3s(sa, sb):
    lhs = jnp.concatenate([sa[0], sa[1], sa[0]], axis=1)
    rhs = jnp.concatenate([sb[0], sb[0], sb[1]], axis=0)
    return _dot(lhs, rhs)


def _dot3k(a, b):
    return _dot3s(_split2(a), _split2(b))


def _gdn_prep_kernel(q_ref, k_ref, v_ref, kt_ref, gcol_ref, grow_ref,
                     uf_ref, wf_ref, qf_ref, af_ref, ktf_ref, ub_ref, wb_ref, qb_ref, ab_ref, ktb_ref, dl_ref):
    C = DN_CHUNK
    nch = q_ref.shape[1] // C
    r = lax.broadcasted_iota(i32, (C, C), 0)
    c = lax.broadcasted_iota(i32, (C, C), 1)
    sub_mask = (r // DN_SUB) == (c // DN_SUB)
    outs = ((uf_ref, wf_ref, qf_ref, af_ref, ktf_ref), (ub_ref, wb_ref, qb_ref, ab_ref, ktb_ref))
    chains = [(ch, hd) for ch in range(nch) for hd in range(DN_HEADS)]
    kks, qks = {}, {}
    for ch, hd in chains:
        rs = slice(ch * C, (ch + 1) * C)
        hs = slice(hd * DN_DK, (hd + 1) * DN_DK)
        kbf = k_ref[0, rs, hs].astype(bf16)
        kks[ch, hd] = _dot_nt(kbf, kbf)
        qks[ch, hd] = _dot_nt(q_ref[0, rs, hs].astype(bf16), kbf)
    for ch in range(nch):
        rs = slice(ch * C, (ch + 1) * C)
        dls = []
        for d, reverse in enumerate((False, True)):
            for hd in range(DN_HEADS):
                gr = grow_ref[0, 4 * d + hd: 4 * d + hd + 1, rs]
                g_last = gr[:, 0:1] if reverse else gr[:, C - 1:C]
                dls.append(jnp.broadcast_to(jnp.exp(g_last), (1, LANES)))
        dl_ref[0, ch] = jnp.concatenate(dls, axis=0)

    full = [(ch, hd, d) for ch, hd in chains for d in range(2)]
    Ls, rhss = [], []
    for ch, hd, d in full:
        rs = slice(ch * C, (ch + 1) * C)
        hs = slice(hd * DN_DK, (hd + 1) * DN_DK)
        incl = (c >= r) if d else (c <= r)
        gc = gcol_ref[0, rs, 4 * d + hd: 4 * d + hd + 1]
        beta = gcol_ref[0, rs, 8 + 4 * d + hd: 8 + 4 * d + hd + 1]
        gr = grow_ref[0, 4 * d + hd: 4 * d + hd + 1, rs]
        dec = jnp.exp(jnp.where(incl, gc - gr, -jnp.inf))
        outs[d][3][0, hd, rs, :] = (qks[ch, hd] * dec).astype(bf16)
        Ls.append(jnp.where(r == c, 0.0, beta * kks[ch, hd] * dec))
        g_last = gr[:, 0:1] if d else gr[:, C - 1:C]
        outs[d][4][0, ch, hs, :] = (kt_ref[0, hs, rs] * jnp.exp(g_last - gr)).astype(bf16)
        eg = jnp.exp(gc)
        outs[d][2][0, rs, hs] = (q_ref[0, rs, hs] * eg).astype(bf16)
        rhss.append(jnp.concatenate([v_ref[0, rs, hs] * beta, k_ref[0, rs, hs] * (beta * eg)], axis=1))
    Ts = _unit_tri_inverses(Ls, sub_mask)
    sols = [_dot3k(T, rhs) for T, rhs in zip(Ts, rhss)]
    for (ch, hd, d), sol in zip(full, sols):
        rs = slice(ch * C, (ch + 1) * C)
        hs = slice(hd * DN_DK, (hd + 1) * DN_DK)
        outs[d][0][0, rs, hs] = sol[:, :DN_DV]
        outs[d][1][0, rs, hs] = sol[:, DN_DV:].astype(bf16)


def _gdn_scan_kernel(uf_ref, wf_ref, qf_ref, af_ref, ktf_ref, dlf_ref,
                     ub_ref, wb_ref, qb_ref, ab_ref, ktb_ref, dlb_ref, of_ref, ob_ref, s_ref):
    @pl.when(pl.program_id(1) == 0)
    def _():
        s_ref[...] = jnp.zeros_like(s_ref)

    C = DN_CHUNK
    nch = uf_ref.shape[1] // C
    dirs = ((uf_ref, wf_ref, qf_ref, af_ref, ktf_ref, dlf_ref, of_ref),
            (ub_ref, wb_ref, qb_ref, ab_ref, ktb_ref, dlb_ref, ob_ref))
    chains = [(d, hd) for d in range(2) for hd in range(DN_HEADS)]
    sts = [s_ref[d, hd] for d, hd in chains]
    for step in range(nch):
        def sl(d, hd):
            ch = step if d == 0 else nch - 1 - step
            return ch, slice(ch * C, (ch + 1) * C), slice(hd * DN_DK, (hd + 1) * DN_DK)

        wss = []
        for (d, hd), st in zip(chains, sts):
            ch, rs, hs = sl(d, hd)
            wq = jnp.concatenate([dirs[d][1][0, rs, hs], dirs[d][2][0, rs, hs]], axis=0)
            wss.append(_dot(wq, st.astype(bf16)))
        vnbs = []
        for (d, hd), ws in zip(chains, wss):
            ch, rs, hs = sl(d, hd)
            vnbs.append((dirs[d][0][0, rs, hs] - ws[:C]).astype(bf16))
        new = []
        for (d, hd), st, ws, vnb in zip(chains, sts, wss, vnbs):
            ch, rs, hs = sl(d, hd)
            dirs[d][6][0, rs, hs] = ws[C:] + _dot(dirs[d][3][0, hd, rs, :], vnb)
            dl = dirs[d][5][0, ch, d * DN_HEADS + hd: d * DN_HEADS + hd + 1, :]
            new.append(st * dl + _dot(dirs[d][4][0, ch, hs, :], vnb))
        sts = new
    for (d, hd), st in zip(chains, sts):
        s_ref[d, hd] = st


def _gdn(dq, dk, dv, dkt, gcol, grow):
    B, S, W = dq.shape
    C = DN_CHUNK
    rows = min(GDN_ROWS, S)
    n = S // rows
    nch = rows // C
    tok = pl.BlockSpec((1, rows, W), lambda b, i: (b, i, 0))
    aspec = pl.BlockSpec((1, DN_HEADS, rows, C), lambda b, i: (b, 0, i, 0))
    kspec = pl.BlockSpec((1, nch, W, C), lambda b, i: (b, i, 0, 0))
    dspec = pl.BlockSpec((1, nch, 2 * DN_HEADS, LANES), lambda b, i: (b, i, 0, 0))
    t32 = jax.ShapeDtypeStruct((B, S, W), f32)
    t16 = jax.ShapeDtypeStruct((B, S, W), bf16)
    ashape = jax.ShapeDtypeStruct((B, DN_HEADS, S, C), bf16)
    kshape = jax.ShapeDtypeStruct((B, S // C, W, C), bf16)
    per_dir_specs = [tok, tok, tok, aspec, kspec]
    per_dir_shapes = [t32, t16, t16, ashape, kshape]
    prep = pl.pallas_call(
        _gdn_prep_kernel,
        grid=(B, n),
        in_specs=[tok, tok, tok, pl.BlockSpec((1, W, rows), lambda b, i: (b, 0, i)),
                  pl.BlockSpec((1, rows, LANES), lambda b, i: (b, i, 0)),
                  pl.BlockSpec((1, 16, rows), lambda b, i: (b, 0, i))],
        out_specs=per_dir_specs * 2 + [dspec],
        out_shape=per_dir_shapes * 2 + [jax.ShapeDtypeStruct((B, S // C, 2 * DN_HEADS, LANES), f32)],
        compiler_params=_cparams(("parallel", "parallel")),
        name="gdn_prep",
    )(dq, dk, dv, dkt, gcol, grow)
    fwd, bwd, dl = prep[:5], prep[5:10], prep[10]

    rows_s = min(GDN_SCAN_ROWS, S)
    ns = S // rows_s
    nchs = rows_s // C

    def specs(rev):
        blk = (lambda i: ns - 1 - i) if rev else (lambda i: i)
        return [pl.BlockSpec((1, rows_s, W), lambda b, i: (b, blk(i), 0))] * 3 + [
            pl.BlockSpec((1, DN_HEADS, rows_s, C), lambda b, i: (b, 0, blk(i), 0)),
            pl.BlockSpec((1, nchs, W, C), lambda b, i: (b, blk(i), 0, 0)),
            pl.BlockSpec((1, nchs, 2 * DN_HEADS, LANES), lambda b, i: (b, blk(i), 0, 0))]

    return pl.pallas_call(
        _gdn_scan_kernel,
        grid=(B, ns),
        in_specs=specs(False) + specs(True),
        out_specs=[specs(False)[0], specs(True)[0]],
        out_shape=[t32, t32],
        scratch_shapes=[pltpu.VMEM((2, DN_HEADS, DN_DK, DN_DV), f32)],
        compiler_params=_cparams(("parallel", "arbitrary")),
        name="gdn_scan",
    )(*fwd, dl, *bwd, dl)


def _pack_bf16_pairs(y):
    w = y.shape[1] // 2
    lo = pltpu.bitcast(y[:, :w].astype(bf16).astype(f32), u32)
    hi = pltpu.bitcast(y[:, w:].astype(bf16).astype(f32), u32)
    return (lo >> 16) | (hi & jnp.uint32(0xFFFF0000))


def _unpack_bf16_pairs(p):
    lo = pltpu.bitcast(p << 16, f32)
    hi = pltpu.bitcast(p & jnp.uint32(0xFFFF0000), f32)
    return jnp.concatenate([lo, hi], axis=1)


ROW_SLABS = 4


def _store_rows(ref, packed):
    n = packed.shape[0]
    for j in range(ROW_SLABS):
        ref[pl.ds(j, n, stride=ROW_SLABS), :] = packed[:, j * LANES:(j + 1) * LANES]


def _load_rows(ref, n):
    return jnp.concatenate([ref[pl.ds(j, n, stride=ROW_SLABS), :] for j in range(ROW_SLABS)], axis=1)


def _outproj_kernel(x_ref, om_ref, of_ref, ob_ref, z_ref, gt_ref, sc_ref, sh_ref,
                    wo_ref, dnn_ref, n2_ref, wr_ref, br_ref, below_ref,
                    x1_ref, hp_ref, meta_ref, metat_ref, cnt_ref, carry_ref):
    first = (pl.program_id(0) == 0) & (pl.program_id(1) == 0)

    @pl.when(first)
    def _():
        carry_ref[...] = jnp.zeros_like(carry_ref)

    tm = x_ref.shape[1]
    o = of_ref[0] + ob_ref[0]
    z = z_ref[0]
    dnn = dnn_ref[...]
    parts = []
    for hd in range(DN_HEADS):
        sl = slice(hd * DN_DV, (hd + 1) * DN_DV)
        ob = o[:, sl]
        ob = ob * lax.rsqrt(jnp.mean(ob * ob, axis=-1, keepdims=True) + EPS) * dnn
        parts.append(ob * _silu(z[:, sl]))
    odn = jnp.concatenate(parts, axis=1).astype(bf16)
    nm = om_ref.shape[2]
    mixed = _dot(om_ref[0], wo_ref[pl.ds(0, nm), :]) + _dot(odn, wo_ref[pl.ds(nm, odn.shape[1]), :])
    x1 = x_ref[0] + gt_ref[0] * mixed
    x1_ref[0] = x1
    h2 = x1 * lax.rsqrt(jnp.mean(x1 * x1, axis=-1, keepdims=True) + EPS) * n2_ref[...]
    h2 = h2 * (1.0 + sc_ref[0]) + sh_ref[0]
    _store_rows(hp_ref, _pack_bf16_pairs(h2))

    hh, hl = _split2(h2)
    wr = wr_ref[...]
    p1 = _dot(hh, wr)
    p2 = _dot(hl, wr[:, :LANES])
    lane = lax.broadcasted_iota(i32, (tm, LANES), 1)
    logits = p1[:, :LANES] + p1[:, LANES:] + p2 + br_ref[...]
    logits = jnp.where(lane < N_EXPERTS, logits, -jnp.inf)
    vals, idxs = [], []
    work = logits
    for _ in range(TOP_K):
        mx = jnp.max(work, axis=-1, keepdims=True)
        ix = jnp.min(jnp.where(work == mx, lane, LANES), axis=-1, keepdims=True)
        vals.append(mx)
        idxs.append(ix)
        work = jnp.where(lane == ix, -jnp.inf, work)
    es = [jnp.exp(vv - vals[0]) for vv in vals]
    den = es[0] + es[1] + es[2] + es[3]
    multihot = jnp.where(work != logits, 1.0, 0.0)
    prefix = _dot(below_ref[...], multihot.astype(bf16)) + carry_ref[0:1]
    meta = jnp.zeros((tm, LANES), f32)
    for kk in range(TOP_K):
        rank = jnp.sum(jnp.where(lane == idxs[kk], prefix, 0.0), axis=-1, keepdims=True)
        meta = jnp.where(lane == kk, idxs[kk].astype(f32), meta)
        meta = jnp.where(lane == TOP_K + kk, rank, meta)
        meta = jnp.where(lane == 2 * TOP_K + kk, es[kk] / den, meta)
    meta_ref[...] = meta
    metat_ref[...] = meta.T[:metat_ref.shape[0]]
    carry = carry_ref[...] + jnp.sum(multihot, axis=0, keepdims=True)
    carry_ref[...] = carry
    cnt_ref[...] = carry


def _outproj(x, o_mla, o_f, o_b, z, gt1, sc2, sh2, w_o, dn_out_norm, norm2, w_router, b_router):
    B, S, D = x.shape
    tm = min(TM_OUT, S)
    nt = S // tm
    T = B * S
    nh = o_f.shape[2]
    wr_hi = w_router.astype(bf16)
    wr_lo = (w_router - wr_hi.astype(f32)).astype(bf16)
    zpad = jnp.zeros((D, LANES - N_EXPERTS), bf16)
    wr = jnp.concatenate([wr_hi, zpad, wr_lo, zpad], axis=1)
    br = jnp.pad(b_router, (0, LANES - N_EXPERTS)).reshape(1, LANES)
    below = jnp.tril(jnp.ones((tm, tm), bf16), -1)
    consts = [w_o.astype(bf16), dn_out_norm.reshape(1, -1), norm2.reshape(1, D), wr, br, below]

    def full(a):
        return pl.BlockSpec(a.shape, lambda b, i: (0,) * a.ndim)

    tok = lambda w: pl.BlockSpec((1, tm, w), lambda b, i: (b, i, 0))
    vec = pl.BlockSpec((1, 1, D), lambda b, i: (b, 0, 0))
    flat = lambda w: pl.BlockSpec((tm, w), lambda b, i: (b * nt + i, 0))
    return pl.pallas_call(
        _outproj_kernel,
        grid=(B, nt),
        in_specs=[tok(D), tok(o_mla.shape[2]), tok(nh), tok(nh), tok(nh), vec, vec, vec] + [full(a) for a in consts],
        out_specs=[tok(D), pl.BlockSpec((tm * ROW_SLABS, LANES), lambda b, i: (b * nt + i, 0)), flat(LANES),
                   pl.BlockSpec((2 * TOP_K, tm), lambda b, i: (0, b * nt + i)),
                   pl.BlockSpec((8, LANES), lambda b, i: (0, 0))],
        out_shape=[jax.ShapeDtypeStruct((B, S, D), f32), jax.ShapeDtypeStruct((T * ROW_SLABS, LANES), u32),
                   jax.ShapeDtypeStruct((T, LANES), f32), jax.ShapeDtypeStruct((2 * TOP_K, T), f32),
                   jax.ShapeDtypeStruct((8, LANES), f32)],
        scratch_shapes=[pltpu.VMEM((8, LANES), f32)],
        compiler_params=_cparams(("arbitrary", "arbitrary")),
        name="outproj",
    )(x, o_mla, o_f, o_b, z, gt1, sc2, sh2, *consts)


DMA_UNROLL = 2


def _row_copy(src_ref, s, dst_ref, d, sem):
    return pltpu.make_async_copy(src_ref.at[pl.ds(pl.multiple_of(s * ROW_SLABS, ROW_SLABS), ROW_SLABS)],
                                 dst_ref.at[pl.ds(pl.multiple_of(d * ROW_SLABS, ROW_SLABS), ROW_SLABS)], sem)


def _dispatch_kernel(ends_ref, dest_ref, hp_ref, xb_ref, zero_ref, sem, zsem):
    n = dest_ref.shape[0]

    @pl.when(pl.program_id(0) == 0)
    def _():
        zero_ref[...] = jnp.zeros_like(zero_ref)
        rows = zero_ref.shape[0]

        def zero_copy(e):
            start = pl.multiple_of(ends_ref[e] * ROW_SLABS - rows, ROW_SLABS * 8)
            return pltpu.make_async_copy(zero_ref, xb_ref.at[pl.ds(start, rows)], zsem)

        nb = xb_ref.shape[0] // rows
        n_used = ends_ref[N_EXPERTS]

        def tail_copy(j):
            return pltpu.make_async_copy(zero_ref, xb_ref.at[pl.ds(j * rows, rows)], zsem)

        for e in range(N_EXPERTS):
            @pl.when(ends_ref[e] >= 0)
            def _():
                zero_copy(e).start()
        for j in range(nb - N_EXPERTS, nb):
            @pl.when(j >= n_used)
            def _():
                tail_copy(j).start()
        for e in range(N_EXPERTS):
            @pl.when(ends_ref[e] >= 0)
            def _():
                zero_copy(e).wait()
        for j in range(nb - N_EXPERTS, nb):
            @pl.when(j >= n_used)
            def _():
                tail_copy(j).wait()

    def issue(t, carry):
        for kk in range(TOP_K):
            _row_copy(hp_ref, t, xb_ref, dest_ref[t * TOP_K + kk], sem).start(priority=kk % 2)
        return carry

    lax.fori_loop(0, n // TOP_K, issue, 0, unroll=DMA_UNROLL)
    for _ in range(TOP_K):
        pltpu.make_async_copy(hp_ref, xb_ref.at[pl.ds(0, hp_ref.shape[0])], sem).wait()


def _dispatch(hp, dest, ends, P):
    T = hp.shape[0] // ROW_SLABS
    tt = min(TT_DISPATCH, T)
    grid_spec = pltpu.PrefetchScalarGridSpec(
        num_scalar_prefetch=1,
        grid=(T // tt,),
        in_specs=[pl.BlockSpec((tt * TOP_K,), lambda i, ends: (i,), memory_space=pltpu.SMEM),
                  pl.BlockSpec((tt * ROW_SLABS, LANES), lambda i, ends: (i, 0))],
        out_specs=pl.BlockSpec(memory_space=pl.ANY),
        scratch_shapes=[pltpu.VMEM((MOE_BLOCK * ROW_SLABS, LANES), u32),
                        pltpu.SemaphoreType.DMA(()), pltpu.SemaphoreType.DMA(())],
    )
    return pl.pallas_call(
        _dispatch_kernel,
        grid_spec=grid_spec,
        out_shape=jax.ShapeDtypeStruct((P * ROW_SLABS, LANES), u32),
        compiler_params=pltpu.CompilerParams(dimension_semantics=("arbitrary",), has_side_effects=True),
        name="dispatch",
    )(ends, dest, hp)


def _expert_kernel(be_ref, nb_ref, x_ref, wg_ref, bg_ref, wu_ref, bu_ref, wd_ref, bd_ref, y_ref):
    b = pl.program_id(0)

    @pl.when(b < nb_ref[0])
    def _():
        subs = [pl.ds(h * MOE_SUB * ROW_SLABS, MOE_SUB * ROW_SLABS) for h in range(MOE_BLOCK // MOE_SUB)]
        xs = [_unpack_bf16_pairs(_load_rows(x_ref.at[sl], MOE_SUB)).astype(bf16) for sl in subs]
        gts = [jnp.minimum(_dot(x, wg_ref[0]) + bg_ref[0], SWIGLU_LIMIT) for x in xs]
        ups = [jnp.clip(_dot(x, wu_ref[0]) + bu_ref[0], -SWIGLU_LIMIT, SWIGLU_LIMIT) for x in xs]
        acts = [((up + 1.0) * gt * _sigmoid(SWIGLU_ALPHA * gt)).astype(bf16) for gt, up in zip(gts, ups)]
        ys = [_dot(act, wd_ref[0]) + bd_ref[0] for act in acts]
        for sl, y in zip(subs, ys):
            _store_rows(y_ref.at[sl], _pack_bf16_pairs(y))

    @pl.when(b >= nb_ref[0])
    def _():
        y_ref[...] = jnp.zeros_like(y_ref)


def _experts(xb, block_expert, n_used, wg, bg, wu, bu, wd, bd):
    E, D, F = wg.shape
    nb = xb.shape[0] // (MOE_BLOCK * ROW_SLABS)
    wspec = lambda r, c: pl.BlockSpec((1, r, c), lambda b, be, nu: (be[b], 0, 0))
    xspec = pl.BlockSpec((MOE_BLOCK * ROW_SLABS, LANES), lambda b, be, nu: (b, 0))
    grid_spec = pltpu.PrefetchScalarGridSpec(
        num_scalar_prefetch=2,
        grid=(nb,),
        in_specs=[xspec, wspec(D, F), wspec(1, F), wspec(D, F), wspec(1, F), wspec(F, D), wspec(1, D)],
        out_specs=xspec,
    )
    return pl.pallas_call(
        _expert_kernel,
        grid_spec=grid_spec,
        out_shape=jax.ShapeDtypeStruct(xb.shape, u32),
        compiler_params=_cparams(("arbitrary",)),
        name="experts",
    )(block_expert, n_used, xb, wg, bg.reshape(E, 1, F), wu, bu.reshape(E, 1, F), wd, bd.reshape(E, 1, D))


def _combine_kernel(dest_ref, dnext_ref, yb_ref, meta_ref, x1_ref, gt_ref, o_ref, buf_ref, sem):
    n = dest_ref.shape[0]
    tt = n // TOP_K
    g = pl.program_id(0)
    slot = g % 2

    def gather(idx_ref, s):
        def issue(t, carry):
            for kk in range(TOP_K):
                _row_copy(yb_ref, idx_ref[t * TOP_K + kk], buf_ref.at[s, kk], t, sem.at[s]).start(priority=kk % 2)
            return carry

        lax.fori_loop(0, tt, issue, 0, unroll=DMA_UNROLL)

    @pl.when(g == 0)
    def _():
        gather(dest_ref, slot)

    @pl.when(g + 1 < pl.num_programs(0))
    def _():
        gather(dnext_ref, 1 - slot)

    for kk in range(TOP_K):
        pltpu.make_async_copy(yb_ref.at[pl.ds(0, tt * ROW_SLABS)], buf_ref.at[slot, kk], sem.at[slot]).wait()
    meta = meta_ref[...]
    moe = jnp.zeros((tt, x1_ref.shape[2]), f32)
    for kk in range(TOP_K):
        gate = meta[:, 2 * TOP_K + kk: 2 * TOP_K + kk + 1]
        moe = moe + gate * _unpack_bf16_pairs(_load_rows(buf_ref.at[slot, kk], tt))
    o_ref[0] = x1_ref[0] + gt_ref[0] * moe


def _combine(yb, dest, meta, x1, gt2):
    B, S, D = x1.shape
    tt = min(TT_COMBINE, S)
    nt = S // tt
    ng = B * nt
    return pl.pallas_call(
        _combine_kernel,
        grid=(ng,),
        in_specs=[pl.BlockSpec((tt * TOP_K,), lambda g: (g,), memory_space=pltpu.SMEM),
                  pl.BlockSpec((tt * TOP_K,), lambda g: (jnp.minimum(g + 1, ng - 1),), memory_space=pltpu.SMEM),
                  pl.BlockSpec(memory_space=pl.ANY),
                  pl.BlockSpec((tt, LANES), lambda g: (g, 0)),
                  pl.BlockSpec((1, tt, D), lambda g: (g // nt, g % nt, 0)),
                  pl.BlockSpec((1, 1, D), lambda g: (g // nt, 0, 0))],
        out_specs=pl.BlockSpec((1, tt, D), lambda g: (g // nt, g % nt, 0)),
        out_shape=jax.ShapeDtypeStruct((B, S, D), f32),
        scratch_shapes=[pltpu.VMEM((2, TOP_K, tt * ROW_SLABS, LANES), u32), pltpu.SemaphoreType.DMA((2,))],
        compiler_params=_cparams(("arbitrary",)),
        name="combine",
    )(dest, dest, yb, meta, x1, gt2)


def _moe(hp, meta, meta_t, cnt, x1, gt2, ew):
    T = meta.shape[0]
    TK_ = T * TOP_K
    nb = -(-TK_ // MOE_BLOCK) + N_EXPERTS
    P = nb * MOE_BLOCK
    counts = cnt[0, :N_EXPERTS].astype(i32)
    padded = (counts + MOE_BLOCK - 1) // MOE_BLOCK * MOE_BLOCK
    cum_padded = jnp.cumsum(padded)
    pstart = cum_padded - padded
    top_idx = meta_t[:TOP_K].astype(i32)
    rank = meta_t[TOP_K:].astype(i32)
    base = jnp.zeros_like(rank)
    for e in range(N_EXPERTS):
        base = jnp.where(top_idx == e, pstart[e], base)
    dest = (base + rank).T.reshape(-1)
    block_start = jnp.arange(nb, dtype=i32) * MOE_BLOCK
    block_expert = jnp.minimum(jnp.sum((cum_padded[None, :] <= block_start[:, None]).astype(i32), axis=1),
                               N_EXPERTS - 1)
    n_used = (cum_padded[-1:] // MOE_BLOCK).astype(i32)
    ends = jnp.concatenate([jnp.where(padded > 0, cum_padded, -1).astype(i32), n_used])
    xb = _dispatch(hp, dest, ends, P)
    yb = _experts(xb, block_expert, n_used, *ew)
    return _combine(yb, dest, meta, x1, gt2)


def _layer(x, c, p, pw, ew, rope):
    B, S, D = x.shape
    mod = _ada(c, p["w_ada"], p["b_ada"]).reshape(B, 6, 1, D)
    sh1, sc1, gt1, sh2, sc2, gt2 = (mod[:, j] for j in range(6))
    q, k, v, dq, dk, dv, dkt, z, gcol, grow = _inproj(
        x, sh1, sc1, p["norm1"], p["q_a_norm"], p["kv_a_norm"], pw, rope)
    o_mla = _attention(q, k, v)
    o_f, o_b = _gdn(dq, dk, dv, dkt, gcol, grow)
    x1, hp, meta, meta_t, cnt = _outproj(x, o_mla, o_f, o_b, z, gt1, sc2, sh2, p["w_o"], p["dn_out_norm"],
                                         p["norm2"], p["w_router"], p["b_router"])
    return _moe(hp, meta, meta_t, cnt, x1, gt2, ew)


def kernel(x_prompt, x_sample, c_prompt, c_sample, w_ada, b_ada, norm1, w_in, q_a_norm, w_q_b, kv_a_norm, w_kv_b, q_norm, k_norm, dn_conv, dn_a_log, dn_dt_bias, dn_out_norm, w_o, norm2, w_router, b_router, w_gate, b_gate, w_up, b_up, w_down, b_down):
    y_prompt, y_sample = x_prompt, x_sample
    depth = w_ada.shape[0]
    for l in range(depth):
        p = {"w_ada": w_ada[l], "b_ada": b_ada[l], "norm1": norm1[l], "q_a_norm": q_a_norm[l],
             "kv_a_norm": kv_a_norm[l], "dn_out_norm": dn_out_norm[l], "w_o": w_o[l], "norm2": norm2[l],
             "w_router": w_router[l], "b_router": b_router[l]}
        pw = _prep_weights(w_in[l], w_q_b[l], w_kv_b[l], q_norm[l], k_norm[l], dn_conv[l], dn_a_log[l],
                           dn_dt_bias[l])
        ew = (w_gate[l].astype(bf16), b_gate[l], w_up[l].astype(bf16), b_up[l], w_down[l].astype(bf16), b_down[l])
        y_prompt = _layer(y_prompt, c_prompt, p, pw, ew, _rope_tables(y_prompt.shape[1]))
        y_sample = _layer(y_sample, c_sample, p, pw, ew, _rope_tables(y_sample.shape[1]))
    return (y_prompt, y_sample)
```

```python
import functools
import math

import jax
import jax.numpy as jnp
from jax import lax
from jax.experimental import pallas as pl
from jax.experimental.pallas import tpu as pltpu

f32 = jnp.float32
bf16 = jnp.bfloat16
u32 = jnp.uint32
i32 = jnp.int32

LANES = 128
VMEM_LIMIT = 56 * 1024 * 1024

MLA_HEADS = 8
MLA_Q_LORA = 384
MLA_KV_LORA = 256
MLA_NOPE = 64
MLA_ROPE = 32
MLA_QK = MLA_NOPE + MLA_ROPE
MLA_V = 64
ROPE_THETA = 10000.0
DN_HEADS = 4
DN_DK = 128
DN_DV = 128
DN_CONV = 5
DN_CHUNK = 64
DN_SUB = 16
N_EXPERTS = 32
TOP_K = 4
SWIGLU_LIMIT = 7.0
SWIGLU_ALPHA = 1.702
MOE_BLOCK = 512
MOE_SUB = 256
EPS = 1e-6

TM_IN = 512
TM_OUT = 512
TQ = 256
TK = 512
GDN_ROWS = 128
GDN_SCAN_ROWS = 512
TT_DISPATCH = 512
TT_COMBINE = 256


def _cparams(sem):
    return pltpu.CompilerParams(dimension_semantics=sem, vmem_limit_bytes=VMEM_LIMIT)


def _split2(x):
    hi = x.astype(bf16)
    lo = (x - hi.astype(f32)).astype(bf16)
    return hi, lo


def _split3(x):
    hi = x.astype(bf16)
    r = x - hi.astype(f32)
    mid = r.astype(bf16)
    lo = (r - mid.astype(f32)).astype(bf16)
    return hi, mid, lo


def _dot(a, b):
    return jnp.dot(a, b, preferred_element_type=f32)


def _dot_nt(a, b):
    return lax.dot_general(a, b, (((1,), (1,)), ((), ())), preferred_element_type=f32)


def _dot3(a, b):
    ah, al = _split2(a)
    bh, bl = _split2(b)
    return _dot(ah, bh) + (_dot(al, bh) + _dot(ah, bl))


def _sigmoid(x):
    return 1.0 / (1.0 + jnp.exp(-x))


def _silu(x):
    return x * _sigmoid(x)


def _softplus(x):
    return jnp.maximum(x, 0.0) + jnp.log(1.0 + jnp.exp(-jnp.abs(x)))


def _ada_kernel(c_ref, w_ref, b_ref, o_ref):
    c = c_ref[...]
    o_ref[...] = _dot3(_silu(c), w_ref[...]) + b_ref[...]


def _ada(c, w_ada, b_ada):
    B, D = c.shape
    N = w_ada.shape[1]
    cp = jnp.pad(c, ((0, 8 - B), (0, 0)))
    tn = 1024
    out = pl.pallas_call(
        _ada_kernel,
        grid=(N // tn,),
        in_specs=[
            pl.BlockSpec((8, D), lambda j: (0, 0)),
            pl.BlockSpec((D, tn), lambda j: (0, j)),
            pl.BlockSpec((1, tn), lambda j: (0, j)),
        ],
        out_specs=pl.BlockSpec((8, tn), lambda j: (0, j)),
        out_shape=jax.ShapeDtypeStruct((8, N), f32),
        compiler_params=_cparams(("parallel",)),
        name="ada",
    )(cp, w_ada, b_ada.reshape(1, N))
    return out[:B]


def _chunk_tri(n, lower):
    r = lax.broadcasted_iota(i32, (n, n), 0)
    c = lax.broadcasted_iota(i32, (n, n), 1)
    same = (r // DN_CHUNK) == (c // DN_CHUNK)
    tri = (c <= r) if lower else (c >= r)
    return jnp.where(same & tri, 1.0, 0.0).astype(bf16)


def _inproj_kernel(
    x_ref, xp_ref, xn_ref, sh_ref, sc_ref, n1_ref,
    wqa_ref, wkva_ref, wkr_ref, wqkv_ref, wz_ref, wab_ref, wabt_ref,
    qan_ref, kvan_ref, wqb_ref, wkb_ref, wvb_ref, qn_ref, kn_ref,
    cos_ref, sina_ref, sinb_ref, conv_ref, gpar_ref, gpart_ref, lo_ref, up_ref,
    q_ref, k_ref, v_ref, dq_ref, dk_ref, dv_ref, dkt_ref, z_ref, gcol_ref, grow_ref,
    ext_ref,
):
    i = pl.program_id(1)
    ni = pl.num_programs(1)
    tm = x_ref.shape[1]
    scale1 = 1.0 + sc_ref[0]
    shift1 = sh_ref[0]
    n1 = n1_ref[...]

    def modulate(xv):
        y = xv * lax.rsqrt(jnp.mean(xv * xv, axis=-1, keepdims=True) + EPS)
        return y * n1 * scale1 + shift1

    h = modulate(x_ref[0])
    hb = h.astype(bf16)
    hh = modulate(jnp.concatenate([xp_ref[0], xn_ref[0]], axis=0)).astype(bf16)

    qa = _dot(hb, wqa_ref[...])
    qa = qa * lax.rsqrt(jnp.mean(qa * qa, axis=-1, keepdims=True) + EPS) * qan_ref[...]
    kva = _dot(hb, wkva_ref[...])
    kva = kva * lax.rsqrt(jnp.mean(kva * kva, axis=-1, keepdims=True) + EPS) * kvan_ref[...]
    kr = _dot(hb, wkr_ref[...])
    qh = _dot(qa.astype(bf16), wqb_ref[...])
    kvb = kva.astype(bf16)
    kh = _dot(kvb, wkb_ref[...])
    vh = _dot(kvb, wvb_ref[...])
    cos = cos_ref[...]
    sina = sina_ref[...]
    sinb = sinb_ref[...]
    qg = qn_ref[...]
    kg = kn_ref[...]
    lane = lax.broadcasted_iota(i32, (tm, LANES), 1)
    q_scale = MLA_QK ** -0.5 * math.log2(math.e)

    def norm_rope(blk, gain):
        ss = jnp.sum(blk * blk, axis=-1, keepdims=True) * (1.0 / MLA_QK)
        y = blk * lax.rsqrt(ss + EPS) * gain
        return y * cos + pltpu.roll(y, LANES - MLA_ROPE // 2, 1) * sina + pltpu.roll(y, MLA_ROPE // 2, 1) * sinb

    for hd in range(MLA_HEADS):
        sl = slice(hd * LANES, (hd + 1) * LANES)
        q_ref[0, hd] = (norm_rope(qh[:, sl], qg) * q_scale).astype(bf16)
        k_ref[0, hd] = norm_rope(kh[:, sl] + kr, kg).astype(bf16)
        vblk = vh[:, hd * MLA_V:(hd + 1) * MLA_V]
        vpad = jnp.concatenate([vblk, jnp.zeros((tm, LANES - MLA_V), f32)], axis=1)
        v_ref[0, hd] = jnp.where(lane == MLA_V, 1.0, vpad).astype(bf16)

    ext_ref[pl.ds(8, tm), :] = _dot(hb, wqkv_ref[...])
    halo = _dot(hh, wqkv_ref[...])
    ext_ref[pl.ds(0, 8), :] = jnp.where(i == 0, 0.0, halo[:8])
    ext_ref[pl.ds(8 + tm, 8), :] = jnp.where(i == ni - 1, 0.0, halo[8:])
    pad = (DN_CONV - 1) // 2
    cw = conv_ref[...]
    acc = ext_ref[pl.ds(8 - pad, tm), :] * cw[0:1]
    for j in range(1, DN_CONV):
        acc = acc + ext_ref[pl.ds(8 - pad + j, tm), :] * cw[j:j + 1]
    act = _silu(acc)
    nqk = DN_HEADS * DN_DK
    for hd in range(DN_HEADS):
        sl = slice(hd * DN_DK, (hd + 1) * DN_DK)
        qb = act[:, sl]
        qb = qb * lax.rsqrt(jnp.sum(qb * qb, axis=-1, keepdims=True) + EPS) * (DN_DK ** -0.5)
        dq_ref[0, :, sl] = qb
        kb = act[:, nqk + hd * DN_DK: nqk + (hd + 1) * DN_DK]
        kb = kb * lax.rsqrt(jnp.sum(kb * kb, axis=-1, keepdims=True) + EPS)
        dk_ref[0, :, sl] = kb
        dkt_ref[0, sl, :] = kb.T
    dv_ref[0] = act[:, 2 * nqk:]
    z_ref[0] = _dot(hb, wz_ref[...])

    hlo = (h - hb.astype(f32)).astype(bf16)
    wab = wab_ref[...]
    p1 = _dot(hb, wab)
    p2 = _dot(hlo, wab)
    ab = p1 + pltpu.roll(p1, LANES - 16, 1) + p2
    gpar = gpar_ref[...]
    lane16 = lax.broadcasted_iota(i32, (tm, LANES), 1)
    gval = jnp.where(lane16 < 8, -gpar[0:1] * _softplus(ab + gpar[1:2]), _sigmoid(ab))
    g3 = _split3(gval)
    lo_tri = lo_ref[...]
    up_tri = up_ref[...]
    pre = _dot(lo_tri, g3[0]) + (_dot(lo_tri, g3[1]) + _dot(lo_tri, g3[2]))
    suf = _dot(up_tri, g3[0]) + (_dot(up_tri, g3[1]) + _dot(up_tri, g3[2]))
    gcol_ref[0] = jnp.where(lane16 < 4, pre, jnp.where(lane16 < 8, suf, gval))

    wabt = wabt_ref[...]
    r1 = _dot_nt(wabt, hb)
    r2 = _dot_nt(wabt[:16], hlo)
    abt = r1[:16] + r1[16:] + r2
    gpt = gpart_ref[...]
    row16 = lax.broadcasted_iota(i32, (16, tm), 0)
    gvt = jnp.where(row16 < 8, -gpt[:, 0:1] * _softplus(abt + gpt[:, 1:2]), _sigmoid(abt))
    t3 = _split3(gvt)
    pre_t = _dot(t3[0], up_tri) + (_dot(t3[1], up_tri) + _dot(t3[2], up_tri))
    suf_t = _dot(t3[0], lo_tri) + (_dot(t3[1], lo_tri) + _dot(t3[2], lo_tri))
    grow_ref[0] = jnp.where(row16 < 4, pre_t, jnp.where(row16 < 8, suf_t, gvt))


def _prep_weights(w_in, w_q_b, w_kv_b, q_norm, k_norm, dn_conv, dn_a_log, dn_dt_bias):
    D = w_in.shape[0]
    i0 = MLA_Q_LORA
    i1 = i0 + MLA_KV_LORA
    i2 = i1 + MLA_ROPE
    nqkv = DN_HEADS * (2 * DN_DK + DN_DV)
    i3 = i2 + nqkv
    i4 = i3 + DN_HEADS * DN_DV
    wqa = w_in[:, :i0].astype(bf16)
    wkva = w_in[:, i0:i1].astype(bf16)
    wkr = jnp.pad(w_in[:, i1:i2], ((0, 0), (MLA_NOPE, LANES - MLA_QK))).astype(bf16)
    wqkv = w_in[:, i2:i3].astype(bf16)
    wz = w_in[:, i3:i4].astype(bf16)
    wab_f = w_in[:, i4:]
    wab_hi = wab_f.astype(bf16)
    wab_lo = (wab_f - wab_hi.astype(f32)).astype(bf16)
    wab = jnp.pad(jnp.concatenate([wab_hi, wab_lo], axis=1), ((0, 0), (0, LANES - 32)))
    wabt = jnp.concatenate([wab_hi.T, wab_lo.T], axis=0)
    wqb = w_q_b.reshape(MLA_Q_LORA, MLA_HEADS, MLA_QK)
    wqb = jnp.pad(wqb, ((0, 0), (0, 0), (0, LANES - MLA_QK))).reshape(MLA_Q_LORA, MLA_HEADS * LANES).astype(bf16)
    wkv = w_kv_b.reshape(MLA_KV_LORA, MLA_HEADS, MLA_NOPE + MLA_V)
    wkb = jnp.pad(wkv[:, :, :MLA_NOPE], ((0, 0), (0, 0), (0, LANES - MLA_NOPE)))
    wkb = wkb.reshape(MLA_KV_LORA, MLA_HEADS * LANES).astype(bf16)
    wvb = wkv[:, :, MLA_NOPE:].reshape(MLA_KV_LORA, MLA_HEADS * MLA_V).astype(bf16)
    qn = jnp.pad(q_norm, (0, LANES - MLA_QK)).reshape(1, LANES)
    kn = jnp.pad(k_norm, (0, LANES - MLA_QK)).reshape(1, LANES)
    conv = jnp.pad(dn_conv, ((0, 8 - DN_CONV), (0, 0)))
    ea = jnp.exp(dn_a_log.astype(f32)).reshape(-1)
    dtb = dn_dt_bias.astype(f32).reshape(-1)
    gpar = jnp.pad(jnp.stack([ea, dtb]), ((0, 6), (0, LANES - 8)))
    gpart = jnp.pad(jnp.stack([ea, dtb], axis=1), ((0, 8), (0, LANES - 2)))
    return dict(wqa=wqa, wkva=wkva, wkr=wkr, wqkv=wqkv, wz=wz, wab=wab, wabt=wabt, wqb=wqb, wkb=wkb,
                wvb=wvb, qn=qn, kn=kn, conv=conv, gpar=gpar, gpart=gpart)


def _rope_tables(S):
    half = MLA_ROPE // 2
    freq = ROPE_THETA ** (-jnp.arange(half, dtype=f32) / half)
    ang = jnp.arange(S, dtype=f32)[:, None] * freq[None, :]
    cos, sin = jnp.cos(ang), jnp.sin(ang)
    zeros = lambda w: jnp.zeros((S, w), f32)
    cos_t = jnp.concatenate([jnp.ones((S, MLA_NOPE), f32), cos, cos, zeros(LANES - MLA_QK)], axis=1)
    sina = jnp.concatenate([zeros(MLA_NOPE), -sin, zeros(LANES - MLA_NOPE - half)], axis=1)
    sinb = jnp.concatenate([zeros(MLA_NOPE + half), sin, zeros(LANES - MLA_QK)], axis=1)
    return cos_t, sina, sinb


def _inproj(x, sh1, sc1, norm1, q_a_norm, kv_a_norm, pw, rope):
    B, S, D = x.shape
    tm = min(TM_IN, S)
    nt = S // tm
    r8 = tm // 8
    nqkv = DN_HEADS * (2 * DN_DK + DN_DV)
    nh = DN_HEADS * DN_DK

    def full(a):
        return pl.BlockSpec(a.shape, lambda b, i: (0,) * a.ndim)

    tok = lambda w: pl.BlockSpec((1, tm, w), lambda b, i: (b, i, 0))
    in_specs = [
        tok(D),
        pl.BlockSpec((1, 8, D), lambda b, i: (b, jnp.maximum(i * r8 - 1, 0), 0)),
        pl.BlockSpec((1, 8, D), lambda b, i: (b, jnp.minimum((i + 1) * r8, S // 8 - 1), 0)),
        pl.BlockSpec((1, 1, D), lambda b, i: (b, 0, 0)),
        pl.BlockSpec((1, 1, D), lambda b, i: (b, 0, 0)),
    ]
    consts = [norm1.reshape(1, D), pw["wqa"], pw["wkva"], pw["wkr"], pw["wqkv"], pw["wz"], pw["wab"], pw["wabt"],
              q_a_norm.reshape(1, -1), kv_a_norm.reshape(1, -1), pw["wqb"], pw["wkb"], pw["wvb"], pw["qn"], pw["kn"]]
    in_specs += [full(a) for a in consts]
    in_specs += [pl.BlockSpec((tm, LANES), lambda b, i: (i, 0))] * 3
    tail = [pw["conv"], pw["gpar"], pw["gpart"], _chunk_tri(tm, True), _chunk_tri(tm, False)]
    in_specs += [full(a) for a in tail]
    hspec = pl.BlockSpec((1, MLA_HEADS, tm, LANES), lambda b, i: (b, 0, i, 0))
    out_specs = [hspec, hspec, hspec, tok(nh), tok(nh), tok(nh),
                 pl.BlockSpec((1, nh, tm), lambda b, i: (b, 0, i)), tok(nh), tok(LANES),
                 pl.BlockSpec((1, 16, tm), lambda b, i: (b, 0, i))]
    hshape = jax.ShapeDtypeStruct((B, MLA_HEADS, S, LANES), bf16)
    tshape = jax.ShapeDtypeStruct((B, S, nh), f32)
    out_shape = [hshape, hshape, hshape, tshape, tshape, tshape,
                 jax.ShapeDtypeStruct((B, nh, S), f32), tshape,
                 jax.ShapeDtypeStruct((B, S, LANES), f32), jax.ShapeDtypeStruct((B, 16, S), f32)]
    return pl.pallas_call(
        _inproj_kernel,
        grid=(B, nt),
        in_specs=in_specs,
        out_specs=out_specs,
        out_shape=out_shape,
        scratch_shapes=[pltpu.VMEM((tm + 16, nqkv), f32)],
        compiler_params=_cparams(("parallel", "parallel")),
        name="inproj",
    )(x, x, x, sh1, sc1, *consts, *rope, *tail)


def _attn_kernel(q_ref, k_ref, v_ref, o_ref, s_scr, m_scr, acc_scr):
    S = k_ref.shape[2]
    tk = s_scr.shape[3]
    nk = S // tk
    nh = q_ref.shape[1]
    qs = [q_ref[0, hd] for hd in range(nh)]
    m_scr[...] = jnp.full(m_scr.shape, -jnp.inf, f32)
    acc_scr[...] = jnp.zeros(acc_scr.shape, f32)

    def scores(slot, j):
        for hd in range(nh):
            s_scr[slot, hd] = _dot_nt(qs[hd], k_ref[0, hd, pl.ds(j * tk, tk), :])

    def accumulate(slot, j):
        off = j * tk
        for hd in range(nh):
            s = s_scr[slot, hd]
            m = m_scr[hd]
            m_new = jnp.maximum(m, jnp.broadcast_to(jnp.max(s, axis=-1, keepdims=True), m.shape))
            p = jnp.exp2(s - jnp.tile(m_new, (1, tk // LANES)))
            m_scr[hd] = m_new
            acc_scr[hd] = acc_scr[hd] * jnp.exp2(m - m_new) + _dot(p.astype(bf16), v_ref[0, hd, pl.ds(off, tk), :])

    scores(0, 0)
    for j in range(nk):
        if j + 1 < nk:
            scores((j + 1) % 2, j + 1)
        accumulate(j % 2, j)
    outs = [acc_scr[hd][:, :MLA_V] / acc_scr[hd][:, MLA_V:MLA_V + 1] for hd in range(nh)]
    o_ref[0] = jnp.concatenate(outs, axis=1).astype(o_ref.dtype)


def _attention(q, k, v):
    B, H, S, _ = q.shape
    tq = min(TQ, S)
    tk = min(TK, S // 2)
    nh = 2
    qspec = pl.BlockSpec((1, nh, tq, LANES), lambda b, h, i: (b, h, i, 0))
    kspec = pl.BlockSpec((1, nh, S, LANES), lambda b, h, i: (b, h, 0, 0))
    return pl.pallas_call(
        _attn_kernel,
        grid=(B, H // nh, S // tq),
        in_specs=[qspec, kspec, kspec],
        out_specs=pl.BlockSpec((1, tq, LANES), lambda b, h, i: (b, i, h)),
        out_shape=jax.ShapeDtypeStruct((B, S, H * MLA_V), bf16),
        scratch_shapes=[pltpu.VMEM((2, nh, tq, tk), f32), pltpu.VMEM((nh, tq, LANES), f32),
                        pltpu.VMEM((nh, tq, LANES), f32)],
        compiler_params=_cparams(("parallel", "parallel", "arbitrary")),
        name="attn",
    )(q, k, v)


def _unit_tri_inverses(Ls, sub_mask):
    C = Ls[0].shape[0]
    r = lax.broadcasted_iota(i32, (C, C), 0)
    c = lax.broadcasted_iota(i32, (C, C), 1)
    diag = r == c

    def with_eye(s, sign):
        hi, lo = (s[0], s[1]) if sign > 0 else (-s[0], -s[1])
        return jnp.where(diag, jnp.ones((), bf16), hi), lo

    sLds = [_split2(jnp.where(sub_mask, L, 0.0)) for L in Ls]
    sLos = [_split2(jnp.where(sub_mask, 0.0, L)) for L in Ls]
    sTs = [with_eye(s, -1) for s in sLds]
    sPs = sLds
    n = 2
    while n < DN_SUB:
        sPs = [_split2(_dot3s(s, s)) for s in sPs]
        sTs = [_split2(_dot3s(sT, with_eye(sP, 1))) for sT, sP in zip(sTs, sPs)]
        n *= 2
    sNs = [_split2(_dot3s(sT, sLo)) for sT, sLo in zip(sTs, sLos)]
    sN2s = [_split2(_dot3s(s, s)) for s in sNs]
    sMs = [_split2(_dot3s(with_eye(sN, -1), with_eye(sN2, 1))) for sN, sN2 in zip(sNs, sN2s)]
    return [_dot3s(sM, sT) for sM, sT in zip(sMs, sTs)]


def _dot3s(sa, sb):
    lhs = jnp.concatenate([sa[0], sa[1], sa[0]], axis=1)
    rhs = jnp.concatenate([sb[0], sb[0], sb[1]], axis=0)
    return _dot(lhs, rhs)


def _dot3k(a, b):
    return _dot3s(_split2(a), _split2(b))


def _gdn_prep_kernel(q_ref, k_ref, v_ref, kt_ref, gcol_ref, grow_ref,
                     uf_ref, wf_ref, qf_ref, af_ref, ktf_ref, ub_ref, wb_ref, qb_ref, ab_ref, ktb_ref, dl_ref):
    C = DN_CHUNK
    nch = q_ref.shape[1] // C
    r = lax.broadcasted_iota(i32, (C, C), 0)
    c = lax.broadcasted_iota(i32, (C, C), 1)
    sub_mask = (r // DN_SUB) == (c // DN_SUB)
    outs = ((uf_ref, wf_ref, qf_ref, af_ref, ktf_ref), (ub_ref, wb_ref, qb_ref, ab_ref, ktb_ref))
    chains = [(ch, hd) for ch in range(nch) for hd in range(DN_HEADS)]
    kks, qks = {}, {}
    for ch, hd in chains:
        rs = slice(ch * C, (ch + 1) * C)
        hs = slice(hd * DN_DK, (hd + 1) * DN_DK)
        kbf = k_ref[0, rs, hs].astype(bf16)
        kks[ch, hd] = _dot_nt(kbf, kbf)
        qks[ch, hd] = _dot_nt(q_ref[0, rs, hs].astype(bf16), kbf)
    for ch in range(nch):
        rs = slice(ch * C, (ch + 1) * C)
        dls = []
        for d, reverse in enumerate((False, True)):
            for hd in range(DN_HEADS):
                gr = grow_ref[0, 4 * d + hd: 4 * d + hd + 1, rs]
                g_last = gr[:, 0:1] if reverse else gr[:, C - 1:C]
                dls.append(jnp.broadcast_to(jnp.exp(g_last), (1, LANES)))
        dl_ref[0, ch] = jnp.concatenate(dls, axis=0)

    full = [(ch, hd, d) for ch, hd in chains for d in range(2)]
    Ls, rhss = [], []
    for ch, hd, d in full:
        rs = slice(ch * C, (ch + 1) * C)
        hs = slice(hd * DN_DK, (hd + 1) * DN_DK)
        incl = (c >= r) if d else (c <= r)
        gc = gcol_ref[0, rs, 4 * d + hd: 4 * d + hd + 1]
        beta = gcol_ref[0, rs, 8 + 4 * d + hd: 8 + 4 * d + hd + 1]
        gr = grow_ref[0, 4 * d + hd: 4 * d + hd + 1, rs]
        dec = jnp.exp(jnp.where(incl, gc - gr, -jnp.inf))
        outs[d][3][0, hd, rs, :] = (qks[ch, hd] * dec).astype(bf16)
        Ls.append(jnp.where(r == c, 0.0, beta * kks[ch, hd] * dec))
        g_last = gr[:, 0:1] if d else gr[:, C - 1:C]
        outs[d][4][0, ch, hs, :] = (kt_ref[0, hs, rs] * jnp.exp(g_last - gr)).astype(bf16)
        eg = jnp.exp(gc)
        outs[d][2][0, rs, hs] = (q_ref[0, rs, hs] * eg).astype(bf16)
        rhss.append(jnp.concatenate([v_ref[0, rs, hs] * beta, k_ref[0, rs, hs] * (beta * eg)], axis=1))
    Ts = _unit_tri_inverses(Ls, sub_mask)
    sols = [_dot3k(T, rhs) for T, rhs in zip(Ts, rhss)]
    for (ch, hd, d), sol in zip(full, sols):
        rs = slice(ch * C, (ch + 1) * C)
        hs = slice(hd * DN_DK, (hd + 1) * DN_DK)
        outs[d][0][0, rs, hs] = sol[:, :DN_DV]
        outs[d][1][0, rs, hs] = sol[:, DN_DV:].astype(bf16)


def _gdn_scan_kernel(uf_ref, wf_ref, qf_ref, af_ref, ktf_ref, dlf_ref,
                     ub_ref, wb_ref, qb_ref, ab_ref, ktb_ref, dlb_ref, of_ref, ob_ref, s_ref):
    @pl.when(pl.program_id(1) == 0)
    def _():
        s_ref[...] = jnp.zeros_like(s_ref)

    C = DN_CHUNK
    nch = uf_ref.shape[1] // C
    dirs = ((uf_ref, wf_ref, qf_ref, af_ref, ktf_ref, dlf_ref, of_ref),
            (ub_ref, wb_ref, qb_ref, ab_ref, ktb_ref, dlb_ref, ob_ref))
    chains = [(d, hd) for d in range(2) for hd in range(DN_HEADS)]
    sts = [s_ref[d, hd] for d, hd in chains]
    for step in range(nch):
        def sl(d, hd):
            ch = step if d == 0 else nch - 1 - step
            return ch, slice(ch * C, (ch + 1) * C), slice(hd * DN_DK, (hd + 1) * DN_DK)

        wss = []
        for (d, hd), st in zip(chains, sts):
            ch, rs, hs = sl(d, hd)
            wq = jnp.concatenate([dirs[d][1][0, rs, hs], dirs[d][2][0, rs, hs]], axis=0)
            wss.append(_dot(wq, st.astype(bf16)))
        vnbs = []
        for (d, hd), ws in zip(chains, wss):
            ch, rs, hs = sl(d, hd)
            vnbs.append((dirs[d][0][0, rs, hs] - ws[:C]).astype(bf16))
        new = []
        for (d, hd), st, ws, vnb in zip(chains, sts, wss, vnbs):
            ch, rs, hs = sl(d, hd)
            dirs[d][6][0, rs, hs] = ws[C:] + _dot(dirs[d][3][0, hd, rs, :], vnb)
            dl = dirs[d][5][0, ch, d * DN_HEADS + hd: d * DN_HEADS + hd + 1, :]
            new.append(st * dl + _dot(dirs[d][4][0, ch, hs, :], vnb))
        sts = new
    for (d, hd), st in zip(chains, sts):
        s_ref[d, hd] = st


def _gdn(dq, dk, dv, dkt, gcol, grow):
    B, S, W = dq.shape
    C = DN_CHUNK
    rows = min(GDN_ROWS, S)
    n = S // rows
    nch = rows // C
    tok = pl.BlockSpec((1, rows, W), lambda b, i: (b, i, 0))
    aspec = pl.BlockSpec((1, DN_HEADS, rows, C), lambda b, i: (b, 0, i, 0))
    kspec = pl.BlockSpec((1, nch, W, C), lambda b, i: (b, i, 0, 0))
    dspec = pl.BlockSpec((1, nch, 2 * DN_HEADS, LANES), lambda b, i: (b, i, 0, 0))
    t32 = jax.ShapeDtypeStruct((B, S, W), f32)
    t16 = jax.ShapeDtypeStruct((B, S, W), bf16)
    ashape = jax.ShapeDtypeStruct((B, DN_HEADS, S, C), bf16)
    kshape = jax.ShapeDtypeStruct((B, S // C, W, C), bf16)
    per_dir_specs = [tok, tok, tok, aspec, kspec]
    per_dir_shapes = [t32, t16, t16, ashape, kshape]
    prep = pl.pallas_call(
        _gdn_prep_kernel,
        grid=(B, n),
        in_specs=[tok, tok, tok, pl.BlockSpec((1, W, rows), lambda b, i: (b, 0, i)),
                  pl.BlockSpec((1, rows, LANES), lambda b, i: (b, i, 0)),
                  pl.BlockSpec((1, 16, rows), lambda b, i: (b, 0, i))],
        out_specs=per_dir_specs * 2 + [dspec],
        out_shape=per_dir_shapes * 2 + [jax.ShapeDtypeStruct((B, S // C, 2 * DN_HEADS, LANES), f32)],
        compiler_params=_cparams(("parallel", "parallel")),
        name="gdn_prep",
    )(dq, dk, dv, dkt, gcol, grow)
    fwd, bwd, dl = prep[:5], prep[5:10], prep[10]

    rows_s = min(GDN_SCAN_ROWS, S)
    ns = S // rows_s
    nchs = rows_s // C

    def specs(rev):
        blk = (lambda i: ns - 1 - i) if rev else (lambda i: i)
        return [pl.BlockSpec((1, rows_s, W), lambda b, i: (b, blk(i), 0))] * 3 + [
            pl.BlockSpec((1, DN_HEADS, rows_s, C), lambda b, i: (b, 0, blk(i), 0)),
            pl.BlockSpec((1, nchs, W, C), lambda b, i: (b, blk(i), 0, 0)),
            pl.BlockSpec((1, nchs, 2 * DN_HEADS, LANES), lambda b, i: (b, blk(i), 0, 0))]

    return pl.pallas_call(
        _gdn_scan_kernel,
        grid=(B, ns),
        in_specs=specs(False) + specs(True),
        out_specs=[specs(False)[0], specs(True)[0]],
        out_shape=[t32, t32],
        scratch_shapes=[pltpu.VMEM((2, DN_HEADS, DN_DK, DN_DV), f32)],
        compiler_params=_cparams(("parallel", "arbitrary")),
        name="gdn_scan",
    )(*fwd, dl, *bwd, dl)


def _pack_bf16_pairs(y):
    w = y.shape[1] // 2
    lo = pltpu.bitcast(y[:, :w].astype(bf16).astype(f32), u32)
    hi = pltpu.bitcast(y[:, w:].astype(bf16).astype(f32), u32)
    return (lo >> 16) | (hi & jnp.uint32(0xFFFF0000))


def _unpack_bf16_pairs(p):
    lo = pltpu.bitcast(p << 16, f32)
    hi = pltpu.bitcast(p & jnp.uint32(0xFFFF0000), f32)
    return jnp.concatenate([lo, hi], axis=1)


ROW_SLABS = 4


def _store_rows(ref, packed):
    n = packed.shape[0]
    for j in range(ROW_SLABS):
        ref[pl.ds(j, n, stride=ROW_SLABS), :] = packed[:, j * LANES:(j + 1) * LANES]


def _load_rows(ref, n):
    return jnp.concatenate([ref[pl.ds(j, n, stride=ROW_SLABS), :] for j in range(ROW_SLABS)], axis=1)


def _outproj_kernel(x_ref, om_ref, of_ref, ob_ref, z_ref, gt_ref, sc_ref, sh_ref,
                    wo_ref, dnn_ref, n2_ref, wr_ref, br_ref, below_ref,
                    x1_ref, hp_ref, meta_ref, metat_ref, cnt_ref, carry_ref):
    first = (pl.program_id(0) == 0) & (pl.program_id(1) == 0)

    @pl.when(first)
    def _():
        carry_ref[...] = jnp.zeros_like(carry_ref)

    tm = x_ref.shape[1]
    o = of_ref[0] + ob_ref[0]
    z = z_ref[0]
    dnn = dnn_ref[...]
    parts = []
    for hd in range(DN_HEADS):
        sl = slice(hd * DN_DV, (hd + 1) * DN_DV)
        ob = o[:, sl]
        ob = ob * lax.rsqrt(jnp.mean(ob * ob, axis=-1, keepdims=True) + EPS) * dnn
        parts.append(ob * _silu(z[:, sl]))
    odn = jnp.concatenate(parts, axis=1).astype(bf16)
    nm = om_ref.shape[2]
    mixed = _dot(om_ref[0], wo_ref[pl.ds(0, nm), :]) + _dot(odn, wo_ref[pl.ds(nm, odn.shape[1]), :])
    x1 = x_ref[0] + gt_ref[0] * mixed
    x1_ref[0] = x1
    h2 = x1 * lax.rsqrt(jnp.mean(x1 * x1, axis=-1, keepdims=True) + EPS) * n2_ref[...]
    h2 = h2 * (1.0 + sc_ref[0]) + sh_ref[0]
    _store_rows(hp_ref, _pack_bf16_pairs(h2))

    hh, hl = _split2(h2)
    wr = wr_ref[...]
    p1 = _dot(hh, wr)
    p2 = _dot(hl, wr[:, :LANES])
    lane = lax.broadcasted_iota(i32, (tm, LANES), 1)
    logits = p1[:, :LANES] + p1[:, LANES:] + p2 + br_ref[...]
    logits = jnp.where(lane < N_EXPERTS, logits, -jnp.inf)
    vals, idxs = [], []
    work = logits
    for _ in range(TOP_K):
        mx = jnp.max(work, axis=-1, keepdims=True)
        ix = jnp.min(jnp.where(work == mx, lane, LANES), axis=-1, keepdims=True)
        vals.append(mx)
        idxs.append(ix)
        work = jnp.where(lane == ix, -jnp.inf, work)
    es = [jnp.exp(vv - vals[0]) for vv in vals]
    den = es[0] + es[1] + es[2] + es[3]
    multihot = jnp.where(work != logits, 1.0, 0.0)
    prefix = _dot(below_ref[...], multihot.astype(bf16)) + carry_ref[0:1]
    meta = jnp.zeros((tm, LANES), f32)
    for kk in range(TOP_K):
        rank = jnp.sum(jnp.where(lane == idxs[kk], prefix, 0.0), axis=-1, keepdims=True)
        meta = jnp.where(lane == kk, idxs[kk].astype(f32), meta)
        meta = jnp.where(lane == TOP_K + kk, rank, meta)
        meta = jnp.where(lane == 2 * TOP_K + kk, es[kk] / den, meta)
    meta_ref[...] = meta
    metat_ref[...] = meta.T[:metat_ref.shape[0]]
    carry = carry_ref[...] + jnp.sum(multihot, axis=0, keepdims=True)
    carry_ref[...] = carry
    cnt_ref[...] = carry


def _outproj(x, o_mla, o_f, o_b, z, gt1, sc2, sh2, w_o, dn_out_norm, norm2, w_router, b_router):
    B, S, D = x.shape
    tm = min(TM_OUT, S)
    nt = S // tm
    T = B * S
    nh = o_f.shape[2]
    wr_hi = w_router.astype(bf16)
    wr_lo = (w_router - wr_hi.astype(f32)).astype(bf16)
    zpad = jnp.zeros((D, LANES - N_EXPERTS), bf16)
    wr = jnp.concatenate([wr_hi, zpad, wr_lo, zpad], axis=1)
    br = jnp.pad(b_router, (0, LANES - N_EXPERTS)).reshape(1, LANES)
    below = jnp.tril(jnp.ones((tm, tm), bf16), -1)
    consts = [w_o.astype(bf16), dn_out_norm.reshape(1, -1), norm2.reshape(1, D), wr, br, below]

    def full(a):
        return pl.BlockSpec(a.shape, lambda b, i: (0,) * a.ndim)

    tok = lambda w: pl.BlockSpec((1, tm, w), lambda b, i: (b, i, 0))
    vec = pl.BlockSpec((1, 1, D), lambda b, i: (b, 0, 0))
    flat = lambda w: pl.BlockSpec((tm, w), lambda b, i: (b * nt + i, 0))
    return pl.pallas_call(
        _outproj_kernel,
        grid=(B, nt),
        in_specs=[tok(D), tok(o_mla.shape[2]), tok(nh), tok(nh), tok(nh), vec, vec, vec] + [full(a) for a in consts],
        out_specs=[tok(D), pl.BlockSpec((tm * ROW_SLABS, LANES), lambda b, i: (b * nt + i, 0)), flat(LANES),
                   pl.BlockSpec((2 * TOP_K, tm), lambda b, i: (0, b * nt + i)),
                   pl.BlockSpec((8, LANES), lambda b, i: (0, 0))],
        out_shape=[jax.ShapeDtypeStruct((B, S, D), f32), jax.ShapeDtypeStruct((T * ROW_SLABS, LANES), u32),
                   jax.ShapeDtypeStruct((T, LANES), f32), jax.ShapeDtypeStruct((2 * TOP_K, T), f32),
                   jax.ShapeDtypeStruct((8, LANES), f32)],
        scratch_shapes=[pltpu.VMEM((8, LANES), f32)],
        compiler_params=_cparams(("arbitrary", "arbitrary")),
        name="outproj",
    )(x, o_mla, o_f, o_b, z, gt1, sc2, sh2, *consts)


DMA_UNROLL = 2


def _row_copy(src_ref, s, dst_ref, d, sem):
    return pltpu.make_async_copy(src_ref.at[pl.ds(pl.multiple_of(s * ROW_SLABS, ROW_SLABS), ROW_SLABS)],
                                 dst_ref.at[pl.ds(pl.multiple_of(d * ROW_SLABS, ROW_SLABS), ROW_SLABS)], sem)


def _dispatch_kernel(ends_ref, dest_ref, hp_ref, xb_ref, zero_ref, sem, zsem):
    n = dest_ref.shape[0]

    @pl.when(pl.program_id(0) == 0)
    def _():
        zero_ref[...] = jnp.zeros_like(zero_ref)
        rows = zero_ref.shape[0]

        def zero_copy(e):
            start = pl.multiple_of(ends_ref[e] * ROW_SLABS - rows, ROW_SLABS * 8)
            return pltpu.make_async_copy(zero_ref, xb_ref.at[pl.ds(start, rows)], zsem)

        nb = xb_ref.shape[0] // rows
        n_used = ends_ref[N_EXPERTS]

        def tail_copy(j):
            return pltpu.make_async_copy(zero_ref, xb_ref.at[pl.ds(j * rows, rows)], zsem)

        for e in range(N_EXPERTS):
            @pl.when(ends_ref[e] >= 0)
            def _():
                zero_copy(e).start()
        for j in range(nb - N_EXPERTS, nb):
            @pl.when(j >= n_used)
            def _():
                tail_copy(j).start()
        for e in range(N_EXPERTS):
            @pl.when(ends_ref[e] >= 0)
            def _():
                zero_copy(e).wait()
        for j in range(nb - N_EXPERTS, nb):
            @pl.when(j >= n_used)
            def _():
                tail_copy(j).wait()

    def issue(t, carry):
        for kk in range(TOP_K):
            _row_copy(hp_ref, t, xb_ref, dest_ref[t * TOP_K + kk], sem).start(priority=kk % 2)
        return carry

    lax.fori_loop(0, n // TOP_K, issue, 0, unroll=DMA_UNROLL)
    for _ in range(TOP_K):
        pltpu.make_async_copy(hp_ref, xb_ref.at[pl.ds(0, hp_ref.shape[0])], sem).wait()


def _dispatch(hp, dest, ends, P):
    T = hp.shape[0] // ROW_SLABS
    tt = min(TT_DISPATCH, T)
    grid_spec = pltpu.PrefetchScalarGridSpec(
        num_scalar_prefetch=1,
        grid=(T // tt,),
        in_specs=[pl.BlockSpec((tt * TOP_K,), lambda i, ends: (i,), memory_space=pltpu.SMEM),
                  pl.BlockSpec((tt * ROW_SLABS, LANES), lambda i, ends: (i, 0))],
        out_specs=pl.BlockSpec(memory_space=pl.ANY),
        scratch_shapes=[pltpu.VMEM((MOE_BLOCK * ROW_SLABS, LANES), u32),
                        pltpu.SemaphoreType.DMA(()), pltpu.SemaphoreType.DMA(())],
    )
    return pl.pallas_call(
        _dispatch_kernel,
        grid_spec=grid_spec,
        out_shape=jax.ShapeDtypeStruct((P * ROW_SLABS, LANES), u32),
        compiler_params=pltpu.CompilerParams(dimension_semantics=("arbitrary",), has_side_effects=True),
        name="dispatch",
    )(ends, dest, hp)


def _expert_kernel(be_ref, nb_ref, x_ref, wg_ref, bg_ref, wu_ref, bu_ref, wd_ref, bd_ref, y_ref):
    b = pl.program_id(0)

    @pl.when(b < nb_ref[0])
    def _():
        subs = [pl.ds(h * MOE_SUB * ROW_SLABS, MOE_SUB * ROW_SLABS) for h in range(MOE_BLOCK // MOE_SUB)]
        xs = [_unpack_bf16_pairs(_load_rows(x_ref.at[sl], MOE_SUB)).astype(bf16) for sl in subs]
        gts = [jnp.minimum(_dot(x, wg_ref[0]) + bg_ref[0], SWIGLU_LIMIT) for x in xs]
        ups = [jnp.clip(_dot(x, wu_ref[0]) + bu_ref[0], -SWIGLU_LIMIT, SWIGLU_LIMIT) for x in xs]
        acts = [((up + 1.0) * gt * _sigmoid(SWIGLU_ALPHA * gt)).astype(bf16) for gt, up in zip(gts, ups)]
        ys = [_dot(act, wd_ref[0]) + bd_ref[0] for act in acts]
        for sl, y in zip(subs, ys):
            _store_rows(y_ref.at[sl], _pack_bf16_pairs(y))

    @pl.when(b >= nb_ref[0])
    def _():
        y_ref[...] = jnp.zeros_like(y_ref)


def _experts(xb, block_expert, n_used, wg, bg, wu, bu, wd, bd):
    E, D, F = wg.shape
    nb = xb.shape[0] // (MOE_BLOCK * ROW_SLABS)
    wspec = lambda r, c: pl.BlockSpec((1, r, c), lambda b, be, nu: (be[b], 0, 0))
    xspec = pl.BlockSpec((MOE_BLOCK * ROW_SLABS, LANES), lambda b, be, nu: (b, 0))
    grid_spec = pltpu.PrefetchScalarGridSpec(
        num_scalar_prefetch=2,
        grid=(nb,),
        in_specs=[xspec, wspec(D, F), wspec(1, F), wspec(D, F), wspec(1, F), wspec(F, D), wspec(1, D)],
        out_specs=xspec,
    )
    return pl.pallas_call(
        _expert_kernel,
        grid_spec=grid_spec,
        out_shape=jax.ShapeDtypeStruct(xb.shape, u32),
        compiler_params=_cparams(("arbitrary",)),
        name="experts",
    )(block_expert, n_used, xb, wg, bg.reshape(E, 1, F), wu, bu.reshape(E, 1, F), wd, bd.reshape(E, 1, D))


def _combine_kernel(dest_ref, dnext_ref, yb_ref, meta_ref, x1_ref, gt_ref, o_ref, buf_ref, sem):
    n = dest_ref.shape[0]
    tt = n // TOP_K
    g = pl.program_id(0)
    slot = g % 2

    def gather(idx_ref, s):
        def issue(t, carry):
            for kk in range(TOP_K):
                _row_copy(yb_ref, idx_ref[t * TOP_K + kk], buf_ref.at[s, kk], t, sem.at[s]).start(priority=kk % 2)
            return carry

        lax.fori_loop(0, tt, issue, 0, unroll=DMA_UNROLL)

    @pl.when(g == 0)
    def _():
        gather(dest_ref, slot)

    @pl.when(g + 1 < pl.num_programs(0))
    def _():
        gather(dnext_ref, 1 - slot)

    for kk in range(TOP_K):
        pltpu.make_async_copy(yb_ref.at[pl.ds(0, tt * ROW_SLABS)], buf_ref.at[slot, kk], sem.at[slot]).wait()
    meta = meta_ref[...]
    moe = jnp.zeros((tt, x1_ref.shape[2]), f32)
    for kk in range(TOP_K):
        gate = meta[:, 2 * TOP_K + kk: 2 * TOP_K + kk + 1]
        moe = moe + gate * _unpack_bf16_pairs(_load_rows(buf_ref.at[slot, kk], tt))
    o_ref[0] = x1_ref[0] + gt_ref[0] * moe


def _combine(yb, dest, meta, x1, gt2):
    B, S, D = x1.shape
    tt = min(TT_COMBINE, S)
    nt = S // tt
    ng = B * nt
    return pl.pallas_call(
        _combine_kernel,
        grid=(ng,),
        in_specs=[pl.BlockSpec((tt * TOP_K,), lambda g: (g,), memory_space=pltpu.SMEM),
                  pl.BlockSpec((tt * TOP_K,), lambda g: (jnp.minimum(g + 1, ng - 1),), memory_space=pltpu.SMEM),
                  pl.BlockSpec(memory_space=pl.ANY),
                  pl.BlockSpec((tt, LANES), lambda g: (g, 0)),
                  pl.BlockSpec((1, tt, D), lambda g: (g // nt, g % nt, 0)),
                  pl.BlockSpec((1, 1, D), lambda g: (g // nt, 0, 0))],
        out_specs=pl.BlockSpec((1, tt, D), lambda g: (g // nt, g % nt, 0)),
        out_shape=jax.ShapeDtypeStruct((B, S, D), f32),
        scratch_shapes=[pltpu.VMEM((2, TOP_K, tt * ROW_SLABS, LANES), u32), pltpu.SemaphoreType.DMA((2,))],
        compiler_params=_cparams(("arbitrary",)),
        name="combine",
    )(dest, dest, yb, meta, x1, gt2)


def _moe(hp, meta, meta_t, cnt, x1, gt2, ew):
    T = meta.shape[0]
    TK_ = T * TOP_K
    nb = -(-TK_ // MOE_BLOCK) + N_EXPERTS
    P = nb * MOE_BLOCK
    counts = cnt[0, :N_EXPERTS].astype(i32)
    padded = (counts + MOE_BLOCK - 1) // MOE_BLOCK * MOE_BLOCK
    cum_padded = jnp.cumsum(padded)
    pstart = cum_padded - padded
    top_idx = meta_t[:TOP_K].astype(i32)
    rank = meta_t[TOP_K:].astype(i32)
    base = jnp.zeros_like(rank)
    for e in range(N_EXPERTS):
        base = jnp.where(top_idx == e, pstart[e], base)
    dest = (base + rank).T.reshape(-1)
    block_start = jnp.arange(nb, dtype=i32) * MOE_BLOCK
    block_expert = jnp.minimum(jnp.sum((cum_padded[None, :] <= block_start[:, None]).astype(i32), axis=1),
                               N_EXPERTS - 1)
    n_used = (cum_padded[-1:] // MOE_BLOCK).astype(i32)
    ends = jnp.concatenate([jnp.where(padded > 0, cum_padded, -1).astype(i32), n_used])
    xb = _dispatch(hp, dest, ends, P)
    yb = _experts(xb, block_expert, n_used, *ew)
    return _combine(yb, dest, meta, x1, gt2)


def _layer(x, c, p, pw, ew, rope):
    B, S, D = x.shape
    mod = _ada(c, p["w_ada"], p["b_ada"]).reshape(B, 6, 1, D)
    sh1, sc1, gt1, sh2, sc2, gt2 = (mod[:, j] for j in range(6))
    q, k, v, dq, dk, dv, dkt, z, gcol, grow = _inproj(
        x, sh1, sc1, p["norm1"], p["q_a_norm"], p["kv_a_norm"], pw, rope)
    o_mla = _attention(q, k, v)
    o_f, o_b = _gdn(dq, dk, dv, dkt, gcol, grow)
    x1, hp, meta, meta_t, cnt = _outproj(x, o_mla, o_f, o_b, z, gt1, sc2, sh2, p["w_o"], p["dn_out_norm"],
                                         p["norm2"], p["w_router"], p["b_router"])
    return _moe(hp, meta, meta_t, cnt, x1, gt2, ew)


def kernel(x_prompt, x_sample, c_prompt, c_sample, w_ada, b_ada, norm1, w_in, q_a_norm, w_q_b, kv_a_norm, w_kv_b, q_norm, k_norm, dn_conv, dn_a_log, dn_dt_bias, dn_out_norm, w_o, norm2, w_router, b_router, w_gate, b_gate, w_up, b_up, w_down, b_down):
    y_prompt, y_sample = x_prompt, x_sample
    depth = w_ada.shape[0]
    for l in range(depth):
        p = {"w_ada": w_ada[l], "b_ada": b_ada[l], "norm1": norm1[l], "q_a_norm": q_a_norm[l],
             "kv_a_norm": kv_a_norm[l], "dn_out_norm": dn_out_norm[l], "w_o": w_o[l], "norm2": norm2[l],
             "w_router": w_router[l], "b_router": b_router[l]}
        pw = _prep_weights(w_in[l], w_q_b[l], w_kv_b[l], q_norm[l], k_norm[l], dn_conv[l], dn_a_log[l],
                           dn_dt_bias[l])
        ew = (w_gate[l].astype(bf16), b_gate[l], w_up[l].astype(bf16), b_up[l], w_down[l].astype(bf16), b_down[l])
        y_prompt = _layer(y_prompt, c_prompt, p, pw, ew, _rope_tables(y_prompt.shape[1]))
        y_sample = _layer(y_sample, c_sample, p, pw, ew, _rope_tables(y_sample.shape[1]))
    return (y_prompt, y_sample)
```

```python
import functools
import math

import jax
import jax.numpy as jnp
from jax import lax
from jax.experimental import pallas as pl
from jax.experimental.pallas import tpu as pltpu

f32 = jnp.float32
bf16 = jnp.bfloat16
u32 = jnp.uint32
i32 = jnp.int32

LANES = 128
VMEM_LIMIT = 56 * 1024 * 1024

MLA_HEADS = 8
MLA_Q_LORA = 384
MLA_KV_LORA = 256
MLA_NOPE = 64
MLA_ROPE = 32
MLA_QK = MLA_NOPE + MLA_ROPE
MLA_V = 64
V_ROWS = 80
ROPE_THETA = 10000.0
DN_HEADS = 4
DN_DK = 128
DN_DV = 128
DN_CONV = 5
DN_CHUNK = 64
DN_SUB = 16
N_EXPERTS = 32
TOP_K = 4
SWIGLU_LIMIT = 7.0
SWIGLU_ALPHA = 1.702
MOE_BLOCK = 512
MOE_SUB = 256
EPS = 1e-6

TM_IN = 512
TM_OUT = 512
TQ = 256
TK = 512
ATTN_SLAB = 256
GDN_ROWS = 128
GDN_SCAN_ROWS = 512
TT_DISPATCH = 512
TT_COMBINE = 256


def _cparams(sem):
    return pltpu.CompilerParams(dimension_semantics=sem, vmem_limit_bytes=VMEM_LIMIT)


def _split2(x):
    hi = x.astype(bf16)
    lo = (x - hi.astype(f32)).astype(bf16)
    return hi, lo


def _split3(x):
    hi = x.astype(bf16)
    r = x - hi.astype(f32)
    mid = r.astype(bf16)
    lo = (r - mid.astype(f32)).astype(bf16)
    return hi, mid, lo


def _dot(a, b):
    return jnp.dot(a, b, preferred_element_type=f32)


def _dot_nt(a, b):
    return lax.dot_general(a, b, (((1,), (1,)), ((), ())), preferred_element_type=f32)


def _dot3(a, b):
    ah, al = _split2(a)
    bh, bl = _split2(b)
    return _dot(ah, bh) + (_dot(al, bh) + _dot(ah, bl))


def _sigmoid(x):
    return 1.0 / (1.0 + jnp.exp(-x))


def _silu(x):
    return x * _sigmoid(x)


def _softplus(x):
    return jnp.maximum(x, 0.0) + jnp.log(1.0 + jnp.exp(-jnp.abs(x)))


def _ada_kernel(c_ref, w_ref, b_ref, o_ref):
    c = c_ref[...]
    o_ref[...] = _dot3(_silu(c), w_ref[...]) + b_ref[...]


def _ada(c, w_ada, b_ada):
    B, D = c.shape
    N = w_ada.shape[1]
    cp = jnp.pad(c, ((0, 8 - B), (0, 0)))
    tn = 1024
    out = pl.pallas_call(
        _ada_kernel,
        grid=(N // tn,),
        in_specs=[
            pl.BlockSpec((8, D), lambda j: (0, 0)),
            pl.BlockSpec((D, tn), lambda j: (0, j)),
            pl.BlockSpec((1, tn), lambda j: (0, j)),
        ],
        out_specs=pl.BlockSpec((8, tn), lambda j: (0, j)),
        out_shape=jax.ShapeDtypeStruct((8, N), f32),
        compiler_params=_cparams(("parallel",)),
        name="ada",
    )(cp, w_ada, b_ada.reshape(1, N))
    return out[:B]


def _chunk_tri(n, lower):
    r = lax.broadcasted_iota(i32, (n, n), 0)
    c = lax.broadcasted_iota(i32, (n, n), 1)
    same = (r // DN_CHUNK) == (c // DN_CHUNK)
    tri = (c <= r) if lower else (c >= r)
    return jnp.where(same & tri, 1.0, 0.0).astype(bf16)


def _inproj_kernel(
    x_ref, xp_ref, xn_ref, sh_ref, sc_ref, n1_ref,
    wqa_ref, wkva_ref, wkr_ref, wqkv_ref, wz_ref, wab_ref, wabt_ref,
    qan_ref, kvan_ref, wqb_ref, wkb_ref, wvb_ref, qn_ref, kn_ref,
    cos_ref, sina_ref, sinb_ref, conv_ref, gpar_ref, gpart_ref, lo_ref, up_ref,
    q_ref, k_ref, v_ref, dq_ref, dk_ref, dv_ref, dkt_ref, z_ref, gcol_ref, grow_ref,
    ext_ref,
):
    i = pl.program_id(1)
    ni = pl.num_programs(1)
    tm = x_ref.shape[1]
    scale1 = 1.0 + sc_ref[0]
    shift1 = sh_ref[0]
    n1 = n1_ref[...]

    def modulate(xv):
        y = xv * lax.rsqrt(jnp.mean(xv * xv, axis=-1, keepdims=True) + EPS)
        return y * n1 * scale1 + shift1

    h = modulate(x_ref[0])
    hb = h.astype(bf16)
    hh = modulate(jnp.concatenate([xp_ref[0], xn_ref[0]], axis=0)).astype(bf16)

    qa = _dot(hb, wqa_ref[...])
    qa = qa * lax.rsqrt(jnp.mean(qa * qa, axis=-1, keepdims=True) + EPS) * qan_ref[...]
    kva = _dot(hb, wkva_ref[...])
    kva = kva * lax.rsqrt(jnp.mean(kva * kva, axis=-1, keepdims=True) + EPS) * kvan_ref[...]
    kr = _dot(hb, wkr_ref[...])
    qh = _dot(qa.astype(bf16), wqb_ref[...])
    kvb = kva.astype(bf16)
    kh = _dot(kvb, wkb_ref[...])
    vh = _dot(kvb, wvb_ref[...])
    cos = cos_ref[...]
    sina = sina_ref[...]
    sinb = sinb_ref[...]
    qg = qn_ref[...]
    kg = kn_ref[...]
    q_scale = MLA_QK ** -0.5 * math.log2(math.e)

    def norm_rope(blk, gain):
        ss = jnp.sum(blk * blk, axis=-1, keepdims=True) * (1.0 / MLA_QK)
        y = blk * lax.rsqrt(ss + EPS) * gain
        return y * cos + pltpu.roll(y, LANES - MLA_ROPE // 2, 1) * sina + pltpu.roll(y, MLA_ROPE // 2, 1) * sinb

    for hd in range(MLA_HEADS):
        sl = slice(hd * LANES, (hd + 1) * LANES)
        q_ref[0, hd] = (norm_rope(qh[:, sl], qg) * q_scale).astype(bf16)
        k_ref[0, hd] = norm_rope(kh[:, sl] + kr, kg).astype(bf16)
    ones_tail = jnp.where(lax.broadcasted_iota(i32, (V_ROWS - MLA_V, tm), 0) == 0, 1.0, 0.0)
    for hp in range(MLA_HEADS // 2):
        vt = vh[:, hp * LANES:(hp + 1) * LANES].T
        for a in range(2):
            v_ref[0, 2 * hp + a] = jnp.concatenate([vt[a * MLA_V:(a + 1) * MLA_V], ones_tail], axis=0).astype(bf16)

    ext_ref[pl.ds(8, tm), :] = _dot(hb, wqkv_ref[...])
    halo = _dot(hh, wqkv_ref[...])
    ext_ref[pl.ds(0, 8), :] = jnp.where(i == 0, 0.0, halo[:8])
    ext_ref[pl.ds(8 + tm, 8), :] = jnp.where(i == ni - 1, 0.0, halo[8:])
    pad = (DN_CONV - 1) // 2
    cw = conv_ref[...]
    acc = ext_ref[pl.ds(8 - pad, tm), :] * cw[0:1]
    for j in range(1, DN_CONV):
        acc = acc + ext_ref[pl.ds(8 - pad + j, tm), :] * cw[j:j + 1]
    act = _silu(acc)
    nqk = DN_HEADS * DN_DK
    for hd in range(DN_HEADS):
        sl = slice(hd * DN_DK, (hd + 1) * DN_DK)
        qb = act[:, sl]
        qb = qb * lax.rsqrt(jnp.sum(qb * qb, axis=-1, keepdims=True) + EPS) * (DN_DK ** -0.5)
        dq_ref[0, :, sl] = qb
        kb = act[:, nqk + hd * DN_DK: nqk + (hd + 1) * DN_DK]
        kb = kb * lax.rsqrt(jnp.sum(kb * kb, axis=-1, keepdims=True) + EPS)
        dk_ref[0, :, sl] = kb
        dkt_ref[0, sl, :] = kb.T
    dv_ref[0] = act[:, 2 * nqk:]
    z_ref[0] = _dot(hb, wz_ref[...])

    hlo = (h - hb.astype(f32)).astype(bf16)
    wab = wab_ref[...]
    p1 = _dot(hb, wab)
    p2 = _dot(hlo, wab)
    ab = p1 + pltpu.roll(p1, LANES - 16, 1) + p2
    gpar = gpar_ref[...]
    lane16 = lax.broadcasted_iota(i32, (tm, LANES), 1)
    gval = jnp.where(lane16 < 8, -gpar[0:1] * _softplus(ab + gpar[1:2]), _sigmoid(ab))
    g3 = _split3(gval)
    lo_tri = lo_ref[...]
    up_tri = up_ref[...]
    pre = _dot(lo_tri, g3[0]) + (_dot(lo_tri, g3[1]) + _dot(lo_tri, g3[2]))
    suf = _dot(up_tri, g3[0]) + (_dot(up_tri, g3[1]) + _dot(up_tri, g3[2]))
    gcol_ref[0] = jnp.where(lane16 < 4, pre, jnp.where(lane16 < 8, suf, gval))

    wabt = wabt_ref[...]
    r1 = _dot_nt(wabt, hb)
    r2 = _dot_nt(wabt[:16], hlo)
    abt = r1[:16] + r1[16:] + r2
    gpt = gpart_ref[...]
    row16 = lax.broadcasted_iota(i32, (16, tm), 0)
    gvt = jnp.where(row16 < 8, -gpt[:, 0:1] * _softplus(abt + gpt[:, 1:2]), _sigmoid(abt))
    t3 = _split3(gvt)
    pre_t = _dot(t3[0], up_tri) + (_dot(t3[1], up_tri) + _dot(t3[2], up_tri))
    suf_t = _dot(t3[0], lo_tri) + (_dot(t3[1], lo_tri) + _dot(t3[2], lo_tri))
    grow_ref[0] = jnp.where(row16 < 4, pre_t, jnp.where(row16 < 8, suf_t, gvt))


def _prep_weights(w_in, w_q_b, w_kv_b, q_norm, k_norm, dn_conv, dn_a_log, dn_dt_bias):
    D = w_in.shape[0]
    i0 = MLA_Q_LORA
    i1 = i0 + MLA_KV_LORA
    i2 = i1 + MLA_ROPE
    nqkv = DN_HEADS * (2 * DN_DK + DN_DV)
    i3 = i2 + nqkv
    i4 = i3 + DN_HEADS * DN_DV
    wqa = w_in[:, :i0].astype(bf16)
    wkva = w_in[:, i0:i1].astype(bf16)
    wkr = jnp.pad(w_in[:, i1:i2], ((0, 0), (MLA_NOPE, LANES - MLA_QK))).astype(bf16)
    wqkv = w_in[:, i2:i3].astype(bf16)
    wz = w_in[:, i3:i4].astype(bf16)
    wab_f = w_in[:, i4:]
    wab_hi = wab_f.astype(bf16)
    wab_lo = (wab_f - wab_hi.astype(f32)).astype(bf16)
    wab = jnp.pad(jnp.concatenate([wab_hi, wab_lo], axis=1), ((0, 0), (0, LANES - 32)))
    wabt = jnp.concatenate([wab_hi.T, wab_lo.T], axis=0)
    wqb = w_q_b.reshape(MLA_Q_LORA, MLA_HEADS, MLA_QK)
    wqb = jnp.pad(wqb, ((0, 0), (0, 0), (0, LANES - MLA_QK))).reshape(MLA_Q_LORA, MLA_HEADS * LANES).astype(bf16)
    wkv = w_kv_b.reshape(MLA_KV_LORA, MLA_HEADS, MLA_NOPE + MLA_V)
    wkb = jnp.pad(wkv[:, :, :MLA_NOPE], ((0, 0), (0, 0), (0, LANES - MLA_NOPE)))
    wkb = wkb.reshape(MLA_KV_LORA, MLA_HEADS * LANES).astype(bf16)
    wvb = wkv[:, :, MLA_NOPE:].reshape(MLA_KV_LORA, MLA_HEADS * MLA_V).astype(bf16)
    qn = jnp.pad(q_norm, (0, LANES - MLA_QK)).reshape(1, LANES)
    kn = jnp.pad(k_norm, (0, LANES - MLA_QK)).reshape(1, LANES)
    conv = jnp.pad(dn_conv, ((0, 8 - DN_CONV), (0, 0)))
    ea = jnp.exp(dn_a_log.astype(f32)).reshape(-1)
    dtb = dn_dt_bias.astype(f32).reshape(-1)
    gpar = jnp.pad(jnp.stack([ea, dtb]), ((0, 6), (0, LANES - 8)))
    gpart = jnp.pad(jnp.stack([ea, dtb], axis=1), ((0, 8), (0, LANES - 2)))
    return dict(wqa=wqa, wkva=wkva, wkr=wkr, wqkv=wqkv, wz=wz, wab=wab, wabt=wabt, wqb=wqb, wkb=wkb,
                wvb=wvb, qn=qn, kn=kn, conv=conv, gpar=gpar, gpart=gpart)


def _rope_tables(S):
    half = MLA_ROPE // 2
    freq = ROPE_THETA ** (-jnp.arange(half, dtype=f32) / half)
    ang = jnp.arange(S, dtype=f32)[:, None] * freq[None, :]
    cos, sin = jnp.cos(ang), jnp.sin(ang)
    zeros = lambda w: jnp.zeros((S, w), f32)
    cos_t = jnp.concatenate([jnp.ones((S, MLA_NOPE), f32), cos, cos, zeros(LANES - MLA_QK)], axis=1)
    sina = jnp.concatenate([zeros(MLA_NOPE), -sin, zeros(LANES - MLA_NOPE - half)], axis=1)
    sinb = jnp.concatenate([zeros(MLA_NOPE + half), sin, zeros(LANES - MLA_QK)], axis=1)
    return cos_t, sina, sinb


def _inproj(x, sh1, sc1, norm1, q_a_norm, kv_a_norm, pw, rope):
    B, S, D = x.shape
    tm = min(TM_IN, S)
    nt = S // tm
    r8 = tm // 8
    nqkv = DN_HEADS * (2 * DN_DK + DN_DV)
    nh = DN_HEADS * DN_DK

    def full(a):
        return pl.BlockSpec(a.shape, lambda b, i: (0,) * a.ndim)

    tok = lambda w: pl.BlockSpec((1, tm, w), lambda b, i: (b, i, 0))
    in_specs = [
        tok(D),
        pl.BlockSpec((1, 8, D), lambda b, i: (b, jnp.maximum(i * r8 - 1, 0), 0)),
        pl.BlockSpec((1, 8, D), lambda b, i: (b, jnp.minimum((i + 1) * r8, S // 8 - 1), 0)),
        pl.BlockSpec((1, 1, D), lambda b, i: (b, 0, 0)),
        pl.BlockSpec((1, 1, D), lambda b, i: (b, 0, 0)),
    ]
    consts = [norm1.reshape(1, D), pw["wqa"], pw["wkva"], pw["wkr"], pw["wqkv"], pw["wz"], pw["wab"], pw["wabt"],
              q_a_norm.reshape(1, -1), kv_a_norm.reshape(1, -1), pw["wqb"], pw["wkb"], pw["wvb"], pw["qn"], pw["kn"]]
    in_specs += [full(a) for a in consts]
    in_specs += [pl.BlockSpec((tm, LANES), lambda b, i: (i, 0))] * 3
    tail = [pw["conv"], pw["gpar"], pw["gpart"], _chunk_tri(tm, True), _chunk_tri(tm, False)]
    in_specs += [full(a) for a in tail]
    hspec = pl.BlockSpec((1, MLA_HEADS, tm, LANES), lambda b, i: (b, 0, i, 0))
    vspec = pl.BlockSpec((1, MLA_HEADS, V_ROWS, tm), lambda b, i: (b, 0, 0, i))
    out_specs = [hspec, hspec, vspec, tok(nh), tok(nh), tok(nh),
                 pl.BlockSpec((1, nh, tm), lambda b, i: (b, 0, i)), tok(nh), tok(LANES),
                 pl.BlockSpec((1, 16, tm), lambda b, i: (b, 0, i))]
    hshape = jax.ShapeDtypeStruct((B, MLA_HEADS, S, LANES), bf16)
    tshape = jax.ShapeDtypeStruct((B, S, nh), f32)
    vshape = jax.ShapeDtypeStruct((B, MLA_HEADS, V_ROWS, S), bf16)
    out_shape = [hshape, hshape, vshape, tshape, tshape, tshape,
                 jax.ShapeDtypeStruct((B, nh, S), f32), tshape,
                 jax.ShapeDtypeStruct((B, S, LANES), f32), jax.ShapeDtypeStruct((B, 16, S), f32)]
    return pl.pallas_call(
        _inproj_kernel,
        grid=(B, nt),
        in_specs=in_specs,
        out_specs=out_specs,
        out_shape=out_shape,
        scratch_shapes=[pltpu.VMEM((tm + 16, nqkv), f32)],
        compiler_params=_cparams(("parallel", "parallel")),
        name="inproj",
    )(x, x, x, sh1, sc1, *consts, *rope, *tail)


def _attn_kernel(q_ref, k_ref, v_ref, o_ref, s_scr, p_scr, c_scr, m_scr, acc_scr):
    S = k_ref.shape[2]
    tk = s_scr.shape[2]
    nk = S // tk
    nh = q_ref.shape[1]
    qs = [q_ref[0, hd] for hd in range(nh)]
    m_scr[...] = jnp.full(m_scr.shape, -jnp.inf, f32)
    acc_scr[...] = jnp.zeros(acc_scr.shape, f32)

    slabs = range(0, tk, ATTN_SLAB)

    def scores(slot, j):
        for hd in range(nh):
            cmax = None
            for r0 in slabs:
                s = _dot_nt(k_ref[0, hd, pl.ds(j * tk + r0, ATTN_SLAB), :], qs[hd])
                s_scr[slot, hd, pl.ds(r0, ATTN_SLAB), :] = s
                smax = jnp.max(s, axis=0, keepdims=True)
                cmax = smax if cmax is None else jnp.maximum(cmax, smax)
            c_scr[slot, hd] = cmax

    def accumulate(slot, j):
        off = j * tk
        for hd in range(nh):
            m = m_scr[hd]
            m_new = jnp.maximum(m, c_scr[slot, hd])
            for r0 in slabs:
                p_scr[hd, pl.ds(r0, ATTN_SLAB), :] = jnp.exp2(s_scr[slot, hd, pl.ds(r0, ATTN_SLAB), :] - m_new).astype(bf16)
            m_scr[hd] = m_new
            acc_scr[hd] = acc_scr[hd] * jnp.exp2(m - m_new) + _dot(v_ref[0, hd, :, pl.ds(off, tk)], p_scr[hd])

    scores(0, 0)
    for j in range(nk):
        if j + 1 < nk:
            scores((j + 1) % 2, j + 1)
        accumulate(j % 2, j)
    outs = [acc_scr[hd][:MLA_V] / acc_scr[hd][MLA_V:MLA_V + 1] for hd in range(nh)]
    o_ref[0] = jnp.concatenate(outs, axis=0).T.astype(o_ref.dtype)


def _attention(q, k, v):
    B, H, S, _ = q.shape
    tq = min(TQ, S)
    tk = min(TK, S // 2)
    nh = 2
    qspec = pl.BlockSpec((1, nh, tq, LANES), lambda b, h, i: (b, h, i, 0))
    kspec = pl.BlockSpec((1, nh, S, LANES), lambda b, h, i: (b, h, 0, 0))
    vspec = pl.BlockSpec((1, nh, V_ROWS, S), lambda b, h, i: (b, h, 0, 0))
    return pl.pallas_call(
        _attn_kernel,
        grid=(B, H // nh, S // tq),
        in_specs=[qspec, kspec, vspec],
        out_specs=pl.BlockSpec((1, tq, LANES), lambda b, h, i: (b, i, h)),
        out_shape=jax.ShapeDtypeStruct((B, S, H * MLA_V), bf16),
        scratch_shapes=[pltpu.VMEM((2, nh, tk, tq), f32), pltpu.VMEM((nh, tk, tq), bf16),
                        pltpu.VMEM((2, nh, 1, tq), f32),
                        pltpu.VMEM((nh, 1, tq), f32), pltpu.VMEM((nh, V_ROWS, tq), f32)],
        compiler_params=_cparams(("parallel", "parallel", "arbitrary")),
        name="attn",
    )(q, k, v)


def _unit_tri_inverses(Ls, sub_mask):
    C = Ls[0].shape[0]
    r = lax.broadcasted_iota(i32, (C, C), 0)
    c = lax.broadcasted_iota(i32, (C, C), 1)
    diag = r == c

    def with_eye(s, sign):
        hi, lo = (s[0], s[1]) if sign > 0 else (-s[0], -s[1])
        return jnp.where(diag, jnp.ones((), bf16), hi), lo

    sLds = [_split2(jnp.where(sub_mask, L, 0.0)) for L in Ls]
    sLos = [_split2(jnp.where(sub_mask, 0.0, L)) for L in Ls]
    sTs = [with_eye(s, -1) for s in sLds]
    sPs = sLds
    n = 2
    while n < DN_SUB:
        sPs = [_split2(_dot3s(s, s)) for s in sPs]
        sTs = [_split2(_dot3s(sT, with_eye(sP, 1))) for sT, sP in zip(sTs, sPs)]
        n *= 2
    sNs = [_split2(_dot3s(sT, sLo)) for sT, sLo in zip(sTs, sLos)]
    sN2s = [_split2(_dot3s(s, s)) for s in sNs]
    sMs = [_split2(_dot3s(with_eye(sN, -1), with_eye(sN2, 1))) for sN, sN2 in zip(sNs, sN2s)]
    return [_dot3s(sM, sT) for sM, sT in zip(sMs, sTs)]


def _dot3s(sa, sb):
    lhs = jnp.concatenate([sa[0], sa[1], sa[0]], axis=1)
    rhs = jnp.concatenate([sb[0], sb[0], sb[1]], axis=0)
    return _dot(lhs, rhs)


def _dot3k(a, b):
    return _dot3s(_split2(a), _split2(b))


def _gdn_prep_kernel(q_ref, k_ref, v_ref, kt_ref, gcol_ref, grow_ref,
                     uf_ref, wf_ref, qf_ref, af_ref, ktf_ref, ub_ref, wb_ref, qb_ref, ab_ref, ktb_ref, dl_ref):
    C = DN_CHUNK
    nch = q_ref.shape[1] // C
    r = lax.broadcasted_iota(i32, (C, C), 0)
    c = lax.broadcasted_iota(i32, (C, C), 1)
    sub_mask = (r // DN_SUB) == (c // DN_SUB)
    outs = ((uf_ref, wf_ref, qf_ref, af_ref, ktf_ref), (ub_ref, wb_ref, qb_ref, ab_ref, ktb_ref))
    chains = [(ch, hd) for ch in range(nch) for hd in range(DN_HEADS)]
    kks, qks = {}, {}
    for ch, hd in chains:
        rs = slice(ch * C, (ch + 1) * C)
        hs = slice(hd * DN_DK, (hd + 1) * DN_DK)
        kbf = k_ref[0, rs, hs].astype(bf16)
        kks[ch, hd] = _dot_nt(kbf, kbf)
        qks[ch, hd] = _dot_nt(q_ref[0, rs, hs].astype(bf16), kbf)
    for ch in range(nch):
        rs = slice(ch * C, (ch + 1) * C)
        dls = []
        for d, reverse in enumerate((False, True)):
            for hd in range(DN_HEADS):
                gr = grow_ref[0, 4 * d + hd: 4 * d + hd + 1, rs]
                g_last = gr[:, 0:1] if reverse else gr[:, C - 1:C]
                dls.append(jnp.broadcast_to(jnp.exp(g_last), (1, LANES)))
        dl_ref[0, ch] = jnp.concatenate(dls, axis=0)

    full = [(ch, hd, d) for ch, hd in chains for d in range(2)]
    Ls, rhss = [], []
    for ch, hd, d in full:
        rs = slice(ch * C, (ch + 1) * C)
        hs = slice(hd * DN_DK, (hd + 1) * DN_DK)
        incl = (c >= r) if d else (c <= r)
        gc = gcol_ref[0, rs, 4 * d + hd: 4 * d + hd + 1]
        beta = gcol_ref[0, rs, 8 + 4 * d + hd: 8 + 4 * d + hd + 1]
        gr = grow_ref[0, 4 * d + hd: 4 * d + hd + 1, rs]
        dec = jnp.exp(jnp.where(incl, gc - gr, -jnp.inf))
        outs[d][3][0, hd, rs, :] = (qks[ch, hd] * dec).astype(bf16)
        Ls.append(jnp.where(r == c, 0.0, beta * kks[ch, hd] * dec))
        g_last = gr[:, 0:1] if d else gr[:, C - 1:C]
        outs[d][4][0, ch, hs, :] = (kt_ref[0, hs, rs] * jnp.exp(g_last - gr)).astype(bf16)
        eg = jnp.exp(gc)
        outs[d][2][0, rs, hs] = (q_ref[0, rs, hs] * eg).astype(bf16)
        rhss.append(jnp.concatenate([v_ref[0, rs, hs] * beta, k_ref[0, rs, hs] * (beta * eg)], axis=1))
    Ts = _unit_tri_inverses(Ls, sub_mask)
    sols = [_dot3k(T, rhs) for T, rhs in zip(Ts, rhss)]
    for (ch, hd, d), sol in zip(full, sols):
        rs = slice(ch * C, (ch + 1) * C)
        hs = slice(hd * DN_DK, (hd + 1) * DN_DK)
        outs[d][0][0, rs, hs] = sol[:, :DN_DV]
        outs[d][1][0, rs, hs] = sol[:, DN_DV:].astype(bf16)


def _gdn_scan_kernel(uf_ref, wf_ref, qf_ref, af_ref, ktf_ref, dlf_ref,
                     ub_ref, wb_ref, qb_ref, ab_ref, ktb_ref, dlb_ref, of_ref, ob_ref, s_ref):
    @pl.when(pl.program_id(1) == 0)
    def _():
        s_ref[...] = jnp.zeros_like(s_ref)

    C = DN_CHUNK
    nch = uf_ref.shape[1] // C
    dirs = ((uf_ref, wf_ref, qf_ref, af_ref, ktf_ref, dlf_ref, of_ref),
            (ub_ref, wb_ref, qb_ref, ab_ref, ktb_ref, dlb_ref, ob_ref))
    chains = [(d, hd) for d in range(2) for hd in range(DN_HEADS)]
    sts = [s_ref[d, hd] for d, hd in chains]
    for step in range(nch):
        def sl(d, hd):
            ch = step if d == 0 else nch - 1 - step
            return ch, slice(ch * C, (ch + 1) * C), slice(hd * DN_DK, (hd + 1) * DN_DK)

        wss = []
        for (d, hd), st in zip(chains, sts):
            ch, rs, hs = sl(d, hd)
            wq = jnp.concatenate([dirs[d][1][0, rs, hs], dirs[d][2][0, rs, hs]], axis=0)
            wss.append(_dot(wq, st.astype(bf16)))
        vnbs = []
        for (d, hd), ws in zip(chains, wss):
            ch, rs, hs = sl(d, hd)
            vnbs.append((dirs[d][0][0, rs, hs] - ws[:C]).astype(bf16))
        new = []
        for (d, hd), st, ws, vnb in zip(chains, sts, wss, vnbs):
            ch, rs, hs = sl(d, hd)
            dirs[d][6][0, rs, hs] = ws[C:] + _dot(dirs[d][3][0, hd, rs, :], vnb)
            dl = dirs[d][5][0, ch, d * DN_HEADS + hd: d * DN_HEADS + hd + 1, :]
            new.append(st * dl + _dot(dirs[d][4][0, ch, hs, :], vnb))
        sts = new
    for (d, hd), st in zip(chains, sts):
        s_ref[d, hd] = st


def _gdn(dq, dk, dv, dkt, gcol, grow):
    B, S, W = dq.shape
    C = DN_CHUNK
    rows = min(GDN_ROWS, S)
    n = S // rows
    nch = rows // C
    tok = pl.BlockSpec((1, rows, W), lambda b, i: (b, i, 0))
    aspec = pl.BlockSpec((1, DN_HEADS, rows, C), lambda b, i: (b, 0, i, 0))
    kspec = pl.BlockSpec((1, nch, W, C), lambda b, i: (b, i, 0, 0))
    dspec = pl.BlockSpec((1, nch, 2 * DN_HEADS, LANES), lambda b, i: (b, i, 0, 0))
    t32 = jax.ShapeDtypeStruct((B, S, W), f32)
    t16 = jax.ShapeDtypeStruct((B, S, W), bf16)
    ashape = jax.ShapeDtypeStruct((B, DN_HEADS, S, C), bf16)
    kshape = jax.ShapeDtypeStruct((B, S // C, W, C), bf16)
    per_dir_specs = [tok, tok, tok, aspec, kspec]
    per_dir_shapes = [t32, t16, t16, ashape, kshape]
    prep = pl.pallas_call(
        _gdn_prep_kernel,
        grid=(B, n),
        in_specs=[tok, tok, tok, pl.BlockSpec((1, W, rows), lambda b, i: (b, 0, i)),
                  pl.BlockSpec((1, rows, LANES), lambda b, i: (b, i, 0)),
                  pl.BlockSpec((1, 16, rows), lambda b, i: (b, 0, i))],
        out_specs=per_dir_specs * 2 + [dspec],
        out_shape=per_dir_shapes * 2 + [jax.ShapeDtypeStruct((B, S // C, 2 * DN_HEADS, LANES), f32)],
        compiler_params=_cparams(("parallel", "parallel")),
        name="gdn_prep",
    )(dq, dk, dv, dkt, gcol, grow)
    fwd, bwd, dl = prep[:5], prep[5:10], prep[10]

    rows_s = min(GDN_SCAN_ROWS, S)
    ns = S // rows_s
    nchs = rows_s // C

    def specs(rev):
        blk = (lambda i: ns - 1 - i) if rev else (lambda i: i)
        return [pl.BlockSpec((1, rows_s, W), lambda b, i: (b, blk(i), 0))] * 3 + [
            pl.BlockSpec((1, DN_HEADS, rows_s, C), lambda b, i: (b, 0, blk(i), 0)),
            pl.BlockSpec((1, nchs, W, C), lambda b, i: (b, blk(i), 0, 0)),
            pl.BlockSpec((1, nchs, 2 * DN_HEADS, LANES), lambda b, i: (b, blk(i), 0, 0))]

    return pl.pallas_call(
        _gdn_scan_kernel,
        grid=(B, ns),
        in_specs=specs(False) + specs(True),
        out_specs=[specs(False)[0], specs(True)[0]],
        out_shape=[t32, t32],
        scratch_shapes=[pltpu.VMEM((2, DN_HEADS, DN_DK, DN_DV), f32)],
        compiler_params=_cparams(("parallel", "arbitrary")),
        name="gdn_scan",
    )(*fwd, dl, *bwd, dl)


def _pack_bf16_pairs(y):
    w = y.shape[1] // 2
    lo = pltpu.bitcast(y[:, :w].astype(bf16).astype(f32), u32)
    hi = pltpu.bitcast(y[:, w:].astype(bf16).astype(f32), u32)
    return (lo >> 16) | (hi & jnp.uint32(0xFFFF0000))


def _unpack_bf16_pairs(p):
    lo = pltpu.bitcast(p << 16, f32)
    hi = pltpu.bitcast(p & jnp.uint32(0xFFFF0000), f32)
    return jnp.concatenate([lo, hi], axis=1)


ROW_SLABS = 4


def _store_rows(ref, packed):
    n = packed.shape[0]
    for j in range(ROW_SLABS):
        ref[pl.ds(j, n, stride=ROW_SLABS), :] = packed[:, j * LANES:(j + 1) * LANES]


def _load_rows(ref, n):
    return jnp.concatenate([ref[pl.ds(j, n, stride=ROW_SLABS), :] for j in range(ROW_SLABS)], axis=1)


def _outproj_kernel(x_ref, om_ref, of_ref, ob_ref, z_ref, gt_ref, sc_ref, sh_ref,
                    wo_ref, dnn_ref, n2_ref, wr_ref, br_ref, below_ref,
                    x1_ref, hp_ref, meta_ref, metat_ref, cnt_ref, carry_ref):
    first = (pl.program_id(0) == 0) & (pl.program_id(1) == 0)

    @pl.when(first)
    def _():
        carry_ref[...] = jnp.zeros_like(carry_ref)

    tm = x_ref.shape[1]
    o = of_ref[0] + ob_ref[0]
    z = z_ref[0]
    dnn = dnn_ref[...]
    parts = []
    for hd in range(DN_HEADS):
        sl = slice(hd * DN_DV, (hd + 1) * DN_DV)
        ob = o[:, sl]
        ob = ob * lax.rsqrt(jnp.mean(ob * ob, axis=-1, keepdims=True) + EPS) * dnn
        parts.append(ob * _silu(z[:, sl]))
    odn = jnp.concatenate(parts, axis=1).astype(bf16)
    nm = om_ref.shape[2]
    mixed = _dot(om_ref[0], wo_ref[pl.ds(0, nm), :]) + _dot(odn, wo_ref[pl.ds(nm, odn.shape[1]), :])
    x1 = x_ref[0] + gt_ref[0] * mixed
    x1_ref[0] = x1
    h2 = x1 * lax.rsqrt(jnp.mean(x1 * x1, axis=-1, keepdims=True) + EPS) * n2_ref[...]
    h2 = h2 * (1.0 + sc_ref[0]) + sh_ref[0]
    _store_rows(hp_ref, _pack_bf16_pairs(h2))

    hh, hl = _split2(h2)
    wr = wr_ref[...]
    p1 = _dot(hh, wr)
    p2 = _dot(hl, wr[:, :LANES])
    lane = lax.broadcasted_iota(i32, (tm, LANES), 1)
    logits = p1[:, :LANES] + p1[:, LANES:] + p2 + br_ref[...]
    logits = jnp.where(lane < N_EXPERTS, logits, -jnp.inf)
    vals, idxs = [], []
    work = logits
    for _ in range(TOP_K):
        mx = jnp.max(work, axis=-1, keepdims=True)
        ix = jnp.min(jnp.where(work == mx, lane, LANES), axis=-1, keepdims=True)
        vals.append(mx)
        idxs.append(ix)
        work = jnp.where(lane == ix, -jnp.inf, work)
    es = [jnp.exp(vv - vals[0]) for vv in vals]
    den = es[0] + es[1] + es[2] + es[3]
    multihot = jnp.where(work != logits, 1.0, 0.0)
    prefix = _dot(below_ref[...], multihot.astype(bf16)) + carry_ref[0:1]
    meta = jnp.zeros((tm, LANES), f32)
    for kk in range(TOP_K):
        rank = jnp.sum(jnp.where(lane == idxs[kk], prefix, 0.0), axis=-1, keepdims=True)
        meta = jnp.where(lane == kk, idxs[kk].astype(f32), meta)
        meta = jnp.where(lane == TOP_K + kk, rank, meta)
        meta = jnp.where(lane == 2 * TOP_K + kk, es[kk] / den, meta)
    meta_ref[...] = meta
    metat_ref[...] = meta.T[:metat_ref.shape[0]]
    carry = carry_ref[...] + jnp.sum(multihot, axis=0, keepdims=True)
    carry_ref[...] = carry
    cnt_ref[...] = carry


def _outproj(x, o_mla, o_f, o_b, z, gt1, sc2, sh2, w_o, dn_out_norm, norm2, w_router, b_router):
    B, S, D = x.shape
    tm = min(TM_OUT, S)
    nt = S // tm
    T = B * S
    nh = o_f.shape[2]
    wr_hi = w_router.astype(bf16)
    wr_lo = (w_router - wr_hi.astype(f32)).astype(bf16)
    zpad = jnp.zeros((D, LANES - N_EXPERTS), bf16)
    wr = jnp.concatenate([wr_hi, zpad, wr_lo, zpad], axis=1)
    br = jnp.pad(b_router, (0, LANES - N_EXPERTS)).reshape(1, LANES)
    below = jnp.tril(jnp.ones((tm, tm), bf16), -1)
    consts = [w_o.astype(bf16), dn_out_norm.reshape(1, -1), norm2.reshape(1, D), wr, br, below]

    def full(a):
        return pl.BlockSpec(a.shape, lambda b, i: (0,) * a.ndim)

    tok = lambda w: pl.BlockSpec((1, tm, w), lambda b, i: (b, i, 0))
    vec = pl.BlockSpec((1, 1, D), lambda b, i: (b, 0, 0))
    flat = lambda w: pl.BlockSpec((tm, w), lambda b, i: (b * nt + i, 0))
    return pl.pallas_call(
        _outproj_kernel,
        grid=(B, nt),
        in_specs=[tok(D), tok(o_mla.shape[2]), tok(nh), tok(nh), tok(nh), vec, vec, vec] + [full(a) for a in consts],
        out_specs=[tok(D), pl.BlockSpec((tm * ROW_SLABS, LANES), lambda b, i: (b * nt + i, 0)), flat(LANES),
                   pl.BlockSpec((2 * TOP_K, tm), lambda b, i: (0, b * nt + i)),
                   pl.BlockSpec((8, LANES), lambda b, i: (0, 0))],
        out_shape=[jax.ShapeDtypeStruct((B, S, D), f32), jax.ShapeDtypeStruct((T * ROW_SLABS, LANES), u32),
                   jax.ShapeDtypeStruct((T, LANES), f32), jax.ShapeDtypeStruct((2 * TOP_K, T), f32),
                   jax.ShapeDtypeStruct((8, LANES), f32)],
        scratch_shapes=[pltpu.VMEM((8, LANES), f32)],
        compiler_params=_cparams(("arbitrary", "arbitrary")),
        name="outproj",
    )(x, o_mla, o_f, o_b, z, gt1, sc2, sh2, *consts)


DMA_UNROLL = 2


def _row_copy(src_ref, s, dst_ref, d, sem):
    return pltpu.make_async_copy(src_ref.at[pl.ds(pl.multiple_of(s * ROW_SLABS, ROW_SLABS), ROW_SLABS)],
                                 dst_ref.at[pl.ds(pl.multiple_of(d * ROW_SLABS, ROW_SLABS), ROW_SLABS)], sem)


def _dispatch_kernel(ends_ref, dest_ref, hp_ref, xb_ref, zero_ref, sem, zsem):
    n = dest_ref.shape[0]

    @pl.when(pl.program_id(0) == 0)
    def _():
        zero_ref[...] = jnp.zeros_like(zero_ref)
        rows = zero_ref.shape[0]

        def zero_copy(e):
            start = pl.multiple_of(ends_ref[e] * ROW_SLABS - rows, ROW_SLABS * 8)
            return pltpu.make_async_copy(zero_ref, xb_ref.at[pl.ds(start, rows)], zsem)

        nb = xb_ref.shape[0] // rows
        n_used = ends_ref[N_EXPERTS]

        def tail_copy(j):
            return pltpu.make_async_copy(zero_ref, xb_ref.at[pl.ds(j * rows, rows)], zsem)

        for e in range(N_EXPERTS):
            @pl.when(ends_ref[e] >= 0)
            def _():
                zero_copy(e).start()
        for j in range(nb - N_EXPERTS, nb):
            @pl.when(j >= n_used)
            def _():
                tail_copy(j).start()
        for e in range(N_EXPERTS):
            @pl.when(ends_ref[e] >= 0)
            def _():
                zero_copy(e).wait()
        for j in range(nb - N_EXPERTS, nb):
            @pl.when(j >= n_used)
            def _():
                tail_copy(j).wait()

    def issue(t, carry):
        for kk in range(TOP_K):
            _row_copy(hp_ref, t, xb_ref, dest_ref[t * TOP_K + kk], sem).start(priority=kk % 2)
        return carry

    lax.fori_loop(0, n // TOP_K, issue, 0, unroll=DMA_UNROLL)
    for _ in range(TOP_K):
        pltpu.make_async_copy(hp_ref, xb_ref.at[pl.ds(0, hp_ref.shape[0])], sem).wait()


def _dispatch(hp, dest, ends, P):
    T = hp.shape[0] // ROW_SLABS
    tt = min(TT_DISPATCH, T)
    grid_spec = pltpu.PrefetchScalarGridSpec(
        num_scalar_prefetch=1,
        grid=(T // tt,),
        in_specs=[pl.BlockSpec((tt * TOP_K,), lambda i, ends: (i,), memory_space=pltpu.SMEM),
                  pl.BlockSpec((tt * ROW_SLABS, LANES), lambda i, ends: (i, 0))],
        out_specs=pl.BlockSpec(memory_space=pl.ANY),
        scratch_shapes=[pltpu.VMEM((MOE_BLOCK * ROW_SLABS, LANES), u32),
                        pltpu.SemaphoreType.DMA(()), pltpu.SemaphoreType.DMA(())],
    )
    return pl.pallas_call(
        _dispatch_kernel,
        grid_spec=grid_spec,
        out_shape=jax.ShapeDtypeStruct((P * ROW_SLABS, LANES), u32),
        compiler_params=pltpu.CompilerParams(dimension_semantics=("arbitrary",), has_side_effects=True),
        name="dispatch",
    )(ends, dest, hp)


def _expert_kernel(be_ref, nb_ref, x_ref, wg_ref, bg_ref, wu_ref, bu_ref, wd_ref, bd_ref, y_ref):
    b = pl.program_id(0)

    @pl.when(b < nb_ref[0])
    def _():
        subs = [pl.ds(h * MOE_SUB * ROW_SLABS, MOE_SUB * ROW_SLABS) for h in range(MOE_BLOCK // MOE_SUB)]
        xs = [_unpack_bf16_pairs(_load_rows(x_ref.at[sl], MOE_SUB)).astype(bf16) for sl in subs]
        gts = [jnp.minimum(_dot(x, wg_ref[0]) + bg_ref[0], SWIGLU_LIMIT) for x in xs]
        ups = [jnp.clip(_dot(x, wu_ref[0]) + bu_ref[0], -SWIGLU_LIMIT, SWIGLU_LIMIT) for x in xs]
        acts = [((up + 1.0) * gt * _sigmoid(SWIGLU_ALPHA * gt)).astype(bf16) for gt, up in zip(gts, ups)]
        ys = [_dot(act, wd_ref[0]) + bd_ref[0] for act in acts]
        for sl, y in zip(subs, ys):
            _store_rows(y_ref.at[sl], _pack_bf16_pairs(y))

    @pl.when(b >= nb_ref[0])
    def _():
        y_ref[...] = jnp.zeros_like(y_ref)


def _experts(xb, block_expert, n_used, wg, bg, wu, bu, wd, bd):
    E, D, F = wg.shape
    nb = xb.shape[0] // (MOE_BLOCK * ROW_SLABS)
    wspec = lambda r, c: pl.BlockSpec((1, r, c), lambda b, be, nu: (be[b], 0, 0))
    xspec = pl.BlockSpec((MOE_BLOCK * ROW_SLABS, LANES), lambda b, be, nu: (b, 0))
    grid_spec = pltpu.PrefetchScalarGridSpec(
        num_scalar_prefetch=2,
        grid=(nb,),
        in_specs=[xspec, wspec(D, F), wspec(1, F), wspec(D, F), wspec(1, F), wspec(F, D), wspec(1, D)],
        out_specs=xspec,
    )
    return pl.pallas_call(
        _expert_kernel,
        grid_spec=grid_spec,
        out_shape=jax.ShapeDtypeStruct(xb.shape, u32),
        compiler_params=_cparams(("arbitrary",)),
        name="experts",
    )(block_expert, n_used, xb, wg, bg.reshape(E, 1, F), wu, bu.reshape(E, 1, F), wd, bd.reshape(E, 1, D))


def _combine_kernel(dest_ref, dnext_ref, yb_ref, meta_ref, x1_ref, gt_ref, o_ref, buf_ref, sem):
    n = dest_ref.shape[0]
    tt = n // TOP_K
    g = pl.program_id(0)
    slot = g % 2

    def gather(idx_ref, s):
        def issue(t, carry):
            for kk in range(TOP_K):
                _row_copy(yb_ref, idx_ref[t * TOP_K + kk], buf_ref.at[s, kk], t, sem.at[s]).start(priority=kk % 2)
            return carry

        lax.fori_loop(0, tt, issue, 0, unroll=DMA_UNROLL)

    @pl.when(g == 0)
    def _():
        gather(dest_ref, slot)

    @pl.when(g + 1 < pl.num_programs(0))
    def _():
        gather(dnext_ref, 1 - slot)

    for kk in range(TOP_K):
        pltpu.make_async_copy(yb_ref.at[pl.ds(0, tt * ROW_SLABS)], buf_ref.at[slot, kk], sem.at[slot]).wait()
    meta = meta_ref[...]
    moe = jnp.zeros((tt, x1_ref.shape[2]), f32)
    for kk in range(TOP_K):
        gate = meta[:, 2 * TOP_K + kk: 2 * TOP_K + kk + 1]
        moe = moe + gate * _unpack_bf16_pairs(_load_rows(buf_ref.at[slot, kk], tt))
    o_ref[0] = x1_ref[0] + gt_ref[0] * moe


def _combine(yb, dest, meta, x1, gt2):
    B, S, D = x1.shape
    tt = min(TT_COMBINE, S)
    nt = S // tt
    ng = B * nt
    return pl.pallas_call(
        _combine_kernel,
        grid=(ng,),
        in_specs=[pl.BlockSpec((tt * TOP_K,), lambda g: (g,), memory_space=pltpu.SMEM),
                  pl.BlockSpec((tt * TOP_K,), lambda g: (jnp.minimum(g + 1, ng - 1),), memory_space=pltpu.SMEM),
                  pl.BlockSpec(memory_space=pl.ANY),
                  pl.BlockSpec((tt, LANES), lambda g: (g, 0)),
                  pl.BlockSpec((1, tt, D), lambda g: (g // nt, g % nt, 0)),
                  pl.BlockSpec((1, 1, D), lambda g: (g // nt, 0, 0))],
        out_specs=pl.BlockSpec((1, tt, D), lambda g: (g // nt, g % nt, 0)),
        out_shape=jax.ShapeDtypeStruct((B, S, D), f32),
        scratch_shapes=[pltpu.VMEM((2, TOP_K, tt * ROW_SLABS, LANES), u32), pltpu.SemaphoreType.DMA((2,))],
        compiler_params=_cparams(("arbitrary",)),
        name="combine",
    )(dest, dest, yb, meta, x1, gt2)


def _moe(hp, meta, meta_t, cnt, x1, gt2, ew):
    T = meta.shape[0]
    TK_ = T * TOP_K
    nb = -(-TK_ // MOE_BLOCK) + N_EXPERTS
    P = nb * MOE_BLOCK
    counts = cnt[0, :N_EXPERTS].astype(i32)
    padded = (counts + MOE_BLOCK - 1) // MOE_BLOCK * MOE_BLOCK
    cum_padded = jnp.cumsum(padded)
    pstart = cum_padded - padded
    top_idx = meta_t[:TOP_K].astype(i32)
    rank = meta_t[TOP_K:].astype(i32)
    base = jnp.zeros_like(rank)
    for e in range(N_EXPERTS):
        base = jnp.where(top_idx == e, pstart[e], base)
    dest = (base + rank).T.reshape(-1)
    block_start = jnp.arange(nb, dtype=i32) * MOE_BLOCK
    block_expert = jnp.minimum(jnp.sum((cum_padded[None, :] <= block_start[:, None]).astype(i32), axis=1),
                               N_EXPERTS - 1)
    n_used = (cum_padded[-1:] // MOE_BLOCK).astype(i32)
    ends = jnp.concatenate([jnp.where(padded > 0, cum_padded, -1).astype(i32), n_used])
    xb = _dispatch(hp, dest, ends, P)
    yb = _experts(xb, block_expert, n_used, *ew)
    return _combine(yb, dest, meta, x1, gt2)


def _layer(x, c, p, pw, ew, rope):
    B, S, D = x.shape
    mod = _ada(c, p["w_ada"], p["b_ada"]).reshape(B, 6, 1, D)
    sh1, sc1, gt1, sh2, sc2, gt2 = (mod[:, j] for j in range(6))
    q, k, v, dq, dk, dv, dkt, z, gcol, grow = _inproj(
        x, sh1, sc1, p["norm1"], p["q_a_norm"], p["kv_a_norm"], pw, rope)
    o_mla = _attention(q, k, v)
    o_f, o_b = _gdn(dq, dk, dv, dkt, gcol, grow)
    x1, hp, meta, meta_t, cnt = _outproj(x, o_mla, o_f, o_b, z, gt1, sc2, sh2, p["w_o"], p["dn_out_norm"],
                                         p["norm2"], p["w_router"], p["b_router"])
    return _moe(hp, meta, meta_t, cnt, x1, gt2, ew)


def kernel(x_prompt, x_sample, c_prompt, c_sample, w_ada, b_ada, norm1, w_in, q_a_norm, w_q_b, kv_a_norm, w_kv_b, q_norm, k_norm, dn_conv, dn_a_log, dn_dt_bias, dn_out_norm, w_o, norm2, w_router, b_router, w_gate, b_gate, w_up, b_up, w_down, b_down):
    y_prompt, y_sample = x_prompt, x_sample
    depth = w_ada.shape[0]
    for l in range(depth):
        p = {"w_ada": w_ada[l], "b_ada": b_ada[l], "norm1": norm1[l], "q_a_norm": q_a_norm[l],
             "kv_a_norm": kv_a_norm[l], "dn_out_norm": dn_out_norm[l], "w_o": w_o[l], "norm2": norm2[l],
             "w_router": w_router[l], "b_router": b_router[l]}
        pw = _prep_weights(w_in[l], w_q_b[l], w_kv_b[l], q_norm[l], k_norm[l], dn_conv[l], dn_a_log[l],
                           dn_dt_bias[l])
        ew = (w_gate[l].astype(bf16), b_gate[l], w_up[l].astype(bf16), b_up[l], w_down[l].astype(bf16), b_down[l])
        y_prompt = _layer(y_prompt, c_prompt, p, pw, ew, _rope_tables(y_prompt.shape[1]))
        y_sample = _layer(y_sample, c_sample, p, pw, ew, _rope_tables(y_sample.shape[1]))
    return (y_prompt, y_sample)
```

```python
import math

import jax
import jax.numpy as jnp
from jax import lax
from jax.experimental import pallas as pl
from jax.experimental.pallas import tpu as pltpu

f32 = jnp.float32
bf16 = jnp.bfloat16
u32 = jnp.uint32
i32 = jnp.int32

LANES = 128
VMEM_LIMIT = 56 * 1024 * 1024

MLA_HEADS = 8
MLA_Q_LORA = 384
MLA_KV_LORA = 256
MLA_NOPE = 64
MLA_ROPE = 32
MLA_QK = MLA_NOPE + MLA_ROPE
MLA_V = 64
ROPE_THETA = 10000.0
DN_HEADS = 4
DN_DK = 128
DN_DV = 128
DN_CONV = 5
DN_CHUNK = 64
DN_SUB = 16
N_EXPERTS = 32
TOP_K = 4
SWIGLU_LIMIT = 7.0
SWIGLU_ALPHA = 1.702
MOE_BLOCK = 512
MOE_SUB = 256
EPS = 1e-6

TM_IN = 512
TM_OUT = 1024
TQ = 256
TK = 512
GDN_ROWS = 128
GDN_SCAN_ROWS = 1024
TT_DISPATCH = 512
TT_COMBINE = 256


def _cparams(sem):
    return pltpu.CompilerParams(dimension_semantics=sem, vmem_limit_bytes=VMEM_LIMIT)


def _split2(x):
    hi = x.astype(bf16)
    lo = (x - hi.astype(f32)).astype(bf16)
    return hi, lo


def _split3(x):
    hi = x.astype(bf16)
    r = x - hi.astype(f32)
    mid = r.astype(bf16)
    lo = (r - mid.astype(f32)).astype(bf16)
    return hi, mid, lo


def _dot(a, b):
    return jnp.dot(a, b, preferred_element_type=f32)


def _dot_nt(a, b):
    return lax.dot_general(a, b, (((1,), (1,)), ((), ())), preferred_element_type=f32)


def _dot3(a, b):
    ah, al = _split2(a)
    bh, bl = _split2(b)
    return _dot(ah, bh) + (_dot(al, bh) + _dot(ah, bl))


def _sigmoid(x):
    return 1.0 / (1.0 + jnp.exp(-x))


def _silu(x):
    return x * _sigmoid(x)


def _softplus(x):
    return jnp.maximum(x, 0.0) + jnp.log(1.0 + jnp.exp(-jnp.abs(x)))


def _ada_kernel(c_ref, w_ref, b_ref, o_ref):
    c = c_ref[...]
    o_ref[...] = _dot3(_silu(c), w_ref[...]) + b_ref[...]


def _ada(c, w_ada, b_ada):
    B, D = c.shape
    N = w_ada.shape[1]
    cp = jnp.pad(c, ((0, 8 - B), (0, 0)))
    tn = 1024
    out = pl.pallas_call(
        _ada_kernel,
        grid=(N // tn,),
        in_specs=[
            pl.BlockSpec((8, D), lambda j: (0, 0)),
            pl.BlockSpec((D, tn), lambda j: (0, j)),
            pl.BlockSpec((1, tn), lambda j: (0, j)),
        ],
        out_specs=pl.BlockSpec((8, tn), lambda j: (0, j)),
        out_shape=jax.ShapeDtypeStruct((8, N), f32),
        compiler_params=_cparams(("parallel",)),
        name="ada",
    )(cp, w_ada, b_ada.reshape(1, N))
    return out[:B]


def _chunk_tri(n, lower):
    r = lax.broadcasted_iota(i32, (n, n), 0)
    c = lax.broadcasted_iota(i32, (n, n), 1)
    same = (r // DN_CHUNK) == (c // DN_CHUNK)
    tri = (c <= r) if lower else (c >= r)
    return jnp.where(same & tri, 1.0, 0.0).astype(bf16)


def _inproj_kernel(
    x_ref, xp_ref, xn_ref, sh_ref, sc_ref, n1_ref,
    wqa_ref, wkva_ref, wkr_ref, wqkv_ref, wz_ref, wab_ref, wabt_ref,
    qan_ref, kvan_ref, wqb_ref, wkb_ref, wvb_ref, qn_ref, kn_ref,
    cos_ref, sina_ref, sinb_ref, conv_ref, gpar_ref, gpart_ref, lo_ref, up_ref,
    q_ref, k_ref, v_ref, dq_ref, dk_ref, dv_ref, dkt_ref, z_ref, gcol_ref, grow_ref,
    ext_ref,
):
    i = pl.program_id(1)
    ni = pl.num_programs(1)
    tm = x_ref.shape[1]
    scale1 = 1.0 + sc_ref[0]
    shift1 = sh_ref[0]
    n1 = n1_ref[...]

    def modulate(xv):
        y = xv * lax.rsqrt(jnp.mean(xv * xv, axis=-1, keepdims=True) + EPS)
        return y * n1 * scale1 + shift1

    h = modulate(x_ref[0])
    hb = h.astype(bf16)
    hh = modulate(jnp.concatenate([xp_ref[0], xn_ref[0]], axis=0)).astype(bf16)

    qa = _dot(hb, wqa_ref[...])
    qa = qa * lax.rsqrt(jnp.mean(qa * qa, axis=-1, keepdims=True) + EPS) * qan_ref[...]
    kva = _dot(hb, wkva_ref[...])
    kva = kva * lax.rsqrt(jnp.mean(kva * kva, axis=-1, keepdims=True) + EPS) * kvan_ref[...]
    kr = _dot(hb, wkr_ref[...])
    qh = _dot(qa.astype(bf16), wqb_ref[...])
    kvb = kva.astype(bf16)
    kh = _dot(kvb, wkb_ref[...])
    vh = _dot(kvb, wvb_ref[...])
    cos = cos_ref[...]
    sina = sina_ref[...]
    sinb = sinb_ref[...]
    qg = qn_ref[...]
    kg = kn_ref[...]
    lane = lax.broadcasted_iota(i32, (tm, LANES), 1)
    q_scale = MLA_QK ** -0.5 * math.log2(math.e)

    def norm_rope(blk, gain):
        ss = jnp.sum(blk * blk, axis=-1, keepdims=True) * (1.0 / MLA_QK)
        y = blk * lax.rsqrt(ss + EPS) * gain
        return y * cos + pltpu.roll(y, LANES - MLA_ROPE // 2, 1) * sina + pltpu.roll(y, MLA_ROPE // 2, 1) * sinb

    for hd in range(MLA_HEADS):
        sl = slice(hd * LANES, (hd + 1) * LANES)
        q_ref[0, hd] = (norm_rope(qh[:, sl], qg) * q_scale).astype(bf16)
        k_ref[0, hd] = norm_rope(kh[:, sl] + kr, kg).astype(bf16)
        vblk = vh[:, hd * MLA_V:(hd + 1) * MLA_V]
        vpad = jnp.concatenate([vblk, jnp.zeros((tm, LANES - MLA_V), f32)], axis=1)
        v_ref[0, hd] = jnp.where(lane == MLA_V, 1.0, vpad).astype(bf16)

    ext_ref[pl.ds(8, tm), :] = _dot(hb, wqkv_ref[...])
    halo = _dot(hh, wqkv_ref[...])
    ext_ref[pl.ds(0, 8), :] = jnp.where(i == 0, 0.0, halo[:8])
    ext_ref[pl.ds(8 + tm, 8), :] = jnp.where(i == ni - 1, 0.0, halo[8:])
    pad = (DN_CONV - 1) // 2
    cw = conv_ref[...]
    acc = ext_ref[pl.ds(8 - pad, tm), :] * cw[0:1]
    for j in range(1, DN_CONV):
        acc = acc + ext_ref[pl.ds(8 - pad + j, tm), :] * cw[j:j + 1]
    act = _silu(acc)
    nqk = DN_HEADS * DN_DK
    for hd in range(DN_HEADS):
        sl = slice(hd * DN_DK, (hd + 1) * DN_DK)
        qb = act[:, sl]
        qb = qb * lax.rsqrt(jnp.sum(qb * qb, axis=-1, keepdims=True) + EPS) * (DN_DK ** -0.5)
        dq_ref[0, :, sl] = qb
        kb = act[:, nqk + hd * DN_DK: nqk + (hd + 1) * DN_DK]
        kb = kb * lax.rsqrt(jnp.sum(kb * kb, axis=-1, keepdims=True) + EPS)
        dk_ref[0, :, sl] = kb
        dkt_ref[0, sl, :] = kb.T
    dv_ref[0] = act[:, 2 * nqk:]
    z_ref[0] = _dot(hb, wz_ref[...])

    hlo = (h - hb.astype(f32)).astype(bf16)
    wab = wab_ref[...]
    p1 = _dot(hb, wab)
    p2 = _dot(hlo, wab)
    ab = p1 + pltpu.roll(p1, LANES - 16, 1) + p2
    gpar = gpar_ref[...]
    lane16 = lax.broadcasted_iota(i32, (tm, LANES), 1)
    gval = jnp.where(lane16 < 8, -gpar[0:1] * _softplus(ab + gpar[1:2]), _sigmoid(ab))
    g3 = _split3(gval)
    lo_tri = lo_ref[...]
    up_tri = up_ref[...]
    pre = _dot(lo_tri, g3[0]) + (_dot(lo_tri, g3[1]) + _dot(lo_tri, g3[2]))
    suf = _dot(up_tri, g3[0]) + (_dot(up_tri, g3[1]) + _dot(up_tri, g3[2]))
    gcol_ref[0] = jnp.where(lane16 < 4, pre, jnp.where(lane16 < 8, suf, gval))

    wabt = wabt_ref[...]
    r1 = _dot_nt(wabt, hb)
    r2 = _dot_nt(wabt[:16], hlo)
    abt = r1[:16] + r1[16:] + r2
    gpt = gpart_ref[...]
    row16 = lax.broadcasted_iota(i32, (16, tm), 0)
    gvt = jnp.where(row16 < 8, -gpt[:, 0:1] * _softplus(abt + gpt[:, 1:2]), _sigmoid(abt))
    t3 = _split3(gvt)
    pre_t = _dot(t3[0], up_tri) + (_dot(t3[1], up_tri) + _dot(t3[2], up_tri))
    suf_t = _dot(t3[0], lo_tri) + (_dot(t3[1], lo_tri) + _dot(t3[2], lo_tri))
    grow_ref[0] = jnp.where(row16 < 4, pre_t, jnp.where(row16 < 8, suf_t, gvt))


def _prep_weights(w_in, w_q_b, w_kv_b, q_norm, k_norm, dn_conv, dn_a_log, dn_dt_bias):
    D = w_in.shape[0]
    i0 = MLA_Q_LORA
    i1 = i0 + MLA_KV_LORA
    i2 = i1 + MLA_ROPE
    nqkv = DN_HEADS * (2 * DN_DK + DN_DV)
    i3 = i2 + nqkv
    i4 = i3 + DN_HEADS * DN_DV
    wqa = w_in[:, :i0].astype(bf16)
    wkva = w_in[:, i0:i1].astype(bf16)
    wkr = jnp.pad(w_in[:, i1:i2], ((0, 0), (MLA_NOPE, LANES - MLA_QK))).astype(bf16)
    wqkv = w_in[:, i2:i3].astype(bf16)
    wz = w_in[:, i3:i4].astype(bf16)
    wab_f = w_in[:, i4:]
    wab_hi = wab_f.astype(bf16)
    wab_lo = (wab_f - wab_hi.astype(f32)).astype(bf16)
    wab = jnp.pad(jnp.concatenate([wab_hi, wab_lo], axis=1), ((0, 0), (0, LANES - 32)))
    wabt = jnp.concatenate([wab_hi.T, wab_lo.T], axis=0)
    wqb = w_q_b.reshape(MLA_Q_LORA, MLA_HEADS, MLA_QK)
    wqb = jnp.pad(wqb, ((0, 0), (0, 0), (0, LANES - MLA_QK))).reshape(MLA_Q_LORA, MLA_HEADS * LANES).astype(bf16)
    wkv = w_kv_b.reshape(MLA_KV_LORA, MLA_HEADS, MLA_NOPE + MLA_V)
    wkb = jnp.pad(wkv[:, :, :MLA_NOPE], ((0, 0), (0, 0), (0, LANES - MLA_NOPE)))
    wkb = wkb.reshape(MLA_KV_LORA, MLA_HEADS * LANES).astype(bf16)
    wvb = wkv[:, :, MLA_NOPE:].reshape(MLA_KV_LORA, MLA_HEADS * MLA_V).astype(bf16)
    qn = jnp.pad(q_norm, (0, LANES - MLA_QK)).reshape(1, LANES)
    kn = jnp.pad(k_norm, (0, LANES - MLA_QK)).reshape(1, LANES)
    conv = jnp.pad(dn_conv, ((0, 8 - DN_CONV), (0, 0)))
    ea = jnp.exp(dn_a_log.astype(f32)).reshape(-1)
    dtb = dn_dt_bias.astype(f32).reshape(-1)
    gpar = jnp.pad(jnp.stack([ea, dtb]), ((0, 6), (0, LANES - 8)))
    gpart = jnp.pad(jnp.stack([ea, dtb], axis=1), ((0, 8), (0, LANES - 2)))
    return dict(wqa=wqa, wkva=wkva, wkr=wkr, wqkv=wqkv, wz=wz, wab=wab, wabt=wabt, wqb=wqb, wkb=wkb,
                wvb=wvb, qn=qn, kn=kn, conv=conv, gpar=gpar, gpart=gpart)


def _rope_tables(S):
    half = MLA_ROPE // 2
    freq = ROPE_THETA ** (-jnp.arange(half, dtype=f32) / half)
    ang = jnp.arange(S, dtype=f32)[:, None] * freq[None, :]
    cos, sin = jnp.cos(ang), jnp.sin(ang)
    zeros = lambda w: jnp.zeros((S, w), f32)
    cos_t = jnp.concatenate([jnp.ones((S, MLA_NOPE), f32), cos, cos, zeros(LANES - MLA_QK)], axis=1)
    sina = jnp.concatenate([zeros(MLA_NOPE), -sin, zeros(LANES - MLA_NOPE - half)], axis=1)
    sinb = jnp.concatenate([zeros(MLA_NOPE + half), sin, zeros(LANES - MLA_QK)], axis=1)
    return cos_t, sina, sinb


def _inproj(x, sh1, sc1, norm1, q_a_norm, kv_a_norm, pw, rope):
    B, S, D = x.shape
    tm = min(TM_IN, S)
    nt = S // tm
    r8 = tm // 8
    nqkv = DN_HEADS * (2 * DN_DK + DN_DV)
    nh = DN_HEADS * DN_DK

    def full(a):
        return pl.BlockSpec(a.shape, lambda b, i: (0,) * a.ndim)

    tok = lambda w: pl.BlockSpec((1, tm, w), lambda b, i: (b, i, 0))
    in_specs = [
        tok(D),
        pl.BlockSpec((1, 8, D), lambda b, i: (b, jnp.maximum(i * r8 - 1, 0), 0)),
        pl.BlockSpec((1, 8, D), lambda b, i: (b, jnp.minimum((i + 1) * r8, S // 8 - 1), 0)),
        pl.BlockSpec((1, 1, D), lambda b, i: (b, 0, 0)),
        pl.BlockSpec((1, 1, D), lambda b, i: (b, 0, 0)),
    ]
    consts = [norm1.reshape(1, D), pw["wqa"], pw["wkva"], pw["wkr"], pw["wqkv"], pw["wz"], pw["wab"], pw["wabt"],
              q_a_norm.reshape(1, -1), kv_a_norm.reshape(1, -1), pw["wqb"], pw["wkb"], pw["wvb"], pw["qn"], pw["kn"]]
    in_specs += [full(a) for a in consts]
    in_specs += [pl.BlockSpec((tm, LANES), lambda b, i: (i, 0))] * 3
    tail = [pw["conv"], pw["gpar"], pw["gpart"], _chunk_tri(tm, True), _chunk_tri(tm, False)]
    in_specs += [full(a) for a in tail]
    hspec = pl.BlockSpec((1, MLA_HEADS, tm, LANES), lambda b, i: (b, 0, i, 0))
    out_specs = [hspec, hspec, hspec, tok(nh), tok(nh), tok(nh),
                 pl.BlockSpec((1, nh, tm), lambda b, i: (b, 0, i)), tok(nh), tok(LANES),
                 pl.BlockSpec((1, 16, tm), lambda b, i: (b, 0, i))]
    hshape = jax.ShapeDtypeStruct((B, MLA_HEADS, S, LANES), bf16)
    tshape = jax.ShapeDtypeStruct((B, S, nh), f32)
    out_shape = [hshape, hshape, hshape, tshape, tshape, tshape,
                 jax.ShapeDtypeStruct((B, nh, S), f32), tshape,
                 jax.ShapeDtypeStruct((B, S, LANES), f32), jax.ShapeDtypeStruct((B, 16, S), f32)]
    return pl.pallas_call(
        _inproj_kernel,
        grid=(B, nt),
        in_specs=in_specs,
        out_specs=out_specs,
        out_shape=out_shape,
        scratch_shapes=[pltpu.VMEM((tm + 16, nqkv), f32)],
        compiler_params=_cparams(("parallel", "parallel")),
        name="inproj",
    )(x, x, x, sh1, sc1, *consts, *rope, *tail)


def _attn_kernel(q_ref, k_ref, v_ref, o_ref, s_scr, m_scr, acc_scr):
    S = k_ref.shape[2]
    tk = s_scr.shape[3]
    nk = S // tk
    nh = q_ref.shape[1]
    qs = [q_ref[0, hd] for hd in range(nh)]
    m_scr[...] = jnp.full(m_scr.shape, -jnp.inf, f32)
    acc_scr[...] = jnp.zeros(acc_scr.shape, f32)

    def scores(slot, j):
        for hd in range(nh):
            s_scr[slot, hd] = _dot_nt(qs[hd], k_ref[0, hd, pl.ds(j * tk, tk), :])

    def accumulate(slot, j):
        off = j * tk
        for hd in range(nh):
            s = s_scr[slot, hd]
            m = m_scr[hd]
            m_new = jnp.maximum(m, jnp.broadcast_to(jnp.max(s, axis=-1, keepdims=True), m.shape))
            p = jnp.exp2(s - jnp.tile(m_new, (1, tk // LANES)))
            m_scr[hd] = m_new
            acc_scr[hd] = acc_scr[hd] * jnp.exp2(m - m_new) + _dot(p.astype(bf16), v_ref[0, hd, pl.ds(off, tk), :])

    scores(0, 0)
    for j in range(nk):
        if j + 1 < nk:
            scores((j + 1) % 2, j + 1)
        accumulate(j % 2, j)
    outs = [acc_scr[hd][:, :MLA_V] / acc_scr[hd][:, MLA_V:MLA_V + 1] for hd in range(nh)]
    o_ref[0] = jnp.concatenate(outs, axis=1).astype(o_ref.dtype)


def _attention(q, k, v):
    B, H, S, _ = q.shape
    tq = min(TQ, S)
    tk = min(TK, S // 2)
    nh = 2
    qspec = pl.BlockSpec((1, nh, tq, LANES), lambda b, h, i: (b, h, i, 0))
    kspec = pl.BlockSpec((1, nh, S, LANES), lambda b, h, i: (b, h, 0, 0))
    return pl.pallas_call(
        _attn_kernel,
        grid=(B, H // nh, S // tq),
        in_specs=[qspec, kspec, kspec],
        out_specs=pl.BlockSpec((1, tq, LANES), lambda b, h, i: (b, i, h)),
        out_shape=jax.ShapeDtypeStruct((B, S, H * MLA_V), bf16),
        scratch_shapes=[pltpu.VMEM((2, nh, tq, tk), f32), pltpu.VMEM((nh, tq, LANES), f32),
                        pltpu.VMEM((nh, tq, LANES), f32)],
        compiler_params=_cparams(("parallel", "parallel", "arbitrary")),
        name="attn",
    )(q, k, v)


def _unit_tri_inverses(Ls, sub_mask):
    C = Ls[0].shape[0]
    r = lax.broadcasted_iota(i32, (C, C), 0)
    c = lax.broadcasted_iota(i32, (C, C), 1)
    diag = r == c

    def with_eye(s, sign):
        hi, lo = (s[0], s[1]) if sign > 0 else (-s[0], -s[1])
        return jnp.where(diag, jnp.ones((), bf16), hi), lo

    sLds = [_split2(jnp.where(sub_mask, L, 0.0)) for L in Ls]
    sLos = [_split2(jnp.where(sub_mask, 0.0, L)) for L in Ls]
    sTs = [with_eye(s, -1) for s in sLds]
    sPs = sLds
    n = 2
    while n < DN_SUB:
        sPs = [_split2(_dot3s(s, s)) for s in sPs]
        sTs = [_split2(_dot3s(sT, with_eye(sP, 1))) for sT, sP in zip(sTs, sPs)]
        n *= 2
    sNs = [_split2(_dot3s(sT, sLo)) for sT, sLo in zip(sTs, sLos)]
    sN2s = [_split2(_dot3s(s, s)) for s in sNs]
    sMs = [_split2(_dot3s(with_eye(sN, -1), with_eye(sN2, 1))) for sN, sN2 in zip(sNs, sN2s)]
    return [_dot3s(sM, sT) for sM, sT in zip(sMs, sTs)]


def _dot3s(sa, sb):
    lhs = jnp.concatenate([sa[0], sa[1], sa[0]], axis=1)
    rhs = jnp.concatenate([sb[0], sb[0], sb[1]], axis=0)
    return _dot(lhs, rhs)


def _dot3k(a, b):
    return _dot3s(_split2(a), _split2(b))


def _gdn_prep_kernel(q_ref, k_ref, v_ref, kt_ref, gcol_ref, grow_ref,
                     uf_ref, wf_ref, qf_ref, af_ref, ktf_ref, ub_ref, wb_ref, qb_ref, ab_ref, ktb_ref, dl_ref):
    C = DN_CHUNK
    nch = q_ref.shape[1] // C
    r = lax.broadcasted_iota(i32, (C, C), 0)
    c = lax.broadcasted_iota(i32, (C, C), 1)
    sub_mask = (r // DN_SUB) == (c // DN_SUB)
    outs = ((uf_ref, wf_ref, qf_ref, af_ref, ktf_ref), (ub_ref, wb_ref, qb_ref, ab_ref, ktb_ref))
    chains = [(ch, hd) for ch in range(nch) for hd in range(DN_HEADS)]
    kks, qks = {}, {}
    for ch, hd in chains:
        rs = slice(ch * C, (ch + 1) * C)
        hs = slice(hd * DN_DK, (hd + 1) * DN_DK)
        kbf = k_ref[0, rs, hs].astype(bf16)
        kks[ch, hd] = _dot_nt(kbf, kbf)
        qks[ch, hd] = _dot_nt(q_ref[0, rs, hs].astype(bf16), kbf)
    for ch in range(nch):
        rs = slice(ch * C, (ch + 1) * C)
        dls = []
        for d, reverse in enumerate((False, True)):
            for hd in range(DN_HEADS):
                gr = grow_ref[0, 4 * d + hd: 4 * d + hd + 1, rs]
                g_last = gr[:, 0:1] if reverse else gr[:, C - 1:C]
                dls.append(jnp.broadcast_to(jnp.exp(g_last), (1, LANES)))
        dl_ref[0, ch] = jnp.concatenate(dls, axis=0)

    full = [(ch, hd, d) for ch, hd in chains for d in range(2)]
    Ls, rhss = [], []
    for ch, hd, d in full:
        rs = slice(ch * C, (ch + 1) * C)
        hs = slice(hd * DN_DK, (hd + 1) * DN_DK)
        incl = (c >= r) if d else (c <= r)
        gc = gcol_ref[0, rs, 4 * d + hd: 4 * d + hd + 1]
        beta = gcol_ref[0, rs, 8 + 4 * d + hd: 8 + 4 * d + hd + 1]
        gr = grow_ref[0, 4 * d + hd: 4 * d + hd + 1, rs]
        dec = jnp.exp(jnp.where(incl, gc - gr, -jnp.inf))
        outs[d][3][0, hd, rs, :] = (qks[ch, hd] * dec).astype(bf16)
        Ls.append(jnp.where(r == c, 0.0, beta * kks[ch, hd] * dec))
        g_last = gr[:, 0:1] if d else gr[:, C - 1:C]
        outs[d][4][0, ch, hs, :] = (kt_ref[0, hs, rs] * jnp.exp(g_last - gr)).astype(bf16)
        eg = jnp.exp(gc)
        outs[d][2][0, rs, hs] = (q_ref[0, rs, hs] * eg).astype(bf16)
        rhss.append(jnp.concatenate([v_ref[0, rs, hs] * beta, k_ref[0, rs, hs] * (beta * eg)], axis=1))
    Ts = _unit_tri_inverses(Ls, sub_mask)
    sols = [_dot3k(T, rhs) for T, rhs in zip(Ts, rhss)]
    for (ch, hd, d), sol in zip(full, sols):
        rs = slice(ch * C, (ch + 1) * C)
        hs = slice(hd * DN_DK, (hd + 1) * DN_DK)
        outs[d][0][0, rs, hs] = sol[:, :DN_DV]
        outs[d][1][0, rs, hs] = sol[:, DN_DV:].astype(bf16)


def _gdn_scan_kernel(uf_ref, wf_ref, qf_ref, af_ref, ktf_ref, dlf_ref,
                     ub_ref, wb_ref, qb_ref, ab_ref, ktb_ref, dlb_ref, of_ref, ob_ref, s_ref):
    @pl.when(pl.program_id(1) == 0)
    def _():
        s_ref[...] = jnp.zeros_like(s_ref)

    C = DN_CHUNK
    nch = uf_ref.shape[1] // C
    dirs = ((uf_ref, wf_ref, qf_ref, af_ref, ktf_ref, dlf_ref, of_ref),
            (ub_ref, wb_ref, qb_ref, ab_ref, ktb_ref, dlb_ref, ob_ref))
    chains = [(d, hd) for d in range(2) for hd in range(DN_HEADS)]
    sts = [s_ref[d, hd] for d, hd in chains]
    for step in range(nch):
        def sl(d, hd):
            ch = step if d == 0 else nch - 1 - step
            return ch, slice(ch * C, (ch + 1) * C), slice(hd * DN_DK, (hd + 1) * DN_DK)

        wss = []
        for (d, hd), st in zip(chains, sts):
            ch, rs, hs = sl(d, hd)
            wq = jnp.concatenate([dirs[d][1][0, rs, hs], dirs[d][2][0, rs, hs]], axis=0)
            wss.append(_dot(wq, st.astype(bf16)))
        vnbs = []
        for (d, hd), ws in zip(chains, wss):
            ch, rs, hs = sl(d, hd)
            vnbs.append((dirs[d][0][0, rs, hs] - ws[:C]).astype(bf16))
        new = []
        for (d, hd), st, ws, vnb in zip(chains, sts, wss, vnbs):
            ch, rs, hs = sl(d, hd)
            dirs[d][6][0, rs, hs] = ws[C:] + _dot(dirs[d][3][0, hd, rs, :], vnb)
            dl = dirs[d][5][0, ch, d * DN_HEADS + hd: d * DN_HEADS + hd + 1, :]
            new.append(st * dl + _dot(dirs[d][4][0, ch, hs, :], vnb))
        sts = new
    for (d, hd), st in zip(chains, sts):
        s_ref[d, hd] = st


def _gdn(dq, dk, dv, dkt, gcol, grow):
    B, S, W = dq.shape
    C = DN_CHUNK
    rows = min(GDN_ROWS, S)
    n = S // rows
    nch = rows // C
    tok = pl.BlockSpec((1, rows, W), lambda b, i: (b, i, 0))
    aspec = pl.BlockSpec((1, DN_HEADS, rows, C), lambda b, i: (b, 0, i, 0))
    kspec = pl.BlockSpec((1, nch, W, C), lambda b, i: (b, i, 0, 0))
    dspec = pl.BlockSpec((1, nch, 2 * DN_HEADS, LANES), lambda b, i: (b, i, 0, 0))
    t32 = jax.ShapeDtypeStruct((B, S, W), f32)
    t16 = jax.ShapeDtypeStruct((B, S, W), bf16)
    ashape = jax.ShapeDtypeStruct((B, DN_HEADS, S, C), bf16)
    kshape = jax.ShapeDtypeStruct((B, S // C, W, C), bf16)
    per_dir_specs = [tok, tok, tok, aspec, kspec]
    per_dir_shapes = [t32, t16, t16, ashape, kshape]
    prep = pl.pallas_call(
        _gdn_prep_kernel,
        grid=(B, n),
        in_specs=[tok, tok, tok, pl.BlockSpec((1, W, rows), lambda b, i: (b, 0, i)),
                  pl.BlockSpec((1, rows, LANES), lambda b, i: (b, i, 0)),
                  pl.BlockSpec((1, 16, rows), lambda b, i: (b, 0, i))],
        out_specs=per_dir_specs * 2 + [dspec],
        out_shape=per_dir_shapes * 2 + [jax.ShapeDtypeStruct((B, S // C, 2 * DN_HEADS, LANES), f32)],
        compiler_params=_cparams(("parallel", "parallel")),
        name="gdn_prep",
    )(dq, dk, dv, dkt, gcol, grow)
    fwd, bwd, dl = prep[:5], prep[5:10], prep[10]

    rows_s = min(GDN_SCAN_ROWS, S)
    ns = S // rows_s
    nchs = rows_s // C

    def specs(rev):
        blk = (lambda i: ns - 1 - i) if rev else (lambda i: i)
        return [pl.BlockSpec((1, rows_s, W), lambda b, i: (b, blk(i), 0))] * 3 + [
            pl.BlockSpec((1, DN_HEADS, rows_s, C), lambda b, i: (b, 0, blk(i), 0)),
            pl.BlockSpec((1, nchs, W, C), lambda b, i: (b, blk(i), 0, 0)),
            pl.BlockSpec((1, nchs, 2 * DN_HEADS, LANES), lambda b, i: (b, blk(i), 0, 0))]

    return pl.pallas_call(
        _gdn_scan_kernel,
        grid=(B, ns),
        in_specs=specs(False) + specs(True),
        out_specs=[specs(False)[0], specs(True)[0]],
        out_shape=[t32, t32],
        scratch_shapes=[pltpu.VMEM((2, DN_HEADS, DN_DK, DN_DV), f32)],
        compiler_params=_cparams(("parallel", "arbitrary")),
        name="gdn_scan",
    )(*fwd, dl, *bwd, dl)


def _pack_bf16_pairs(y):
    w = y.shape[1] // 2
    lo = pltpu.bitcast(y[:, :w].astype(bf16).astype(f32), u32)
    hi = pltpu.bitcast(y[:, w:].astype(bf16).astype(f32), u32)
    return (lo >> 16) | (hi & jnp.uint32(0xFFFF0000))


def _unpack_bf16_pairs(p):
    lo = pltpu.bitcast(p << 16, f32)
    hi = pltpu.bitcast(p & jnp.uint32(0xFFFF0000), f32)
    return jnp.concatenate([lo, hi], axis=1)


ROW_SLABS = 4


def _store_rows(ref, packed):
    n = packed.shape[0]
    for j in range(ROW_SLABS):
        ref[pl.ds(j, n, stride=ROW_SLABS), :] = packed[:, j * LANES:(j + 1) * LANES]


def _load_rows(ref, n):
    return jnp.concatenate([ref[pl.ds(j, n, stride=ROW_SLABS), :] for j in range(ROW_SLABS)], axis=1)


def _outproj_kernel(x_ref, om_ref, of_ref, ob_ref, z_ref, gt_ref, sc_ref, sh_ref,
                    wo_ref, dnn_ref, n2_ref, wr_ref, br_ref, below_ref,
                    x1_ref, hp_ref, meta_ref, metat_ref, cnt_ref, carry_ref):
    first = (pl.program_id(0) == 0) & (pl.program_id(1) == 0)

    @pl.when(first)
    def _():
        carry_ref[...] = jnp.zeros_like(carry_ref)

    tm = x_ref.shape[1]
    o = of_ref[0] + ob_ref[0]
    z = z_ref[0]
    dnn = dnn_ref[...]
    parts = []
    for hd in range(DN_HEADS):
        sl = slice(hd * DN_DV, (hd + 1) * DN_DV)
        ob = o[:, sl]
        ob = ob * lax.rsqrt(jnp.mean(ob * ob, axis=-1, keepdims=True) + EPS) * dnn
        parts.append(ob * _silu(z[:, sl]))
    odn = jnp.concatenate(parts, axis=1).astype(bf16)
    nm = om_ref.shape[2]
    mixed = _dot(om_ref[0], wo_ref[pl.ds(0, nm), :]) + _dot(odn, wo_ref[pl.ds(nm, odn.shape[1]), :])
    x1 = x_ref[0] + gt_ref[0] * mixed
    x1_ref[0] = x1
    h2 = x1 * lax.rsqrt(jnp.mean(x1 * x1, axis=-1, keepdims=True) + EPS) * n2_ref[...]
    h2 = h2 * (1.0 + sc_ref[0]) + sh_ref[0]
    _store_rows(hp_ref, _pack_bf16_pairs(h2))

    hh, hl = _split2(h2)
    wr = wr_ref[...]
    p1 = _dot(hh, wr)
    p2 = _dot(hl, wr[:, :LANES])
    lane = lax.broadcasted_iota(i32, (tm, LANES), 1)
    logits = p1[:, :LANES] + p1[:, LANES:] + p2 + br_ref[...]
    logits = jnp.where(lane < N_EXPERTS, logits, -jnp.inf)
    vals, idxs = [], []
    work = logits
    for _ in range(TOP_K):
        mx = jnp.max(work, axis=-1, keepdims=True)
        ix = jnp.min(jnp.where(work == mx, lane, LANES), axis=-1, keepdims=True)
        vals.append(mx)
        idxs.append(ix)
        work = jnp.where(lane == ix, -jnp.inf, work)
    es = [jnp.exp(vv - vals[0]) for vv in vals]
    den = es[0] + es[1] + es[2] + es[3]
    multihot = jnp.where(work != logits, 1.0, 0.0)
    prefix = _dot(below_ref[...], multihot.astype(bf16)) + carry_ref[0:1]
    meta = jnp.zeros((tm, LANES), f32)
    for kk in range(TOP_K):
        rank = jnp.sum(jnp.where(lane == idxs[kk], prefix, 0.0), axis=-1, keepdims=True)
        meta = jnp.where(lane == kk, idxs[kk].astype(f32), meta)
        meta = jnp.where(lane == TOP_K + kk, rank, meta)
        meta = jnp.where(lane == 2 * TOP_K + kk, es[kk] / den, meta)
    meta_ref[...] = meta
    metat_ref[...] = meta.T[:metat_ref.shape[0]]
    carry = carry_ref[...] + jnp.sum(multihot, axis=0, keepdims=True)
    carry_ref[...] = carry
    cnt_ref[...] = carry


def _outproj(x, o_mla, o_f, o_b, z, gt1, sc2, sh2, w_o, dn_out_norm, norm2, w_router, b_router):
    B, S, D = x.shape
    tm = min(TM_OUT, S)
    nt = S // tm
    T = B * S
    nh = o_f.shape[2]
    wr_hi = w_router.astype(bf16)
    wr_lo = (w_router - wr_hi.astype(f32)).astype(bf16)
    zpad = jnp.zeros((D, LANES - N_EXPERTS), bf16)
    wr = jnp.concatenate([wr_hi, zpad, wr_lo, zpad], axis=1)
    br = jnp.pad(b_router, (0, LANES - N_EXPERTS)).reshape(1, LANES)
    below = jnp.tril(jnp.ones((tm, tm), bf16), -1)
    consts = [w_o.astype(bf16), dn_out_norm.reshape(1, -1), norm2.reshape(1, D), wr, br, below]

    def full(a):
        return pl.BlockSpec(a.shape, lambda b, i: (0,) * a.ndim)

    tok = lambda w: pl.BlockSpec((1, tm, w), lambda b, i: (b, i, 0))
    vec = pl.BlockSpec((1, 1, D), lambda b, i: (b, 0, 0))
    flat = lambda w: pl.BlockSpec((tm, w), lambda b, i: (b * nt + i, 0))
    return pl.pallas_call(
        _outproj_kernel,
        grid=(B, nt),
        in_specs=[tok(D), tok(o_mla.shape[2]), tok(nh), tok(nh), tok(nh), vec, vec, vec] + [full(a) for a in consts],
        out_specs=[tok(D), pl.BlockSpec((tm * ROW_SLABS, LANES), lambda b, i: (b * nt + i, 0)), flat(LANES),
                   pl.BlockSpec((2 * TOP_K, tm), lambda b, i: (0, b * nt + i)),
                   pl.BlockSpec((8, LANES), lambda b, i: (0, 0))],
        out_shape=[jax.ShapeDtypeStruct((B, S, D), f32), jax.ShapeDtypeStruct((T * ROW_SLABS, LANES), u32),
                   jax.ShapeDtypeStruct((T, LANES), f32), jax.ShapeDtypeStruct((2 * TOP_K, T), f32),
                   jax.ShapeDtypeStruct((8, LANES), f32)],
        scratch_shapes=[pltpu.VMEM((8, LANES), f32)],
        compiler_params=_cparams(("arbitrary", "arbitrary")),
        name="outproj",
    )(x, o_mla, o_f, o_b, z, gt1, sc2, sh2, *consts)


DMA_UNROLL = 2


def _row_copy(src_ref, s, dst_ref, d, sem):
    return pltpu.make_async_copy(src_ref.at[pl.ds(pl.multiple_of(s * ROW_SLABS, ROW_SLABS), ROW_SLABS)],
                                 dst_ref.at[pl.ds(pl.multiple_of(d * ROW_SLABS, ROW_SLABS), ROW_SLABS)], sem)


def _dispatch_kernel(ends_ref, dest_ref, hp_ref, xb_ref, zero_ref, sem, zsem):
    n = dest_ref.shape[0]

    @pl.when(pl.program_id(0) == 0)
    def _():
        zero_ref[...] = jnp.zeros_like(zero_ref)
        rows = zero_ref.shape[0]

        def zero_copy(e):
            start = pl.multiple_of(ends_ref[e] * ROW_SLABS - rows, ROW_SLABS * 8)
            return pltpu.make_async_copy(zero_ref, xb_ref.at[pl.ds(start, rows)], zsem)

        nb = xb_ref.shape[0] // rows
        n_used = ends_ref[N_EXPERTS]

        def tail_copy(j):
            return pltpu.make_async_copy(zero_ref, xb_ref.at[pl.ds(j * rows, rows)], zsem)

        for e in range(N_EXPERTS):
            @pl.when(ends_ref[e] >= 0)
            def _():
                zero_copy(e).start()
        for j in range(nb - N_EXPERTS, nb):
            @pl.when(j >= n_used)
            def _():
                tail_copy(j).start()
        for e in range(N_EXPERTS):
            @pl.when(ends_ref[e] >= 0)
            def _():
                zero_copy(e).wait()
        for j in range(nb - N_EXPERTS, nb):
            @pl.when(j >= n_used)
            def _():
                tail_copy(j).wait()

    def issue(t, carry):
        for kk in range(TOP_K):
            _row_copy(hp_ref, t, xb_ref, dest_ref[t * TOP_K + kk], sem).start(priority=kk % 2)
        return carry

    lax.fori_loop(0, n // TOP_K, issue, 0, unroll=DMA_UNROLL)
    for _ in range(TOP_K):
        pltpu.make_async_copy(hp_ref, xb_ref.at[pl.ds(0, hp_ref.shape[0])], sem).wait()


def _dispatch(hp, dest, ends, P):
    T = hp.shape[0] // ROW_SLABS
    tt = min(TT_DISPATCH, T)
    grid_spec = pltpu.PrefetchScalarGridSpec(
        num_scalar_prefetch=1,
        grid=(T // tt,),
        in_specs=[pl.BlockSpec((tt * TOP_K,), lambda i, ends: (i,), memory_space=pltpu.SMEM),
                  pl.BlockSpec((tt * ROW_SLABS, LANES), lambda i, ends: (i, 0))],
        out_specs=pl.BlockSpec(memory_space=pl.ANY),
        scratch_shapes=[pltpu.VMEM((MOE_BLOCK * ROW_SLABS, LANES), u32),
                        pltpu.SemaphoreType.DMA(()), pltpu.SemaphoreType.DMA(())],
    )
    return pl.pallas_call(
        _dispatch_kernel,
        grid_spec=grid_spec,
        out_shape=jax.ShapeDtypeStruct((P * ROW_SLABS, LANES), u32),
        compiler_params=pltpu.CompilerParams(dimension_semantics=("arbitrary",), has_side_effects=True),
        name="dispatch",
    )(ends, dest, hp)


def _expert_kernel(be_ref, nb_ref, x_ref, wg_ref, bg_ref, wu_ref, bu_ref, wd_ref, bd_ref, y_ref):
    b = pl.program_id(0)

    @pl.when(b < nb_ref[0])
    def _():
        subs = [pl.ds(h * MOE_SUB * ROW_SLABS, MOE_SUB * ROW_SLABS) for h in range(MOE_BLOCK // MOE_SUB)]
        xs = [_unpack_bf16_pairs(_load_rows(x_ref.at[sl], MOE_SUB)).astype(bf16) for sl in subs]
        gts = [jnp.minimum(_dot(x, wg_ref[0]) + bg_ref[0], SWIGLU_LIMIT) for x in xs]
        ups = [jnp.clip(_dot(x, wu_ref[0]) + bu_ref[0], -SWIGLU_LIMIT, SWIGLU_LIMIT) for x in xs]
        acts = [((up + 1.0) * gt * _sigmoid(SWIGLU_ALPHA * gt)).astype(bf16) for gt, up in zip(gts, ups)]
        ys = [_dot(act, wd_ref[0]) + bd_ref[0] for act in acts]
        for sl, y in zip(subs, ys):
            _store_rows(y_ref.at[sl], _pack_bf16_pairs(y))

    @pl.when(b >= nb_ref[0])
    def _():
        y_ref[...] = jnp.zeros_like(y_ref)


def _experts(xb, block_expert, n_used, wg, bg, wu, bu, wd, bd):
    E, D, F = wg.shape
    nb = xb.shape[0] // (MOE_BLOCK * ROW_SLABS)
    wspec = lambda r, c: pl.BlockSpec((1, r, c), lambda b, be, nu: (be[b], 0, 0))
    xspec = pl.BlockSpec((MOE_BLOCK * ROW_SLABS, LANES), lambda b, be, nu: (b, 0))
    grid_spec = pltpu.PrefetchScalarGridSpec(
        num_scalar_prefetch=2,
        grid=(nb,),
        in_specs=[xspec, wspec(D, F), wspec(1, F), wspec(D, F), wspec(1, F), wspec(F, D), wspec(1, D)],
        out_specs=xspec,
    )
    return pl.pallas_call(
        _expert_kernel,
        grid_spec=grid_spec,
        out_shape=jax.ShapeDtypeStruct(xb.shape, u32),
        compiler_params=_cparams(("arbitrary",)),
        name="experts",
    )(block_expert, n_used, xb, wg, bg.reshape(E, 1, F), wu, bu.reshape(E, 1, F), wd, bd.reshape(E, 1, D))


def _combine_kernel(dest_ref, dnext_ref, yb_ref, meta_ref, x1_ref, gt_ref, o_ref, buf_ref, sem):
    n = dest_ref.shape[0]
    tt = n // TOP_K
    g = pl.program_id(0)
    slot = g % 2

    def gather(idx_ref, s):
        def issue(t, carry):
            for kk in range(TOP_K):
                _row_copy(yb_ref, idx_ref[t * TOP_K + kk], buf_ref.at[s, kk], t, sem.at[s]).start(priority=kk % 2)
            return carry

        lax.fori_loop(0, tt, issue, 0, unroll=DMA_UNROLL)

    @pl.when(g == 0)
    def _():
        gather(dest_ref, slot)

    @pl.when(g + 1 < pl.num_programs(0))
    def _():
        gather(dnext_ref, 1 - slot)

    for kk in range(TOP_K):
        pltpu.make_async_copy(yb_ref.at[pl.ds(0, tt * ROW_SLABS)], buf_ref.at[slot, kk], sem.at[slot]).wait()
    meta = meta_ref[...]
    moe = jnp.zeros((tt, x1_ref.shape[2]), f32)
    for kk in range(TOP_K):
        gate = meta[:, 2 * TOP_K + kk: 2 * TOP_K + kk + 1]
        moe = moe + gate * _unpack_bf16_pairs(_load_rows(buf_ref.at[slot, kk], tt))
    o_ref[0] = x1_ref[0] + gt_ref[0] * moe


def _combine(yb, dest, meta, x1, gt2):
    B, S, D = x1.shape
    tt = min(TT_COMBINE, S)
    nt = S // tt
    ng = B * nt
    return pl.pallas_call(
        _combine_kernel,
        grid=(ng,),
        in_specs=[pl.BlockSpec((tt * TOP_K,), lambda g: (g,), memory_space=pltpu.SMEM),
                  pl.BlockSpec((tt * TOP_K,), lambda g: (jnp.minimum(g + 1, ng - 1),), memory_space=pltpu.SMEM),
                  pl.BlockSpec(memory_space=pl.ANY),
                  pl.BlockSpec((tt, LANES), lambda g: (g, 0)),
                  pl.BlockSpec((1, tt, D), lambda g: (g // nt, g % nt, 0)),
                  pl.BlockSpec((1, 1, D), lambda g: (g // nt, 0, 0))],
        out_specs=pl.BlockSpec((1, tt, D), lambda g: (g // nt, g % nt, 0)),
        out_shape=jax.ShapeDtypeStruct((B, S, D), f32),
        scratch_shapes=[pltpu.VMEM((2, TOP_K, tt * ROW_SLABS, LANES), u32), pltpu.SemaphoreType.DMA((2,))],
        compiler_params=_cparams(("arbitrary",)),
        name="combine",
    )(dest, dest, yb, meta, x1, gt2)


def _moe(hp, meta, meta_t, cnt, x1, gt2, ew):
    T = meta.shape[0]
    TK_ = T * TOP_K
    nb = -(-TK_ // MOE_BLOCK) + N_EXPERTS
    P = nb * MOE_BLOCK
    counts = cnt[0, :N_EXPERTS].astype(i32)
    padded = (counts + MOE_BLOCK - 1) // MOE_BLOCK * MOE_BLOCK
    cum_padded = jnp.cumsum(padded)
    pstart = cum_padded - padded
    top_idx = meta_t[:TOP_K].astype(i32)
    rank = meta_t[TOP_K:].astype(i32)
    base = jnp.zeros_like(rank)
    for e in range(N_EXPERTS):
        base = jnp.where(top_idx == e, pstart[e], base)
    dest = (base + rank).T.reshape(-1)
    block_start = jnp.arange(nb, dtype=i32) * MOE_BLOCK
    block_expert = jnp.minimum(jnp.sum((cum_padded[None, :] <= block_start[:, None]).astype(i32), axis=1),
                               N_EXPERTS - 1)
    n_used = (cum_padded[-1:] // MOE_BLOCK).astype(i32)
    ends = jnp.concatenate([jnp.where(padded > 0, cum_padded, -1).astype(i32), n_used])
    xb = _dispatch(hp, dest, ends, P)
    yb = _experts(xb, block_expert, n_used, *ew)
    return _combine(yb, dest, meta, x1, gt2)


def _layer(x, c, p, pw, ew, rope):
    B, S, D = x.shape
    mod = _ada(c, p["w_ada"], p["b_ada"]).reshape(B, 6, 1, D)
    sh1, sc1, gt1, sh2, sc2, gt2 = (mod[:, j] for j in range(6))
    q, k, v, dq, dk, dv, dkt, z, gcol, grow = _inproj(
        x, sh1, sc1, p["norm1"], p["q_a_norm"], p["kv_a_norm"], pw, rope)
    o_mla = _attention(q, k, v)
    o_f, o_b = _gdn(dq, dk, dv, dkt, gcol, grow)
    x1, hp, meta, meta_t, cnt = _outproj(x, o_mla, o_f, o_b, z, gt1, sc2, sh2, p["w_o"], p["dn_out_norm"],
                                         p["norm2"], p["w_router"], p["b_router"])
    return _moe(hp, meta, meta_t, cnt, x1, gt2, ew)


def kernel(x_prompt, x_sample, c_prompt, c_sample, w_ada, b_ada, norm1, w_in, q_a_norm, w_q_b, kv_a_norm, w_kv_b, q_norm, k_norm, dn_conv, dn_a_log, dn_dt_bias, dn_out_norm, w_o, norm2, w_router, b_router, w_gate, b_gate, w_up, b_up, w_down, b_down):
    y_prompt, y_sample = x_prompt, x_sample
    depth = w_ada.shape[0]
    for l in range(depth):
        p = {"w_ada": w_ada[l], "b_ada": b_ada[l], "norm1": norm1[l], "q_a_norm": q_a_norm[l],
             "kv_a_norm": kv_a_norm[l], "dn_out_norm": dn_out_norm[l], "w_o": w_o[l], "norm2": norm2[l],
             "w_router": w_router[l], "b_router": b_router[l]}
        pw = _prep_weights(w_in[l], w_q_b[l], w_kv_b[l], q_norm[l], k_norm[l], dn_conv[l], dn_a_log[l],
                           dn_dt_bias[l])
        ew = (w_gate[l].astype(bf16), b_gate[l], w_up[l].astype(bf16), b_up[l], w_down[l].astype(bf16), b_down[l])
        y_prompt = _layer(y_prompt, c_prompt, p, pw, ew, _rope_tables(y_prompt.shape[1]))
        y_sample = _layer(y_sample, c_sample, p, pw, ew, _rope_tables(y_sample.shape[1]))
    return (y_prompt, y_sample)
```
